```python
import functools
import jax, jax.numpy as jnp
from jax import lax
import numpy as np

D_MODEL = 1024
BATCH = 2
SEQ = 8192
DEPTH = 1
DEC_BATCH = 32
DEC_SEQ = 1
PAST_LEN = 8192
PAGE_SIZE = 128

N_META = 16
D_RWKV = D_MODEL // 2
RWKV_HEAD = 64
H_RWKV = D_RWKV // RWKV_HEAD
LORA_DECAY = 64
LORA_A = 64
LORA_GATE = 128
D_FOX = D_MODEL // 2
FOX_HEAD = 64
H_FOX = D_FOX // FOX_HEAD
D_FF = 4 * D_MODEL
Q_BLOCK = 128
D_SHIFT = 3 * D_RWKV + LORA_DECAY + LORA_A + LORA_GATE
D_FOX_IN = 3 * D_FOX + H_FOX
D_GATES = 2 * D_MODEL
D_IN = D_SHIFT + D_FOX_IN + D_GATES
RMS_EPS = 1e-6
GN_EPS = 64e-5

kernel_name = 'rwkv7_fox_gated_hybrid_step'


def rms_norm(x, g):
    xf = x.astype(jnp.float32)
    y = xf * lax.rsqrt(jnp.mean(xf * xf, axis=-1, keepdims=True) + RMS_EPS)
    return (y * g.astype(jnp.float32)).astype(x.dtype)


def wkv_scan(r, decay, k, v, kk, a, state0):
    def step(state, inp):
        r_t, w_t, k_t, v_t, kk_t, a_t = inp
        s_kk = jnp.einsum('bhvk,bhk->bhv', state, kk_t)
        state = (state * w_t[:, :, None, :]
                 - s_kk[..., None] * (kk_t * a_t)[:, :, None, :]
                 + v_t[..., None] * k_t[:, :, None, :])
        return state, jnp.einsum('bhvk,bhk->bhv', state, r_t)
    xs = tuple(jnp.moveaxis(z, 1, 0) for z in (r, decay, k, v, kk, a))
    state, ys = lax.scan(step, state0, xs)
    return jnp.moveaxis(ys, 0, 1), state


def rwkv_time_mix(u_r, shift_prev, wkv0, lw):
    f32 = jnp.float32
    bsz, t = u_r.shape[:2]
    prev = jnp.concatenate([shift_prev[:, None, :].astype(u_r.dtype), u_r[:, :-1]], axis=1)
    us = u_r + lw['mu_shift'] * (prev - u_r)
    cuts = [D_RWKV, 2 * D_RWKV, 3 * D_RWKV, 3 * D_RWKV + LORA_DECAY, 3 * D_RWKV + LORA_DECAY + LORA_A]
    r, k, v, zw, za, zg = jnp.split(us, cuts, axis=-1)
    w_raw = -jax.nn.softplus(-(lw['w0'] + jnp.tanh(zw) @ lw['w_decay_up']).astype(f32)) - 0.5
    decay = jnp.exp(-jnp.exp(w_raw))
    a = jax.nn.sigmoid((lw['a0'] + za @ lw['w_a_up']).astype(f32))
    g = (jax.nn.sigmoid(zg) @ lw['w_g_up']).astype(f32)
    heads = lambda z: z.astype(f32).reshape(bsz, t, H_RWKV, RWKV_HEAD)
    r, k, v, decay, a = heads(r), heads(k), heads(v), heads(decay), heads(a)
    k_k = lw['k_k'].astype(f32).reshape(H_RWKV, RWKV_HEAD)
    k_a = lw['k_a'].astype(f32).reshape(H_RWKV, RWKV_HEAD)
    kk = k * k_k
    kk = kk / jnp.maximum(jnp.sqrt(jnp.sum(kk * kk, axis=-1, keepdims=True)), 1e-12)
    k = k * (1.0 + (a - 1.0) * k_a)
    y, wkv = wkv_scan(r, decay, k, v, kk, a, wkv0.astype(f32))
    mu = jnp.mean(y, axis=-1, keepdims=True)
    var = jnp.mean(jnp.square(y - mu), axis=-1, keepdims=True)
    y = ((y - mu) * lax.rsqrt(var + GN_EPS)).reshape(bsz, t, D_RWKV)
    y = y * lw['ln_x_w'].astype(f32) + lw['ln_x_b'].astype(f32)
    bonus = jnp.sum(r * k * lw['r_k'].astype(f32), axis=-1, keepdims=True) * v
    out = (y + bonus.reshape(bsz, t, D_RWKV)) * g
    return out.astype(u_r.dtype), u_r[:, -1], wkv.astype(wkv0.dtype)


def _fox_block(q, cq, pq, k, v, ck, pk):
    s = jnp.einsum('bhqd,bhkd->bhqk', q, k).astype(jnp.float32) * (FOX_HEAD ** -0.5)
    s = s + cq[..., :, None] - ck[..., None, :]
    s = jnp.where(pk[None, :] <= pq[:, None], s, -jnp.inf)
    p = jax.nn.softmax(s, axis=-1)
    return jnp.einsum('bhqk,bhkd->bhqd', p.astype(v.dtype), v)


def fox_attention(q, cq, pq, k, v, ck, pk):
    bsz, nh, nq, dh = q.shape
    if nq <= Q_BLOCK:
        return _fox_block(q, cq, pq, k, v, ck, pk)
    nb = -(-nq // Q_BLOCK)
    pad = nb * Q_BLOCK - nq
    qb = jnp.pad(q, ((0, 0), (0, 0), (0, pad), (0, 0))).reshape(bsz, nh, nb, Q_BLOCK, dh).transpose(2, 0, 1, 3, 4)
    cqb = jnp.pad(cq, ((0, 0), (0, 0), (0, pad))).reshape(bsz, nh, nb, Q_BLOCK).transpose(2, 0, 1, 3)
    pqb = jnp.pad(pq, (0, pad)).reshape(nb, Q_BLOCK)
    out = lax.map(lambda blk: _fox_block(blk[0], blk[1], blk[2], k, v, ck, pk), (qb, cqb, pqb))
    out = out.transpose(1, 2, 0, 3, 4).reshape(bsz, nh, nb * Q_BLOCK, dh)
    return out[:, :, :nq]


def attend_prompt(q, k, v, lf):
    bsz, t = q.shape[:2]
    pos = jnp.arange(t, dtype=jnp.int32)
    c = jnp.cumsum(lf, axis=1).transpose(0, 2, 1)
    qh, kh, vh = q.transpose(0, 2, 1, 3), k.transpose(0, 2, 1, 3), v.transpose(0, 2, 1, 3)
    m = N_META
    o_meta = fox_attention(qh[:, :, :m], c[:, :, :m], pos[:m], kh[:, :, :m], vh[:, :, :m], c[:, :, :m], pos[:m])
    o_real = fox_attention(qh[:, :, m:], c[:, :, m:], pos[m:], kh, vh, c, pos)
    o = jnp.concatenate([o_meta, o_real], axis=2)
    return o.transpose(0, 2, 1, 3).reshape(bsz, t, D_FOX)


def attend_sample(k_pool, v_pool, lf_pool, page_table, q, k, v, lf):
    nb, n_pages = page_table.shape
    past = n_pages * k_pool.shape[1]
    t = q.shape[1]
    k_past = k_pool[page_table].reshape(nb, past, H_FOX, FOX_HEAD)
    v_past = v_pool[page_table].reshape(nb, past, H_FOX, FOX_HEAD)
    lf_past = lf_pool[page_table].reshape(nb, past, H_FOX).astype(jnp.float32)
    k_all = jnp.concatenate([k_past, k.astype(k_past.dtype)], axis=1).transpose(0, 2, 1, 3)
    v_all = jnp.concatenate([v_past, v.astype(v_past.dtype)], axis=1).transpose(0, 2, 1, 3)
    c = jnp.cumsum(jnp.concatenate([lf_past, lf], axis=1), axis=1).transpose(0, 2, 1)
    pos = jnp.arange(past + t, dtype=jnp.int32)
    o = fox_attention(q.transpose(0, 2, 1, 3).astype(k_all.dtype), c[:, :, past:], pos[past:], k_all, v_all, c, pos)
    return o.transpose(0, 2, 1, 3).reshape(nb, t, D_FOX).astype(q.dtype)


def decoder_layer(x, shift_prev, wkv0, attend, lw):
    bsz, t = x.shape[:2]
    h = rms_norm(x, lw['norm_mix'])
    u = h @ lw['w_in']
    u_r, u_f, u_g = jnp.split(u, [D_SHIFT, D_SHIFT + D_FOX_IN], axis=-1)
    o_r, new_shift, new_wkv = rwkv_time_mix(u_r, shift_prev, wkv0, lw)
    q, k, v, f_logit = jnp.split(u_f, [D_FOX, 2 * D_FOX, 3 * D_FOX], axis=-1)
    q = q.reshape(bsz, t, H_FOX, FOX_HEAD)
    k = k.reshape(bsz, t, H_FOX, FOX_HEAD)
    v = v.reshape(bsz, t, H_FOX, FOX_HEAD)
    lf = jax.nn.log_sigmoid((f_logit + lw['b_forget']).astype(jnp.float32))
    o_f = attend(q, k, v, lf)
    g_r, g_f = jnp.split(u_g, 2, axis=-1)
    merged = jax.nn.sigmoid(g_r) * (o_r @ lw['w_br_rwkv']) + jax.nn.sigmoid(g_f) * (o_f @ lw['w_br_fox'])
    x = x + merged @ lw['w_out']
    h2 = rms_norm(x, lw['norm_ffn'])
    x = x + jnp.square(jax.nn.relu(h2 @ lw['w_ffn_up'])) @ lw['w_ffn_down']
    return x, (new_shift, new_wkv, k, v, lf.astype(x.dtype))


def setup_inputs(seed: int = 0) -> dict:
    key = jax.random.key(seed)
    ks = jax.random.split(key, 32)
    f32 = jnp.float32
    nrm = lambda kk, shape, s: s * jax.random.normal(kk, shape, f32)
    n_pages = PAST_LEN // PAGE_SIZE
    n_used = DEC_BATCH * n_pages
    n_pool = n_used + max(1, n_used // 4)
    perm = jax.random.permutation(ks[0], n_pool)
    page_table = perm[:n_used].reshape(DEC_BATCH, n_pages).astype(jnp.int32)
    L = DEPTH
    return {
        'x_prompt': nrm(ks[1], (BATCH, SEQ, D_MODEL), 1.0),
        'x_sample': nrm(ks[2], (DEC_BATCH, DEC_SEQ, D_MODEL), 1.0),
        'state_shift': nrm(ks[3], (L, DEC_BATCH, D_SHIFT), 1.0),
        'state_wkv': nrm(ks[4], (L, DEC_BATCH, H_RWKV, RWKV_HEAD, RWKV_HEAD), 0.3),
        'cache_k': nrm(ks[5], (L, n_pool, PAGE_SIZE, H_FOX, FOX_HEAD), 1.0),
        'cache_v': nrm(ks[6], (L, n_pool, PAGE_SIZE, H_FOX, FOX_HEAD), 1.0),
        'cache_logf': jax.nn.log_sigmoid(3.0 + nrm(ks[7], (L, n_pool, PAGE_SIZE, H_FOX), 1.0)),
        'page_table': page_table,
        'meta_tokens': nrm(ks[8], (N_META, D_MODEL), 1.0),
        'norm_mix': 1.0 + nrm(ks[9], (L, D_MODEL), 0.02),
        'w_in': nrm(ks[10], (L, D_MODEL, D_IN), D_MODEL ** -0.5),
        'mu_shift': jax.random.uniform(ks[11], (L, D_SHIFT), f32),
        'w0': -2.0 + nrm(ks[12], (L, D_RWKV), 0.5),
        'w_decay_up': nrm(ks[13], (L, LORA_DECAY, D_RWKV), LORA_DECAY ** -0.5),
        'a0': nrm(ks[14], (L, D_RWKV), 0.1),
        'w_a_up': nrm(ks[15], (L, LORA_A, D_RWKV), LORA_A ** -0.5),
        'w_g_up': nrm(ks[16], (L, LORA_GATE, D_RWKV), LORA_GATE ** -0.5),
        'k_k': 0.85 + nrm(ks[17], (L, D_RWKV), 0.05),
        'k_a': 1.0 + nrm(ks[18], (L, D_RWKV), 0.05),
        'r_k': nrm(ks[19], (L, H_RWKV, RWKV_HEAD), 0.1),
        'ln_x_w': 1.0 + nrm(ks[20], (L, D_RWKV), 0.02),
        'ln_x_b': nrm(ks[21], (L, D_RWKV), 0.02),
        'b_forget': jax.random.uniform(ks[22], (L, H_FOX), f32, 1.0, 5.0),
        'w_br_rwkv': nrm(ks[23], (L, D_RWKV, D_MODEL), D_RWKV ** -0.5),
        'w_br_fox': nrm(ks[24], (L, D_FOX, D_MODEL), D_FOX ** -0.5),
        'w_out': nrm(ks[25], (L, D_MODEL, D_MODEL), D_MODEL ** -0.5),
        'norm_ffn': 1.0 + nrm(ks[26], (L, D_MODEL), 0.02),
        'w_ffn_up': nrm(ks[27], (L, D_MODEL, D_FF), D_MODEL ** -0.5),
        'w_ffn_down': nrm(ks[28], (L, D_FF, D_MODEL), D_FF ** -0.5),
        'norm_final': 1.0 + nrm(ks[29], (D_MODEL,), 0.02),
    }


def reference(x_prompt, x_sample, state_shift, state_wkv, cache_k, cache_v, cache_logf, page_table,
              meta_tokens, norm_mix, w_in, mu_shift, w0, w_decay_up, a0, w_a_up, w_g_up, k_k, k_a, r_k,
              ln_x_w, ln_x_b, b_forget, w_br_rwkv, w_br_fox, w_out, norm_ffn, w_ffn_up, w_ffn_down,
              norm_final):
    bsz = x_prompt.shape[0]
    meta = jnp.broadcast_to(meta_tokens[None].astype(x_prompt.dtype), (bsz, N_META, D_MODEL))
    xp = jnp.concatenate([meta, x_prompt], axis=1)
    xs = x_sample
    p_shift, p_wkv, p_k, p_v, p_lf = [], [], [], [], []
    s_shift, s_wkv, s_k, s_v, s_lf = [], [], [], [], []
    for l in range(DEPTH):
        lw = {
            'norm_mix': norm_mix[l], 'w_in': w_in[l], 'mu_shift': mu_shift[l], 'w0': w0[l],
            'w_decay_up': w_decay_up[l], 'a0': a0[l], 'w_a_up': w_a_up[l], 'w_g_up': w_g_up[l],
            'k_k': k_k[l], 'k_a': k_a[l], 'r_k': r_k[l], 'ln_x_w': ln_x_w[l], 'ln_x_b': ln_x_b[l],
            'b_forget': b_forget[l], 'w_br_rwkv': w_br_rwkv[l], 'w_br_fox': w_br_fox[l],
            'w_out': w_out[l], 'norm_ffn': norm_ffn[l], 'w_ffn_up': w_ffn_up[l],
            'w_ffn_down': w_ffn_down[l],
        }
        shift0 = jnp.zeros((bsz, D_SHIFT), xp.dtype)
        wkv0 = jnp.zeros((bsz, H_RWKV, RWKV_HEAD, RWKV_HEAD), xp.dtype)
        xp, (sh, wk, kr, vr, lr) = decoder_layer(xp, shift0, wkv0, attend_prompt, lw)
        p_shift.append(sh); p_wkv.append(wk); p_k.append(kr); p_v.append(vr); p_lf.append(lr)
        attend = functools.partial(attend_sample, cache_k[l], cache_v[l], cache_logf[l], page_table)
        xs, (sh, wk, kr, vr, lr) = decoder_layer(xs, state_shift[l], state_wkv[l], attend, lw)
        s_shift.append(sh); s_wkv.append(wk); s_k.append(kr); s_v.append(vr); s_lf.append(lr)
    y_prompt = rms_norm(xp[:, N_META:], norm_final)
    y_sample = rms_norm(xs, norm_final)
    return (y_prompt, y_sample,
            jnp.stack(p_shift), jnp.stack(p_wkv), jnp.stack(p_k), jnp.stack(p_v), jnp.stack(p_lf),
            jnp.stack(s_shift), jnp.stack(s_wkv), jnp.stack(s_k), jnp.stack(s_v), jnp.stack(s_lf))
```

```python
import functools

import jax
import jax.numpy as jnp
from jax import lax
from jax.experimental import pallas as pl
from jax.experimental.pallas import tpu as pltpu

f32 = jnp.float32
bf16 = jnp.bfloat16

D_MODEL = 1024
N_META = 16
D_RWKV = 512
HEAD = 64
N_HEAD = 8
D_SHIFT = 1792
D_FF = 4096
RMS_EPS = 1e-6
GN_EPS = 64e-5
PAGE = 128

FRONT_PAD = 112
ROW0 = FRONT_PAD + N_META
CHUNK = 64
NEG = -1e30

COL_R = 0
COL_F = 1792
COL_G = 2048
COL_Q = 4096
D_INP = 5632

VMEM_LIMIT = 56 * 1024 * 1024

NN = (((1,), (0,)), ((), ()))
NT = (((1,), (1,)), ((), ()))
TN = (((0,), (0,)), ((), ()))


def _dot(a, b, dims=NN):
    return lax.dot_general(a, b, dims, preferred_element_type=f32)


def _split2(x):
    hi = x.astype(bf16)
    lo = (x - hi.astype(f32)).astype(bf16)
    return hi, lo


def _split3(x):
    hi = x.astype(bf16)
    r1 = x - hi.astype(f32)
    mid = r1.astype(bf16)
    lo = (r1 - mid.astype(f32)).astype(bf16)
    return hi, mid, lo


def _dot3(a, b, dims=NN):
    ah, al = _split2(a)
    bh, bl = _split2(b)
    return _dot(ah, bh, dims) + (_dot(ah, bl, dims) + _dot(al, bh, dims))


def _dot_onesr(x, ones_bf16, dims=NN):
    hi, mid, lo = _split3(x)
    return _dot(hi, ones_bf16, dims) + (_dot(mid, ones_bf16, dims) + _dot(lo, ones_bf16, dims))


def _dot_onesl(ones_bf16, x, dims=NN):
    hi, mid, lo = _split3(x)
    return _dot(ones_bf16, hi, dims) + (_dot(ones_bf16, mid, dims) + _dot(ones_bf16, lo, dims))


def _params(sem, vmem=None):
    return pltpu.CompilerParams(dimension_semantics=sem, vmem_limit_bytes=vmem)


def _norm_matmul_kernel(x_ref, g_ref, w_ref, o_ref, h_ref):
    @pl.when(pl.program_id(1) == 0)
    def _():
        x = x_ref[...]
        ms = jnp.mean(x * x, axis=-1, keepdims=True)
        h_ref[...] = (x * lax.rsqrt(ms + RMS_EPS) * g_ref[...]).astype(bf16)

    o_ref[...] = _dot(h_ref[...], w_ref[...])


def norm_matmul(x, g, w, tm, tn):
    m, d = x.shape
    n = w.shape[1]
    return pl.pallas_call(
        _norm_matmul_kernel,
        grid=(m // tm, n // tn),
        in_specs=[pl.BlockSpec((tm, d), lambda i, j: (i, 0)),
                  pl.BlockSpec((1, d), lambda i, j: (0, 0)),
                  pl.BlockSpec((d, tn), lambda i, j: (0, j))],
        out_specs=pl.BlockSpec((tm, tn), lambda i, j: (i, j)),
        out_shape=jax.ShapeDtypeStruct((m, n), f32),
        scratch_shapes=[pltpu.VMEM((tm, d), bf16)],
        compiler_params=_params(("parallel", "arbitrary"), VMEM_LIMIT),
        name="norm_matmul",
    )(x, g, w)


def _prep_kernel(u_ref, up_ref, mu_ref, w0_ref, wd_ref, a0_ref, wa_ref, wg_ref, kk_ref, ka_ref, rk_ref,
                 bd_ref, r_o, lw_o, k_o, v_o, kkn_o, b_o, g_o, bonus_o):
    u = u_ref[0]
    up = up_ref[0]
    us = u + mu_ref[...] * (up - u)
    r = us[:, 0:512]
    k = us[:, 512:1024]
    v = us[:, 1024:1536]
    zw = us[:, 1536:1600]
    za = us[:, 1600:1664]
    zg = us[:, 1664:1792]
    bd = bd_ref[...]

    z = -(w0_ref[...] + _dot3(jnp.tanh(zw), wd_ref[...]))
    w_raw = -(jnp.maximum(z, 0.0) + jnp.log1p(jnp.exp(-jnp.abs(z)))) - 0.5
    lw = -jnp.exp(w_raw)
    a = jax.nn.sigmoid(a0_ref[...] + _dot3(za, wa_ref[...]))
    g = _dot3(jax.nn.sigmoid(zg), wg_ref[...])
    kk = k * kk_ref[...]
    ss = _dot_onesr(kk * kk, bd)
    kkn = kk / jnp.maximum(jnp.sqrt(ss), 1e-12)
    k2 = k * (1.0 + (a - 1.0) * ka_ref[...])
    b = kkn * a
    bonus = _dot_onesr(r * k2 * rk_ref[...], bd) * v

    for h in range(N_HEAD):
        sl = slice(HEAD * h, HEAD * (h + 1))
        r_o[0, h] = r[:, sl]
        lw_o[0, h] = lw[:, sl]
        k_o[0, h] = k2[:, sl]
        v_o[0, h] = v[:, sl]
        kkn_o[0, h] = kkn[:, sl]
        b_o[0, h] = b[:, sl]
        g_o[0, h] = g[:, sl]
        bonus_o[0, h] = bonus[:, sl]


def rwkv_prep(u3, up3, pr, tm):
    nb, t = u3.shape[0], u3.shape[1]
    row = lambda i, j: (i, j, 0)
    cst = lambda i, j: (0, 0)
    hm = jax.ShapeDtypeStruct((nb, N_HEAD, t, HEAD), f32)
    hm_spec = pl.BlockSpec((1, N_HEAD, tm, HEAD), lambda i, j: (i, 0, j, 0))
    vec = lambda n: pl.BlockSpec((1, n), cst)
    return pl.pallas_call(
        _prep_kernel,
        grid=(nb, t // tm),
        in_specs=[pl.BlockSpec((1, tm, D_SHIFT), row), pl.BlockSpec((1, tm, D_SHIFT), row),
                  vec(D_SHIFT), vec(512), pl.BlockSpec((64, 512), cst), vec(512), pl.BlockSpec((64, 512), cst),
                  pl.BlockSpec((128, 512), cst), vec(512), vec(512), vec(512), pl.BlockSpec((512, 512), cst)],
        out_specs=[hm_spec] * 8,
        out_shape=[hm] * 8,
        compiler_params=_params(("parallel", "parallel"), VMEM_LIMIT),
        name="rwkv_prep",
    )(u3, up3, pr["mu"], pr["w0"], pr["wd"], pr["a0"], pr["wa"], pr["wg"], pr["k_k"], pr["k_a"], pr["r_k"],
      pr["bd"])


def _tri_inv(a_strict, same_blk, eye):
    d = jnp.where(same_blk, a_strict, 0.0)
    lo = a_strict - d
    n1 = -d
    n2 = _dot3(n1, n1)
    n4 = _dot3(n2, n2)
    n8 = _dot3(n4, n4)
    td = _dot3(_dot3(eye + n1, eye + n2), _dot3(eye + n4, eye + n8))
    x1 = -_dot3(td, lo)
    x2 = _dot3(x1, x1)
    return _dot3(_dot3(eye + x1, eye + x2), td)


def _wkv_chunk_kernel(r_ref, lw_ref, k_ref, v_ref, kk_ref, b_ref, pw_ref, qy_ref, *, nc):
    ii = lax.broadcasted_iota(jnp.int32, (CHUNK, CHUNK), 0)
    jj = lax.broadcasted_iota(jnp.int32, (CHUNK, CHUNK), 1)
    incl = jj <= ii
    strict = jj < ii
    ones_incl = incl.astype(bf16)
    same_blk = (ii >> 4) == (jj >> 4)
    eye = (ii == jj).astype(f32)

    def body(c, carry):
        sl = pl.ds(pl.multiple_of(c * CHUNK, CHUNK), CHUNK)
        r = r_ref[0, sl, :]
        lw = lw_ref[0, sl, :]
        k = k_ref[0, sl, :]
        v = v_ref[0, sl, :]
        kk = kk_ref[0, sl, :]
        b = b_ref[0, sl, :]

        gcum = _dot_onesl(ones_incl, lw)
        gend = gcum[CHUNK - 1:CHUNK, :]
        kkt = kk * jnp.exp(gcum - lw)
        rt = r * jnp.exp(gcum)
        em = jnp.exp(-gcum)
        kh = k * em
        bh = b * em
        ec = jnp.exp(gend - gcum)
        kg = k * ec
        bg = b * ec

        lhs = jnp.concatenate([kkt, rt], axis=0)
        ab = _dot3(lhs, bh, NT)
        ak = _dot3(lhs, kh, NT)
        a_kb = jnp.where(strict, ab[:CHUNK], 0.0)
        a_rb = jnp.where(incl, ab[CHUNK:], 0.0)
        a_kk = jnp.where(strict, ak[:CHUNK], 0.0)
        a_rk = jnp.where(incl, ak[CHUNK:], 0.0)

        tinv = _tri_inv(a_kb, same_blk, eye)
        w1 = _dot3(tinv, kkt)
        uu = _dot3(tinv, _dot3(a_kk, v))
        p = eye * jnp.exp(gend) - _dot3(bg, w1, TN)
        q = _dot3(kg, v, TN) - _dot3(bg, uu, TN)
        wy = rt - _dot3(a_rb, w1)
        yl = _dot3(a_rk, v) - _dot3(a_rb, uu)

        pw_ref[0, c, 0:CHUNK, :] = p
        pw_ref[0, c, CHUNK:2 * CHUNK, :] = wy
        qy_ref[0, c, 0:CHUNK, :] = q
        qy_ref[0, c, CHUNK:2 * CHUNK, :] = yl
        return carry

    lax.fori_loop(0, nc, body, 0)


def wkv_chunks(r, lw, k, v, kk, b, nc):
    nbh, t, _ = r.shape
    nchunk = t // CHUNK
    in_spec = pl.BlockSpec((1, nc * CHUNK, HEAD), lambda i, j: (i, j, 0))
    out_spec = pl.BlockSpec((1, nc, 2 * CHUNK, HEAD), lambda i, j: (i, j, 0, 0))
    out = jax.ShapeDtypeStruct((nbh, nchunk, 2 * CHUNK, HEAD), f32)
    return pl.pallas_call(
        functools.partial(_wkv_chunk_kernel, nc=nc),
        grid=(nbh, nchunk // nc),
        in_specs=[in_spec] * 6,
        out_specs=[out_spec, out_spec],
        out_shape=[out, out],
        compiler_params=_params(("parallel", "parallel"), VMEM_LIMIT),
        name="wkv_chunks",
    )(r, lw, k, v, kk, b)


def _wkv_serial_kernel(pw_ref, qy_ref, y_ref, s_ref, st_ref, *, nbh, nc):
    @pl.when(pl.program_id(0) == 0)
    def _():
        st_ref[...] = jnp.zeros_like(st_ref)

    def body(c, carry):
        for i in range(nbh):
            z = _dot3(pw_ref[i, c], st_ref[i]) + qy_ref[i, c]
            st_ref[i] = z[:CHUNK]
            y_ref[i, pl.ds(pl.multiple_of(c * CHUNK, CHUNK), CHUNK), :] = z[CHUNK:]
        return carry

    lax.fori_loop(0, nc, body, 0)

    @pl.when(pl.program_id(0) == pl.num_programs(0) - 1)
    def _():
        s_ref[...] = st_ref[...]


def wkv_serial(pw, qy, nc):
    nbh, nchunk = pw.shape[0], pw.shape[1]
    blk = pl.BlockSpec((nbh, nc, 2 * CHUNK, HEAD), lambda j: (0, j, 0, 0))
    return pl.pallas_call(
        functools.partial(_wkv_serial_kernel, nbh=nbh, nc=nc),
        grid=(nchunk // nc,),
        in_specs=[blk, blk],
        out_specs=[pl.BlockSpec((nbh, nc * CHUNK, HEAD), lambda j: (0, j, 0)),
                   pl.BlockSpec((nbh, HEAD, HEAD), lambda j: (0, 0, 0))],
        out_shape=[jax.ShapeDtypeStruct((nbh, nchunk * CHUNK, HEAD), f32),
                   jax.ShapeDtypeStruct((nbh, HEAD, HEAD), f32)],
        scratch_shapes=[pltpu.VMEM((nbh, HEAD, HEAD), f32)],
        compiler_params=_params(("arbitrary",), VMEM_LIMIT),
        name="wkv_serial",
    )(pw, qy)


def _wkv_step_kernel(s_ref, r_ref, lw_ref, k_ref, kk_ref, b_ref, vc_ref, so_ref, y_ref):
    s = s_ref[0]
    skk = jnp.sum(s * kk_ref[0], axis=-1, keepdims=True)
    s1 = s * jnp.exp(lw_ref[0]) - skk * b_ref[0] + vc_ref[0] * k_ref[0]
    so_ref[0] = s1
    y_ref[0] = jnp.sum(s1 * r_ref[0], axis=-1, keepdims=True)


def wkv_step(s, r, lw, k, kk, b, vcol):
    n = s.shape[0]
    row = pl.BlockSpec((1, N_HEAD, 1, HEAD), lambda i: (i, 0, 0, 0))
    col = pl.BlockSpec((1, N_HEAD, HEAD, 1), lambda i: (i, 0, 0, 0))
    mat = pl.BlockSpec((1, N_HEAD, HEAD, HEAD), lambda i: (i, 0, 0, 0))
    return pl.pallas_call(
        _wkv_step_kernel,
        grid=(n,),
        in_specs=[mat, row, row, row, row, row, col],
        out_specs=[mat, col],
        out_shape=[jax.ShapeDtypeStruct(s.shape, f32), jax.ShapeDtypeStruct((n, N_HEAD, HEAD, 1), f32)],
        compiler_params=_params(("parallel",)),
        name="wkv_step",
    )(s, r, lw, k, kk, b, vcol)


def _post_kernel(y_ref, g_ref, bonus_ref, lnw_ref, lnb_ref, o_ref):
    parts = []
    for h in range(N_HEAD):
        y = y_ref[0, h]
        mu = jnp.mean(y, axis=-1, keepdims=True)
        yc = y - mu
        var = jnp.mean(yc * yc, axis=-1, keepdims=True)
        yn = yc * lax.rsqrt(var + GN_EPS)
        parts.append((yn * lnw_ref[h] + lnb_ref[h] + bonus_ref[0, h]) * g_ref[0, h])
    o_ref[0] = jnp.concatenate(parts, axis=-1)


def rwkv_post(y, g, bonus, lnw, lnb, tm):
    nb, _, t, _ = y.shape
    hm = pl.BlockSpec((1, N_HEAD, tm, HEAD), lambda i, j: (i, 0, j, 0))
    par = pl.BlockSpec((N_HEAD, 1, HEAD), lambda i, j: (0, 0, 0))
    return pl.pallas_call(
        _post_kernel,
        grid=(nb, t // tm),
        in_specs=[hm, hm, hm, par, par],
        out_specs=pl.BlockSpec((1, tm, D_RWKV), lambda i, j: (i, j, 0)),
        out_shape=jax.ShapeDtypeStruct((nb, t, D_RWKV), f32),
        compiler_params=_params(("parallel", "parallel"), VMEM_LIMIT),
        name="rwkv_post",
    )(y, g, bonus, lnw, lnb)


def _logf_kernel(fl_ref, b_ref, lf_ref, c_ref, *, nblk):
    ii = lax.broadcasted_iota(jnp.int32, (128, 128), 0)
    jj = lax.broadcasted_iota(jnp.int32, (128, 128), 1)
    upper = (ii <= jj).astype(bf16)
    carry = jnp.zeros((fl_ref.shape[0], 1), f32)
    for blk in range(nblk):
        sl = slice(128 * blk, 128 * (blk + 1))
        lf = jax.nn.log_sigmoid(fl_ref[:, sl] + b_ref[...])
        lf_ref[:, sl] = lf
        cs = _dot_onesr(lf, upper) + carry
        c_ref[:, sl] = cs
        carry = cs[:, 127:128]


def logf_cumsum(fl_t, bias):
    n, t = fl_t.shape
    out = jax.ShapeDtypeStruct((n, t), f32)
    return pl.pallas_call(
        functools.partial(_logf_kernel, nblk=t // 128),
        out_shape=[out, out],
        name="logf_cumsum",
    )(fl_t, bias)


def _fox_kernel(q_ref, k_ref, v_ref, ck_ref, o_ref, m_ref, l_ref, acc_ref, *, tq, tk):
    qi = pl.program_id(2)
    ki = pl.program_id(3)

    @pl.when(ki == 0)
    def _():
        m_ref[...] = jnp.full_like(m_ref, NEG)
        l_ref[...] = jnp.zeros_like(l_ref)
        acc_ref[...] = jnp.zeros_like(acc_ref)

    def step(masked):
        lane_lo = lax.broadcasted_iota(jnp.int32, (tq, 2 * HEAD), 1) < HEAD
        q = q_ref[0] * (HEAD ** -0.5)
        kb = k_ref[0].astype(bf16)
        vb = v_ref[0].astype(bf16)
        if masked:
            row = lax.broadcasted_iota(jnp.int32, (tq, tk), 0)
            col = lax.broadcasted_iota(jnp.int32, (tq, tk), 1)
            visible = col <= row
        alphas, pvs = [], []
        for e in range(2):
            qe = jnp.where(lane_lo == (e == 0), q, 0.0).astype(bf16)
            s = _dot(qe, kb, NT) - ck_ref[0, 0, e:e + 1, :]
            if masked:
                s = jnp.where(visible, s, NEG)
            m_old = m_ref[e]
            m_new = jnp.maximum(m_old, jnp.max(s, axis=-1, keepdims=True))
            alpha = jnp.exp(m_old - m_new)
            p = jnp.exp(s - m_new)
            l_ref[e] = alpha * l_ref[e] + jnp.sum(p, axis=-1, keepdims=True)
            m_ref[e] = m_new
            alphas.append(alpha)
            pvs.append(_dot(p.astype(bf16), vb))
        acc_ref[...] = (jnp.where(lane_lo, alphas[0], alphas[1]) * acc_ref[...]
                        + jnp.where(lane_lo, pvs[0], pvs[1]))

    @pl.when(ki < qi)
    def _():
        step(False)

    @pl.when(ki == qi)
    def _():
        step(True)
        lane_lo = lax.broadcasted_iota(jnp.int32, (tq, 2 * HEAD), 1) < HEAD
        o_ref[0] = acc_ref[...] / jnp.where(lane_lo, l_ref[0], l_ref[1])


def fox_prompt(u3, ck, tq):
    nb, t, _ = u3.shape
    nq = t // tq
    cq, ckk, cv = COL_Q // 128, (COL_Q + 512) // 128, (COL_Q + 1024) // 128
    return pl.pallas_call(
        functools.partial(_fox_kernel, tq=tq, tk=tq),
        grid=(nb, N_HEAD // 2, nq, nq),
        in_specs=[pl.BlockSpec((1, tq, 128), lambda b, h, i, j: (b, i, cq + h)),
                  pl.BlockSpec((1, tq, 128), lambda b, h, i, j: (b, jnp.minimum(i, j), ckk + h)),
                  pl.BlockSpec((1, tq, 128), lambda b, h, i, j: (b, jnp.minimum(i, j), cv + h)),
                  pl.BlockSpec((1, 1, 2, tq), lambda b, h, i, j: (b, h, 0, jnp.minimum(i, j)))],
        out_specs=pl.BlockSpec((1, tq, 128), lambda b, h, i, j: (b, i, h)),
        out_shape=jax.ShapeDtypeStruct((nb, t, 512), f32),
        scratch_shapes=[pltpu.VMEM((2, tq, 1), f32), pltpu.VMEM((2, tq, 1), f32),
                        pltpu.VMEM((tq, 2 * HEAD), f32)],
        compiler_params=_params(("parallel", "parallel", "parallel", "arbitrary"), VMEM_LIMIT),
        name="fox_prompt",
    )(u3, u3, u3, ck)


def _paged_kernel(pt_ref, qbd_ref, kn_ref, vn_ref, lfn_ref, kp_ref, vp_ref, lfp_ref, o_ref,
                  m_ref, l_ref, acc_ref, car_ref):
    p = pl.program_id(1)
    qbd = qbd_ref[0]

    @pl.when(p == 0)
    def _():
        m_ref[...] = jnp.sum(qbd * kn_ref[0], axis=-1, keepdims=True)
        l_ref[...] = jnp.ones_like(l_ref)
        acc_ref[...] = jnp.broadcast_to(vn_ref[0], acc_ref.shape)
        car_ref[...] = lfn_ref[0]

    ii = lax.broadcasted_iota(jnp.int32, (PAGE, PAGE), 0)
    jj = lax.broadcasted_iota(jnp.int32, (PAGE, PAGE), 1)
    later = (ii > jj).astype(bf16)
    lfp = lfp_ref[0]
    bias = _dot_onesr(lfp, later) + car_ref[...]
    s = _dot(qbd.astype(bf16), kp_ref[0].astype(bf16), NT) + bias
    m_old = m_ref[...]
    m_new = jnp.maximum(m_old, jnp.max(s, axis=-1, keepdims=True))
    alpha = jnp.exp(m_old - m_new)
    pr = jnp.exp(s - m_new)
    l_ref[...] = alpha * l_ref[...] + jnp.sum(pr, axis=-1, keepdims=True)
    acc_ref[...] = alpha * acc_ref[...] + _dot(pr.astype(bf16), vp_ref[0].astype(bf16))
    m_ref[...] = m_new
    car_ref[...] = car_ref[...] + jnp.sum(lfp, axis=-1, keepdims=True)

    @pl.when(p == pl.num_programs(1) - 1)
    def _():
        o = acc_ref[...] / l_ref[...]
        hrow = lax.broadcasted_iota(jnp.int32, o.shape, 0)
        hlane = lax.broadcasted_iota(jnp.int32, o.shape, 1) // HEAD
        o_ref[0] = jnp.sum(jnp.where(hrow == hlane, o, 0.0), axis=0, keepdims=True)


def fox_paged(page_table, qbd, knew, vnew, lfnew, kpool, vpool, lfpool_t):
    nb, npages = page_table.shape
    last = npages - 1
    cur = lambda b, p, pt: (b, 0, 0)
    pool = lambda b, p, pt: (pt[b, last - p], 0, 0)
    return pl.pallas_call(
        _paged_kernel,
        grid_spec=pltpu.PrefetchScalarGridSpec(
            num_scalar_prefetch=1,
            grid=(nb, npages),
            in_specs=[pl.BlockSpec((1, N_HEAD, 512), cur), pl.BlockSpec((1, 1, 512), cur),
                      pl.BlockSpec((1, 1, 512), cur), pl.BlockSpec((1, N_HEAD, 1), cur),
                      pl.BlockSpec((1, PAGE, 512), pool), pl.BlockSpec((1, PAGE, 512), pool),
                      pl.BlockSpec((1, N_HEAD, PAGE), pool)],
            out_specs=pl.BlockSpec((1, 1, 512), cur),
            scratch_shapes=[pltpu.VMEM((N_HEAD, 1), f32), pltpu.VMEM((N_HEAD, 1), f32),
                            pltpu.VMEM((N_HEAD, 512), f32), pltpu.VMEM((N_HEAD, 1), f32)]),
        out_shape=jax.ShapeDtypeStruct((nb, 1, 512), f32),
        compiler_params=_params(("parallel", "arbitrary")),
        name="fox_paged",
    )(page_table, qbd, knew, vnew, lfnew, kpool, vpool, lfpool_t)


def _merge_kernel(x_ref, or_ref, of_ref, ug_ref, wr_ref, wf_ref, wo_ref, o_ref):
    br = _dot(or_ref[...].astype(bf16), wr_ref[...])
    bf = _dot(of_ref[...].astype(bf16), wf_ref[...])
    ug = ug_ref[...]
    merged = jax.nn.sigmoid(ug[:, :D_MODEL]) * br + jax.nn.sigmoid(ug[:, D_MODEL:]) * bf
    o_ref[...] = x_ref[...] + _dot(merged.astype(bf16), wo_ref[...])


def merge_out(x, o_r, o_f, u, wr, wf, wo, tm):
    m = x.shape[0]
    row = lambda i: (i, 0)
    cst = lambda i: (0, 0)
    return pl.pallas_call(
        _merge_kernel,
        grid=(m // tm,),
        in_specs=[pl.BlockSpec((tm, D_MODEL), row), pl.BlockSpec((tm, 512), row), pl.BlockSpec((tm, 512), row),
                  pl.BlockSpec((tm, 2 * D_MODEL), lambda i: (i, COL_G // (2 * D_MODEL))),
                  pl.BlockSpec((512, D_MODEL), cst), pl.BlockSpec((512, D_MODEL), cst),
                  pl.BlockSpec((D_MODEL, D_MODEL), cst)],
        out_specs=pl.BlockSpec((tm, D_MODEL), row),
        out_shape=jax.ShapeDtypeStruct((m, D_MODEL), f32),
        compiler_params=_params(("parallel",), VMEM_LIMIT),
        name="merge_out",
    )(x, o_r, o_f, u, wr, wf, wo)


def _ffn_kernel(x_ref, g2_ref, gf_ref, wu_ref, wd_ref, o_ref, h_ref, acc_ref):
    j = pl.program_id(1)

    @pl.when(j == 0)
    def _():
        x = x_ref[...]
        ms = jnp.mean(x * x, axis=-1, keepdims=True)
        h_ref[...] = (x * lax.rsqrt(ms + RMS_EPS) * g2_ref[...]).astype(bf16)
        acc_ref[...] = jnp.zeros_like(acc_ref)

    hid = jnp.maximum(_dot(h_ref[...], wu_ref[...]), 0.0)
    acc_ref[...] += _dot((hid * hid).astype(bf16), wd_ref[...])

    @pl.when(j == pl.num_programs(1) - 1)
    def _():
        x2 = x_ref[...] + acc_ref[...]
        ms = jnp.mean(x2 * x2, axis=-1, keepdims=True)
        o_ref[...] = x2 * lax.rsqrt(ms + RMS_EPS) * gf_ref[...]


def ffn_final(x, g2, gf, wu, wd, tm, tf):
    m = x.shape[0]
    return pl.pallas_call(
        _ffn_kernel,
        grid=(m // tm, D_FF // tf),
        in_specs=[pl.BlockSpec((tm, D_MODEL), lambda i, j: (i, 0)),
                  pl.BlockSpec((1, D_MODEL), lambda i, j: (0, 0)), pl.BlockSpec((1, D_MODEL), lambda i, j: (0, 0)),
                  pl.BlockSpec((D_MODEL, tf), lambda i, j: (0, j)), pl.BlockSpec((tf, D_MODEL), lambda i, j: (j, 0))],
        out_specs=pl.BlockSpec((tm, D_MODEL), lambda i, j: (i, 0)),
        out_shape=jax.ShapeDtypeStruct((m, D_MODEL), f32),
        scratch_shapes=[pltpu.VMEM((tm, D_MODEL), bf16), pltpu.VMEM((tm, D_MODEL), f32)],
        compiler_params=_params(("parallel", "arbitrary"), VMEM_LIMIT),
        name="ffn_final",
    )(x, g2, gf, wu, wd)


def kernel(x_prompt, x_sample, state_shift, state_wkv, cache_k, cache_v, cache_logf, page_table, meta_tokens,
           norm_mix, w_in, mu_shift, w0, w_decay_up, a0, w_a_up, w_g_up, k_k, k_a, r_k, ln_x_w, ln_x_b, b_forget,
           w_br_rwkv, w_br_fox, w_out, norm_ffn, w_ffn_up, w_ffn_down, norm_final):
    assert norm_mix.shape[0] == 1, "single layer"
    nb, seq, _ = x_prompt.shape
    ns = x_sample.shape[0]
    tp = FRONT_PAD + N_META + seq
    tlen = N_META + seq
    tm = 640

    wi = w_in[0]
    wi = jnp.concatenate([wi[:, 0:1792], wi[:, 3328:3336], jnp.zeros((D_MODEL, 248), f32),
                          wi[:, 3336:5384], wi[:, 1792:3328]], axis=1).astype(bf16)
    row2 = lambda a: a.reshape(1, -1)
    lane_head = jnp.arange(512, dtype=jnp.int32) // HEAD
    pr = dict(mu=row2(mu_shift[0]), w0=row2(w0[0]), wd=w_decay_up[0], a0=row2(a0[0]), wa=w_a_up[0], wg=w_g_up[0],
              k_k=row2(k_k[0]), k_a=row2(k_a[0]), r_k=row2(r_k[0]),
              bd=(lane_head[:, None] == lane_head[None, :]).astype(bf16))
    lnw = ln_x_w[0].reshape(N_HEAD, 1, HEAD)
    lnb = ln_x_b[0].reshape(N_HEAD, 1, HEAD)
    g_mix, g_ffn, g_fin = row2(norm_mix[0]), row2(norm_ffn[0]), row2(norm_final)
    wr, wf, wo = w_br_rwkv[0].astype(bf16), w_br_fox[0].astype(bf16), w_out[0].astype(bf16)
    wu, wdn = w_ffn_up[0].astype(bf16), w_ffn_down[0].astype(bf16)

    meta = jnp.broadcast_to(meta_tokens[None], (nb, N_META, D_MODEL))
    xp = jnp.concatenate([jnp.zeros((nb, FRONT_PAD, D_MODEL), f32), meta, x_prompt], axis=1)
    xp2 = xp.reshape(nb * tp, D_MODEL)
    u2 = norm_matmul(xp2, g_mix, wi, tm, 512)
    u3 = u2.reshape(nb, tp, D_INP)
    us2 = norm_matmul(x_sample.reshape(ns, D_MODEL), g_mix, wi, ns, 512)

    ur = u3[:, :, :D_SHIFT]
    up3 = jnp.concatenate([jnp.zeros((nb, 1, D_SHIFT), f32), ur[:, :-1]], axis=1)
    r, lw, k2, v, kkn, b, g, bonus = rwkv_prep(u3, up3, pr, 320)
    flat = lambda a: a.reshape(nb * N_HEAD, tp, HEAD)
    pw, qy = wkv_chunks(flat(r), flat(lw), flat(k2), flat(v), flat(kkn), flat(b), 10)
    y, s_kv = wkv_serial(pw, qy, 5)
    o_r = rwkv_post(y.reshape(nb, N_HEAD, tp, HEAD), g, bonus, lnw, lnb, tm)
    new_wkv_p = jnp.swapaxes(s_kv, 1, 2).reshape(1, nb, N_HEAD, HEAD, HEAD)

    us3 = us2.reshape(1, ns, D_INP)
    sp = rwkv_prep(us3, state_shift[0].reshape(1, ns, D_SHIFT), pr, ns)
    rs, lws, ks, vs, kks, bs, gs, bonus_s = sp
    rowify = lambda a: jnp.transpose(a[0], (1, 0, 2)).reshape(ns, N_HEAD, 1, HEAD)
    s_new, ys = wkv_step(state_wkv[0], rowify(rs), rowify(lws), rowify(ks), rowify(kks), rowify(bs),
                         rowify(vs).reshape(ns, N_HEAD, HEAD, 1))
    ys_hm = jnp.transpose(ys.reshape(ns, N_HEAD, HEAD), (1, 0, 2))[None]
    o_r_s = rwkv_post(ys_hm, gs, bonus_s, lnw, lnb, ns)[0]

    fl_t = jnp.transpose(u3[:, :, COL_F:COL_F + N_HEAD], (0, 2, 1)).reshape(nb * N_HEAD, tp)
    bias_rows = jnp.tile(b_forget[0], nb).reshape(nb * N_HEAD, 1)
    lf_t, c_t = logf_cumsum(fl_t, bias_rows)
    pad_key = jnp.arange(tp, dtype=jnp.int32)[None, :] < FRONT_PAD
    ck = jnp.where(pad_key, -NEG, c_t).reshape(nb, N_HEAD // 2, 2, tp)
    o_f = fox_prompt(u3, ck, tm)

    qs = us2[:, COL_Q:COL_Q + 512] * (HEAD ** -0.5)
    head_row = jnp.arange(N_HEAD, dtype=jnp.int32)[:, None] == lane_head[None, :]
    qbd = jnp.where(head_row[None], qs[:, None, :], 0.0)
    k_s = us2[:, COL_Q + 512:COL_Q + 1024]
    v_s = us2[:, COL_Q + 1024:COL_Q + 1536]
    lf_s = jax.nn.log_sigmoid(us2[:, COL_F:COL_F + N_HEAD] + b_forget[0][None, :])
    n_pool = cache_k.shape[1]
    o_f_s = fox_paged(page_table, qbd, k_s.reshape(ns, 1, 512), v_s.reshape(ns, 1, 512),
                      lf_s.reshape(ns, N_HEAD, 1), cache_k[0].reshape(n_pool, PAGE, 512),
                      cache_v[0].reshape(n_pool, PAGE, 512), jnp.swapaxes(cache_logf[0], 1, 2))[:, 0]

    x1 = merge_out(xp2, o_r.reshape(nb * tp, 512), o_f.reshape(nb * tp, 512), u2, wr, wf, wo, tm)
    yp = ffn_final(x1, g_ffn, g_fin, wu, wdn, tm, 512).reshape(nb, tp, D_MODEL)
    x1s = merge_out(x_sample.reshape(ns, D_MODEL), o_r_s, o_f_s, us2, wr, wf, wo, ns)
    ysamp = ffn_final(x1s, g_ffn, g_fin, wu, wdn, ns, 512)

    y_prompt = yp[:, ROW0:]
    y_sample = ysamp.reshape(ns, 1, D_MODEL)
    new_shift_p = u3[:, tp - 1, :D_SHIFT][None]
    k_p = u3[:, FRONT_PAD:, COL_Q + 512:COL_Q + 1024].reshape(1, nb, tlen, N_HEAD, HEAD)
    v_p = u3[:, FRONT_PAD:, COL_Q + 1024:COL_Q + 1536].reshape(1, nb, tlen, N_HEAD, HEAD)
    lf_p = jnp.transpose(lf_t.reshape(nb, N_HEAD, tp), (0, 2, 1))[:, FRONT_PAD:][None]
    return (y_prompt, y_sample, new_shift_p, new_wkv_p, k_p, v_p, lf_p,
            us2[:, :D_SHIFT][None], s_new[None], k_s.reshape(1, ns, 1, N_HEAD, HEAD),
            v_s.reshape(1, ns, 1, N_HEAD, HEAD), lf_s.reshape(1, ns, 1, N_HEAD))
```

```python
import functools

import jax
import jax.numpy as jnp
from jax import lax
from jax.experimental import pallas as pl
from jax.experimental.pallas import tpu as pltpu

f32 = jnp.float32
bf16 = jnp.bfloat16

D_MODEL = 1024
N_META = 16
D_RWKV = 512
HEAD = 64
N_HEAD = 8
D_SHIFT = 1792
D_FF = 4096
RMS_EPS = 1e-6
GN_EPS = 64e-5
PAGE = 128

FRONT_PAD = 112
ROW0 = FRONT_PAD + N_META
CHUNK = 64
NEG = -1e30

COL_R = 0
COL_F = 1792
COL_G = 2048
COL_Q = 4096
D_INP = 5632

VMEM_LIMIT = 56 * 1024 * 1024

NN = (((1,), (0,)), ((), ()))
NT = (((1,), (1,)), ((), ()))
TN = (((0,), (0,)), ((), ()))


def _dot(a, b, dims=NN):
    return lax.dot_general(a, b, dims, preferred_element_type=f32)


def _split2(x):
    hi = x.astype(bf16)
    lo = (x - hi.astype(f32)).astype(bf16)
    return hi, lo


def _split3(x):
    hi = x.astype(bf16)
    r1 = x - hi.astype(f32)
    mid = r1.astype(bf16)
    lo = (r1 - mid.astype(f32)).astype(bf16)
    return hi, mid, lo


def _dot3(a, b, dims=NN):
    ah, al = _split2(a)
    bh, bl = _split2(b)
    return _dot(ah, bh, dims) + (_dot(ah, bl, dims) + _dot(al, bh, dims))


def _dot_onesr(x, ones_bf16, dims=NN):
    hi, mid, lo = _split3(x)
    return _dot(hi, ones_bf16, dims) + (_dot(mid, ones_bf16, dims) + _dot(lo, ones_bf16, dims))


def _dot_onesl(ones_bf16, x, dims=NN):
    hi, mid, lo = _split3(x)
    return _dot(ones_bf16, hi, dims) + (_dot(ones_bf16, mid, dims) + _dot(ones_bf16, lo, dims))


def _params(sem, vmem=None):
    return pltpu.CompilerParams(dimension_semantics=sem, vmem_limit_bytes=vmem)


def _norm_matmul_kernel(x_ref, g_ref, w_ref, o_ref, h_ref):
    @pl.when(pl.program_id(1) == 0)
    def _():
        x = x_ref[...]
        ms = jnp.mean(x * x, axis=-1, keepdims=True)
        h_ref[...] = (x * lax.rsqrt(ms + RMS_EPS) * g_ref[...]).astype(bf16)

    o_ref[...] = _dot(h_ref[...], w_ref[...])


def norm_matmul(x, g, w, tm, tn):
    m, d = x.shape
    n = w.shape[1]
    return pl.pallas_call(
        _norm_matmul_kernel,
        grid=(m // tm, n // tn),
        in_specs=[pl.BlockSpec((tm, d), lambda i, j: (i, 0)),
                  pl.BlockSpec((1, d), lambda i, j: (0, 0)),
                  pl.BlockSpec((d, tn), lambda i, j: (0, j))],
        out_specs=pl.BlockSpec((tm, tn), lambda i, j: (i, j)),
        out_shape=jax.ShapeDtypeStruct((m, n), f32),
        scratch_shapes=[pltpu.VMEM((tm, d), bf16)],
        compiler_params=_params(("parallel", "arbitrary"), VMEM_LIMIT),
        name="norm_matmul",
    )(x, g, w)


def _prep_kernel(u_ref, up_ref, mu_ref, w0_ref, wd_ref, a0_ref, wa_ref, wg_ref, kk_ref, ka_ref, rk_ref,
                 bd_ref, r_o, lw_o, k_o, v_o, kkn_o, b_o, g_o, bonus_o):
    u = u_ref[0]
    up = up_ref[0]
    us = u + mu_ref[...] * (up - u)
    r = us[:, 0:512]
    k = us[:, 512:1024]
    v = us[:, 1024:1536]
    zw = us[:, 1536:1600]
    za = us[:, 1600:1664]
    zg = us[:, 1664:1792]
    bd = bd_ref[...]

    z = -(w0_ref[...] + _dot3(jnp.tanh(zw), wd_ref[...]))
    w_raw = -(jnp.maximum(z, 0.0) + jnp.log1p(jnp.exp(-jnp.abs(z)))) - 0.5
    lw = -jnp.exp(w_raw)
    a = jax.nn.sigmoid(a0_ref[...] + _dot3(za, wa_ref[...]))
    g = _dot3(jax.nn.sigmoid(zg), wg_ref[...])
    kk = k * kk_ref[...]
    ss = _dot_onesr(kk * kk, bd)
    kkn = kk / jnp.maximum(jnp.sqrt(ss), 1e-12)
    k2 = k * (1.0 + (a - 1.0) * ka_ref[...])
    b = kkn * a
    bonus = _dot_onesr(r * k2 * rk_ref[...], bd) * v

    for h in range(N_HEAD):
        sl = slice(HEAD * h, HEAD * (h + 1))
        r_o[0, h] = r[:, sl]
        lw_o[0, h] = lw[:, sl]
        k_o[0, h] = k2[:, sl]
        v_o[0, h] = v[:, sl]
        kkn_o[0, h] = kkn[:, sl]
        b_o[0, h] = b[:, sl]
        g_o[0, h] = g[:, sl]
        bonus_o[0, h] = bonus[:, sl]


def rwkv_prep(u3, up3, pr, tm):
    nb, t = u3.shape[0], u3.shape[1]
    row = lambda i, j: (i, j, 0)
    cst = lambda i, j: (0, 0)
    hm = jax.ShapeDtypeStruct((nb, N_HEAD, t, HEAD), f32)
    hm_spec = pl.BlockSpec((1, N_HEAD, tm, HEAD), lambda i, j: (i, 0, j, 0))
    vec = lambda n: pl.BlockSpec((1, n), cst)
    return pl.pallas_call(
        _prep_kernel,
        grid=(nb, t // tm),
        in_specs=[pl.BlockSpec((1, tm, D_SHIFT), row), pl.BlockSpec((1, tm, D_SHIFT), row),
                  vec(D_SHIFT), vec(512), pl.BlockSpec((64, 512), cst), vec(512), pl.BlockSpec((64, 512), cst),
                  pl.BlockSpec((128, 512), cst), vec(512), vec(512), vec(512), pl.BlockSpec((512, 512), cst)],
        out_specs=[hm_spec] * 8,
        out_shape=[hm] * 8,
        compiler_params=_params(("parallel", "parallel"), VMEM_LIMIT),
        name="rwkv_prep",
    )(u3, up3, pr["mu"], pr["w0"], pr["wd"], pr["a0"], pr["wa"], pr["wg"], pr["k_k"], pr["k_a"], pr["r_k"],
      pr["bd"])


def _each(f, *lists):
    return [f(*xs) for xs in zip(*lists)]


def _wkv_chunk_kernel(r_ref, lw_ref, k_ref, v_ref, kk_ref, b_ref, pw_ref, qy_ref, *, nc):
    ii = lax.broadcasted_iota(jnp.int32, (CHUNK, CHUNK), 0)
    jj = lax.broadcasted_iota(jnp.int32, (CHUNK, CHUNK), 1)
    incl = jj <= ii
    strict = jj < ii
    ones_incl = incl.astype(bf16)
    same_blk = (ii >> 4) == (jj >> 4)
    eye = (ii == jj).astype(f32)
    rows = [slice(c * CHUNK, (c + 1) * CHUNK) for c in range(nc)]
    r = [r_ref[0, s, :] for s in rows]
    lw = [lw_ref[0, s, :] for s in rows]
    k = [k_ref[0, s, :] for s in rows]
    v = [v_ref[0, s, :] for s in rows]
    kk = [kk_ref[0, s, :] for s in rows]
    b = [b_ref[0, s, :] for s in rows]

    gcum = _each(lambda x: _dot_onesl(ones_incl, x), lw)
    gend = _each(lambda x: x[CHUNK - 1:CHUNK, :], gcum)
    kkt = _each(lambda x, gc, l: x * jnp.exp(gc - l), kk, gcum, lw)
    rt = _each(lambda x, gc: x * jnp.exp(gc), r, gcum)
    em = _each(lambda gc: jnp.exp(-gc), gcum)
    kh = _each(jnp.multiply, k, em)
    bh = _each(jnp.multiply, b, em)
    ec = _each(lambda ge, gc: jnp.exp(ge - gc), gend, gcum)
    kg = _each(jnp.multiply, k, ec)
    bg = _each(jnp.multiply, b, ec)

    lhs = _each(lambda x, y: jnp.concatenate([x, y], axis=0), kkt, rt)
    ab = _each(lambda x, y: _dot3(x, y, NT), lhs, bh)
    ak = _each(lambda x, y: _dot3(x, y, NT), lhs, kh)
    a_kb = _each(lambda x: jnp.where(strict, x[:CHUNK], 0.0), ab)
    a_rb = _each(lambda x: jnp.where(incl, x[CHUNK:], 0.0), ab)
    a_kk = _each(lambda x: jnp.where(strict, x[:CHUNK], 0.0), ak)
    a_rk = _each(lambda x: jnp.where(incl, x[CHUNK:], 0.0), ak)

    dg = _each(lambda x: jnp.where(same_blk, x, 0.0), a_kb)
    lo = _each(jnp.subtract, a_kb, dg)
    n1 = _each(jnp.negative, dg)
    n2 = _each(lambda x: _dot3(x, x), n1)
    n4 = _each(lambda x: _dot3(x, x), n2)
    n8 = _each(lambda x: _dot3(x, x), n4)
    t12 = _each(lambda x, y: _dot3(eye + x, eye + y), n1, n2)
    t48 = _each(lambda x, y: _dot3(eye + x, eye + y), n4, n8)
    td = _each(_dot3, t12, t48)
    x1 = _each(lambda x, y: -_dot3(x, y), td, lo)
    x2 = _each(lambda x: _dot3(x, x), x1)
    xx = _each(lambda x, y: _dot3(eye + x, eye + y), x1, x2)
    tinv = _each(_dot3, xx, td)

    akv = _each(_dot3, a_kk, v)
    w1u = _each(lambda t, x, y: _dot3(t, jnp.concatenate([x, y], axis=1)), tinv, kkt, akv)
    z = _each(lambda x, y, w: _dot3(jnp.concatenate([x.T, y], axis=0), w), bg, a_rb, w1u)
    kv = _each(lambda x, y, w: _dot3(jnp.concatenate([x.T, y], axis=0), w), kg, a_rk, v)
    for c in range(nc):
        base = jnp.concatenate([eye * jnp.exp(gend[c]), rt[c]], axis=0)
        pw_ref[0, c] = base - z[c][:, :CHUNK]
        qy_ref[0, c] = kv[c] - z[c][:, CHUNK:]


def wkv_chunks(r, lw, k, v, kk, b, nc):
    nbh, t, _ = r.shape
    nchunk = t // CHUNK
    in_spec = pl.BlockSpec((1, nc * CHUNK, HEAD), lambda i, j: (i, j, 0))
    out_spec = pl.BlockSpec((1, nc, 2 * CHUNK, HEAD), lambda i, j: (i, j, 0, 0))
    out = jax.ShapeDtypeStruct((nbh, nchunk, 2 * CHUNK, HEAD), f32)
    return pl.pallas_call(
        functools.partial(_wkv_chunk_kernel, nc=nc),
        grid=(nbh, nchunk // nc),
        in_specs=[in_spec] * 6,
        out_specs=[out_spec, out_spec],
        out_shape=[out, out],
        compiler_params=_params(("parallel", "parallel"), VMEM_LIMIT),
        name="wkv_chunks",
    )(r, lw, k, v, kk, b)


def _wkv_serial_kernel(pw_ref, qy_ref, y_ref, s_ref, st_ref, *, nbh, nc):
    @pl.when(pl.program_id(0) == 0)
    def _():
        st_ref[...] = jnp.zeros_like(st_ref)

    def body(c, carry):
        for i in range(nbh):
            z = _dot3(pw_ref[i, c], st_ref[i]) + qy_ref[i, c]
            st_ref[i] = z[:CHUNK]
            y_ref[i, pl.ds(pl.multiple_of(c * CHUNK, CHUNK), CHUNK), :] = z[CHUNK:]
        return carry

    lax.fori_loop(0, nc, body, 0)

    @pl.when(pl.program_id(0) == pl.num_programs(0) - 1)
    def _():
        s_ref[...] = st_ref[...]


def wkv_serial(pw, qy, nc):
    nbh, nchunk = pw.shape[0], pw.shape[1]
    blk = pl.BlockSpec((nbh, nc, 2 * CHUNK, HEAD), lambda j: (0, j, 0, 0))
    return pl.pallas_call(
        functools.partial(_wkv_serial_kernel, nbh=nbh, nc=nc),
        grid=(nchunk // nc,),
        in_specs=[blk, blk],
        out_specs=[pl.BlockSpec((nbh, nc * CHUNK, HEAD), lambda j: (0, j, 0)),
                   pl.BlockSpec((nbh, HEAD, HEAD), lambda j: (0, 0, 0))],
        out_shape=[jax.ShapeDtypeStruct((nbh, nchunk * CHUNK, HEAD), f32),
                   jax.ShapeDtypeStruct((nbh, HEAD, HEAD), f32)],
        scratch_shapes=[pltpu.VMEM((nbh, HEAD, HEAD), f32)],
        compiler_params=_params(("arbitrary",), VMEM_LIMIT),
        name="wkv_serial",
    )(pw, qy)


def _wkv_step_kernel(s_ref, r_ref, lw_ref, k_ref, kk_ref, b_ref, vc_ref, so_ref, y_ref):
    s = s_ref[0]
    skk = jnp.sum(s * kk_ref[0], axis=-1, keepdims=True)
    s1 = s * jnp.exp(lw_ref[0]) - skk * b_ref[0] + vc_ref[0] * k_ref[0]
    so_ref[0] = s1
    y_ref[0] = jnp.sum(s1 * r_ref[0], axis=-1, keepdims=True)


def wkv_step(s, r, lw, k, kk, b, vcol):
    n = s.shape[0]
    row = pl.BlockSpec((1, N_HEAD, 1, HEAD), lambda i: (i, 0, 0, 0))
    col = pl.BlockSpec((1, N_HEAD, HEAD, 1), lambda i: (i, 0, 0, 0))
    mat = pl.BlockSpec((1, N_HEAD, HEAD, HEAD), lambda i: (i, 0, 0, 0))
    return pl.pallas_call(
        _wkv_step_kernel,
        grid=(n,),
        in_specs=[mat, row, row, row, row, row, col],
        out_specs=[mat, col],
        out_shape=[jax.ShapeDtypeStruct(s.shape, f32), jax.ShapeDtypeStruct((n, N_HEAD, HEAD, 1), f32)],
        compiler_params=_params(("parallel",)),
        name="wkv_step",
    )(s, r, lw, k, kk, b, vcol)


def _post_kernel(y_ref, g_ref, bonus_ref, lnw_ref, lnb_ref, o_ref):
    parts = []
    for h in range(N_HEAD):
        y = y_ref[0, h]
        mu = jnp.mean(y, axis=-1, keepdims=True)
        yc = y - mu
        var = jnp.mean(yc * yc, axis=-1, keepdims=True)
        yn = yc * lax.rsqrt(var + GN_EPS)
        parts.append((yn * lnw_ref[h] + lnb_ref[h] + bonus_ref[0, h]) * g_ref[0, h])
    o_ref[0] = jnp.concatenate(parts, axis=-1)


def rwkv_post(y, g, bonus, lnw, lnb, tm):
    nb, _, t, _ = y.shape
    hm = pl.BlockSpec((1, N_HEAD, tm, HEAD), lambda i, j: (i, 0, j, 0))
    par = pl.BlockSpec((N_HEAD, 1, HEAD), lambda i, j: (0, 0, 0))
    return pl.pallas_call(
        _post_kernel,
        grid=(nb, t // tm),
        in_specs=[hm, hm, hm, par, par],
        out_specs=pl.BlockSpec((1, tm, D_RWKV), lambda i, j: (i, j, 0)),
        out_shape=jax.ShapeDtypeStruct((nb, t, D_RWKV), f32),
        compiler_params=_params(("parallel", "parallel"), VMEM_LIMIT),
        name="rwkv_post",
    )(y, g, bonus, lnw, lnb)


def _logf_kernel(fl_ref, b_ref, lf_ref, c_ref, *, nblk):
    ii = lax.broadcasted_iota(jnp.int32, (128, 128), 0)
    jj = lax.broadcasted_iota(jnp.int32, (128, 128), 1)
    upper = (ii <= jj).astype(bf16)
    carry = jnp.zeros((fl_ref.shape[0], 1), f32)
    for blk in range(nblk):
        sl = slice(128 * blk, 128 * (blk + 1))
        lf = jax.nn.log_sigmoid(fl_ref[:, sl] + b_ref[...])
        lf_ref[:, sl] = lf
        cs = _dot_onesr(lf, upper) + carry
        c_ref[:, sl] = cs
        carry = cs[:, 127:128]


def logf_cumsum(fl_t, bias):
    n, t = fl_t.shape
    out = jax.ShapeDtypeStruct((n, t), f32)
    return pl.pallas_call(
        functools.partial(_logf_kernel, nblk=t // 128),
        out_shape=[out, out],
        name="logf_cumsum",
    )(fl_t, bias)


def _fox_kernel(q_ref, k_ref, v_ref, ck_ref, o_ref, m_ref, l_ref, acc_ref, *, tq, tk):
    qi = pl.program_id(2)
    ki = pl.program_id(3)

    @pl.when(ki == 0)
    def _():
        m_ref[...] = jnp.full_like(m_ref, NEG)
        l_ref[...] = jnp.zeros_like(l_ref)
        acc_ref[...] = jnp.zeros_like(acc_ref)

    def step(masked):
        lane_lo = lax.broadcasted_iota(jnp.int32, (tq, 2 * HEAD), 1) < HEAD
        q = q_ref[0] * (HEAD ** -0.5)
        kb = k_ref[0].astype(bf16)
        vb = v_ref[0].astype(bf16)
        if masked:
            row = lax.broadcasted_iota(jnp.int32, (tq, tk), 0)
            col = lax.broadcasted_iota(jnp.int32, (tq, tk), 1)
            visible = col <= row
        alphas, pvs = [], []
        for e in range(2):
            qe = jnp.where(lane_lo == (e == 0), q, 0.0).astype(bf16)
            s = _dot(qe, kb, NT) - ck_ref[0, 0, e:e + 1, :]
            if masked:
                s = jnp.where(visible, s, NEG)
            m_old = m_ref[e]
            m_new = jnp.maximum(m_old, jnp.max(s, axis=-1, keepdims=True))
            alpha = jnp.exp(m_old - m_new)
            p = jnp.exp(s - m_new)
            l_ref[e] = alpha * l_ref[e] + jnp.sum(p, axis=-1, keepdims=True)
            m_ref[e] = m_new
            alphas.append(alpha)
            pvs.append(_dot(p.astype(bf16), vb))
        acc_ref[...] = (jnp.where(lane_lo, alphas[0], alphas[1]) * acc_ref[...]
                        + jnp.where(lane_lo, pvs[0], pvs[1]))

    @pl.when(ki < qi)
    def _():
        step(False)

    @pl.when(ki == qi)
    def _():
        step(True)
        lane_lo = lax.broadcasted_iota(jnp.int32, (tq, 2 * HEAD), 1) < HEAD
        o_ref[0] = acc_ref[...] / jnp.where(lane_lo, l_ref[0], l_ref[1])


def fox_prompt(u3, ck, tq):
    nb, t, _ = u3.shape
    nq = t // tq
    cq, ckk, cv = COL_Q // 128, (COL_Q + 512) // 128, (COL_Q + 1024) // 128
    return pl.pallas_call(
        functools.partial(_fox_kernel, tq=tq, tk=tq),
        grid=(nb, N_HEAD // 2, nq, nq),
        in_specs=[pl.BlockSpec((1, tq, 128), lambda b, h, i, j: (b, i, cq + h)),
                  pl.BlockSpec((1, tq, 128), lambda b, h, i, j: (b, jnp.minimum(i, j), ckk + h)),
                  pl.BlockSpec((1, tq, 128), lambda b, h, i, j: (b, jnp.minimum(i, j), cv + h)),
                  pl.BlockSpec((1, 1, 2, tq), lambda b, h, i, j: (b, h, 0, jnp.minimum(i, j)))],
        out_specs=pl.BlockSpec((1, tq, 128), lambda b, h, i, j: (b, i, h)),
        out_shape=jax.ShapeDtypeStruct((nb, t, 512), f32),
        scratch_shapes=[pltpu.VMEM((2, tq, 1), f32), pltpu.VMEM((2, tq, 1), f32),
                        pltpu.VMEM((tq, 2 * HEAD), f32)],
        compiler_params=_params(("parallel", "parallel", "parallel", "arbitrary"), VMEM_LIMIT),
        name="fox_prompt",
    )(u3, u3, u3, ck)


def _paged_kernel(pt_ref, q_ref, kn_ref, vn_ref, lfn_ref, *refs, g):
    k_refs, v_refs, lf_refs = refs[0:g], refs[g:2 * g], refs[2 * g:3 * g]
    o_ref = refs[3 * g]
    m_ref, l_ref, acc_ref, car_ref = refs[3 * g + 1:]
    step = pl.program_id(1)
    heads = [slice(HEAD * h, HEAD * (h + 1)) for h in range(N_HEAD)]
    q = q_ref[0]

    @pl.when(step == 0)
    def _():
        lane = lax.broadcasted_iota(jnp.int32, acc_ref.shape, 1)
        acc_ref[...] = jnp.where(lane == 0, vn_ref[0], 0.0)
        qk = q * kn_ref[0]
        for h in range(N_HEAD):
            m_ref[h] = jnp.sum(qk[heads[h]], axis=0, keepdims=True)
            l_ref[h] = jnp.ones((1, 1), f32)
        car_ref[...] = lfn_ref[0]

    ii = lax.broadcasted_iota(jnp.int32, (PAGE, PAGE), 0)
    jj = lax.broadcasted_iota(jnp.int32, (PAGE, PAGE), 1)
    later = (ii > jj).astype(bf16)
    carry = car_ref[...]
    biases = []
    for j in range(g):
        lfp = lf_refs[j][0]
        biases.append(_dot_onesr(lfp, later) + carry)
        carry = carry + jnp.sum(lfp, axis=-1, keepdims=True)
    car_ref[...] = carry

    for h in range(N_HEAD):
        qh = q[heads[h]]
        s = jnp.concatenate(
            [jnp.sum(k_refs[j][0, heads[h], :] * qh, axis=0, keepdims=True) + biases[j][h:h + 1, :]
             for j in range(g)], axis=0)
        m_old = m_ref[h]
        m_new = jnp.maximum(m_old, jnp.max(jnp.max(s, axis=-1, keepdims=True), axis=0, keepdims=True))
        alpha = jnp.exp(m_old - m_new)
        p = jnp.exp(s - m_new)
        l_ref[h] = alpha * l_ref[h] + jnp.sum(jnp.sum(p, axis=-1, keepdims=True), axis=0, keepdims=True)
        acc = alpha * acc_ref[heads[h], :]
        for j in range(g):
            acc = acc + v_refs[j][0, heads[h], :] * p[j:j + 1, :]
        acc_ref[heads[h], :] = acc
        m_ref[h] = m_new

    @pl.when(step == pl.num_programs(1) - 1)
    def _():
        for h in range(N_HEAD):
            o_ref[0, heads[h], :] = jnp.sum(acc_ref[heads[h], :], axis=-1, keepdims=True) / l_ref[h]


def fox_paged(page_table, q, knew, vnew, lfnew, kpool, vpool, lfpool, g):
    nb, npages = page_table.shape
    last = npages - 1
    cur = lambda b, i, pt: (b, 0, 0)
    pool = lambda j: (lambda b, i, pt: (pt[b, last - (i * g + j)], 0, 0))
    col = pl.BlockSpec((1, 512, 1), cur)
    return pl.pallas_call(
        functools.partial(_paged_kernel, g=g),
        grid_spec=pltpu.PrefetchScalarGridSpec(
            num_scalar_prefetch=1,
            grid=(nb, npages // g),
            in_specs=([col, col, col, pl.BlockSpec((1, N_HEAD, 1), cur)]
                      + [pl.BlockSpec((1, 512, PAGE), pool(j)) for j in range(g)]
                      + [pl.BlockSpec((1, 512, PAGE), pool(j)) for j in range(g)]
                      + [pl.BlockSpec((1, N_HEAD, PAGE), pool(j)) for j in range(g)]),
            out_specs=col,
            scratch_shapes=[pltpu.VMEM((N_HEAD, 1, 1), f32), pltpu.VMEM((N_HEAD, 1, 1), f32),
                            pltpu.VMEM((512, PAGE), f32), pltpu.VMEM((N_HEAD, 1), f32)]),
        out_shape=jax.ShapeDtypeStruct((nb, 512, 1), f32),
        compiler_params=_params(("parallel", "arbitrary"), VMEM_LIMIT),
        name="fox_paged",
    )(page_table, q, knew, vnew, lfnew, *([kpool] * g), *([vpool] * g), *([lfpool] * g))


def _merge_kernel(x_ref, or_ref, of_ref, ug_ref, wr_ref, wf_ref, wo_ref, o_ref):
    br = _dot(or_ref[...].astype(bf16), wr_ref[...])
    bf = _dot(of_ref[...].astype(bf16), wf_ref[...])
    ug = ug_ref[...]
    merged = jax.nn.sigmoid(ug[:, :D_MODEL]) * br + jax.nn.sigmoid(ug[:, D_MODEL:]) * bf
    o_ref[...] = x_ref[...] + _dot(merged.astype(bf16), wo_ref[...])


def merge_out(x, o_r, o_f, u, wr, wf, wo, tm):
    m = x.shape[0]
    row = lambda i: (i, 0)
    cst = lambda i: (0, 0)
    return pl.pallas_call(
        _merge_kernel,
        grid=(m // tm,),
        in_specs=[pl.BlockSpec((tm, D_MODEL), row), pl.BlockSpec((tm, 512), row), pl.BlockSpec((tm, 512), row),
                  pl.BlockSpec((tm, 2 * D_MODEL), lambda i: (i, COL_G // (2 * D_MODEL))),
                  pl.BlockSpec((512, D_MODEL), cst), pl.BlockSpec((512, D_MODEL), cst),
                  pl.BlockSpec((D_MODEL, D_MODEL), cst)],
        out_specs=pl.BlockSpec((tm, D_MODEL), row),
        out_shape=jax.ShapeDtypeStruct((m, D_MODEL), f32),
        compiler_params=_params(("parallel",), VMEM_LIMIT),
        name="merge_out",
    )(x, o_r, o_f, u, wr, wf, wo)


def _ffn_kernel(x_ref, g2_ref, gf_ref, wu_ref, wd_ref, o_ref, h_ref, acc_ref):
    j = pl.program_id(1)

    @pl.when(j == 0)
    def _():
        x = x_ref[...]
        ms = jnp.mean(x * x, axis=-1, keepdims=True)
        h_ref[...] = (x * lax.rsqrt(ms + RMS_EPS) * g2_ref[...]).astype(bf16)
        acc_ref[...] = jnp.zeros_like(acc_ref)

    hid = jnp.maximum(_dot(h_ref[...], wu_ref[...]), 0.0)
    acc_ref[...] += _dot((hid * hid).astype(bf16), wd_ref[...])

    @pl.when(j == pl.num_programs(1) - 1)
    def _():
        x2 = x_ref[...] + acc_ref[...]
        ms = jnp.mean(x2 * x2, axis=-1, keepdims=True)
        o_ref[...] = x2 * lax.rsqrt(ms + RMS_EPS) * gf_ref[...]


def ffn_final(x, g2, gf, wu, wd, tm, tf):
    m = x.shape[0]
    return pl.pallas_call(
        _ffn_kernel,
        grid=(m // tm, D_FF // tf),
        in_specs=[pl.BlockSpec((tm, D_MODEL), lambda i, j: (i, 0)),
                  pl.BlockSpec((1, D_MODEL), lambda i, j: (0, 0)), pl.BlockSpec((1, D_MODEL), lambda i, j: (0, 0)),
                  pl.BlockSpec((D_MODEL, tf), lambda i, j: (0, j)), pl.BlockSpec((tf, D_MODEL), lambda i, j: (j, 0))],
        out_specs=pl.BlockSpec((tm, D_MODEL), lambda i, j: (i, 0)),
        out_shape=jax.ShapeDtypeStruct((m, D_MODEL), f32),
        scratch_shapes=[pltpu.VMEM((tm, D_MODEL), bf16), pltpu.VMEM((tm, D_MODEL), f32)],
        compiler_params=_params(("parallel", "arbitrary"), VMEM_LIMIT),
        name="ffn_final",
    )(x, g2, gf, wu, wd)


def kernel(x_prompt, x_sample, state_shift, state_wkv, cache_k, cache_v, cache_logf, page_table, meta_tokens,
           norm_mix, w_in, mu_shift, w0, w_decay_up, a0, w_a_up, w_g_up, k_k, k_a, r_k, ln_x_w, ln_x_b, b_forget,
           w_br_rwkv, w_br_fox, w_out, norm_ffn, w_ffn_up, w_ffn_down, norm_final):
    assert norm_mix.shape[0] == 1, "single layer"
    nb, seq, _ = x_prompt.shape
    ns = x_sample.shape[0]
    tp = FRONT_PAD + N_META + seq
    tlen = N_META + seq
    tm = 640

    wi = w_in[0]
    wi = jnp.concatenate([wi[:, 0:1792], wi[:, 3328:3336], jnp.zeros((D_MODEL, 248), f32),
                          wi[:, 3336:5384], wi[:, 1792:3328]], axis=1).astype(bf16)
    row2 = lambda a: a.reshape(1, -1)
    lane_head = jnp.arange(512, dtype=jnp.int32) // HEAD
    pr = dict(mu=row2(mu_shift[0]), w0=row2(w0[0]), wd=w_decay_up[0], a0=row2(a0[0]), wa=w_a_up[0], wg=w_g_up[0],
              k_k=row2(k_k[0]), k_a=row2(k_a[0]), r_k=row2(r_k[0]),
              bd=(lane_head[:, None] == lane_head[None, :]).astype(bf16))
    lnw = ln_x_w[0].reshape(N_HEAD, 1, HEAD)
    lnb = ln_x_b[0].reshape(N_HEAD, 1, HEAD)
    g_mix, g_ffn, g_fin = row2(norm_mix[0]), row2(norm_ffn[0]), row2(norm_final)
    wr, wf, wo = w_br_rwkv[0].astype(bf16), w_br_fox[0].astype(bf16), w_out[0].astype(bf16)
    wu, wdn = w_ffn_up[0].astype(bf16), w_ffn_down[0].astype(bf16)

    meta = jnp.broadcast_to(meta_tokens[None], (nb, N_META, D_MODEL))
    xp = jnp.concatenate([jnp.zeros((nb, FRONT_PAD, D_MODEL), f32), meta, x_prompt], axis=1)
    xp2 = xp.reshape(nb * tp, D_MODEL)
    u2 = norm_matmul(xp2, g_mix, wi, tm, 512)
    u3 = u2.reshape(nb, tp, D_INP)
    us2 = norm_matmul(x_sample.reshape(ns, D_MODEL), g_mix, wi, ns, 512)

    ur = u3[:, :, :D_SHIFT]
    up3 = jnp.concatenate([jnp.zeros((nb, 1, D_SHIFT), f32), ur[:, :-1]], axis=1)
    r, lw, k2, v, kkn, b, g, bonus = rwkv_prep(u3, up3, pr, 320)
    flat = lambda a: a.reshape(nb * N_HEAD, tp, HEAD)
    pw, qy = wkv_chunks(flat(r), flat(lw), flat(k2), flat(v), flat(kkn), flat(b), 13)
    y, s_kv = wkv_serial(pw, qy, 5)
    o_r = rwkv_post(y.reshape(nb, N_HEAD, tp, HEAD), g, bonus, lnw, lnb, tm)
    new_wkv_p = jnp.swapaxes(s_kv, 1, 2).reshape(1, nb, N_HEAD, HEAD, HEAD)

    us3 = us2.reshape(1, ns, D_INP)
    sp = rwkv_prep(us3, state_shift[0].reshape(1, ns, D_SHIFT), pr, ns)
    rs, lws, ks, vs, kks, bs, gs, bonus_s = sp
    rowify = lambda a: jnp.transpose(a[0], (1, 0, 2)).reshape(ns, N_HEAD, 1, HEAD)
    s_new, ys = wkv_step(state_wkv[0], rowify(rs), rowify(lws), rowify(ks), rowify(kks), rowify(bs),
                         rowify(vs).reshape(ns, N_HEAD, HEAD, 1))
    ys_hm = jnp.transpose(ys.reshape(ns, N_HEAD, HEAD), (1, 0, 2))[None]
    o_r_s = rwkv_post(ys_hm, gs, bonus_s, lnw, lnb, ns)[0]

    fl_t = jnp.transpose(u3[:, :, COL_F:COL_F + N_HEAD], (0, 2, 1)).reshape(nb * N_HEAD, tp)
    bias_rows = jnp.tile(b_forget[0], nb).reshape(nb * N_HEAD, 1)
    lf_t, c_t = logf_cumsum(fl_t, bias_rows)
    pad_key = jnp.arange(tp, dtype=jnp.int32)[None, :] < FRONT_PAD
    ck = jnp.where(pad_key, -NEG, c_t).reshape(nb, N_HEAD // 2, 2, tp)
    o_f = fox_prompt(u3, ck, tm)

    qs = us2[:, COL_Q:COL_Q + 512] * (HEAD ** -0.5)
    k_s = us2[:, COL_Q + 512:COL_Q + 1024]
    v_s = us2[:, COL_Q + 1024:COL_Q + 1536]
    lf_s = jax.nn.log_sigmoid(us2[:, COL_F:COL_F + N_HEAD] + b_forget[0][None, :])
    n_pool = cache_k.shape[1]
    kpool = jnp.transpose(cache_k[0], (0, 2, 3, 1)).reshape(n_pool, 512, PAGE)
    vpool = jnp.transpose(cache_v[0], (0, 2, 3, 1)).reshape(n_pool, 512, PAGE)
    o_f_s = fox_paged(page_table, qs.reshape(ns, 512, 1), k_s.reshape(ns, 512, 1), v_s.reshape(ns, 512, 1),
                      lf_s.reshape(ns, N_HEAD, 1), kpool, vpool, jnp.swapaxes(cache_logf[0], 1, 2),
                      8).reshape(ns, 512)

    x1 = merge_out(xp2, o_r.reshape(nb * tp, 512), o_f.reshape(nb * tp, 512), u2, wr, wf, wo, tm)
    yp = ffn_final(x1, g_ffn, g_fin, wu, wdn, tm, 512).reshape(nb, tp, D_MODEL)
    x1s = merge_out(x_sample.reshape(ns, D_MODEL), o_r_s, o_f_s, us2, wr, wf, wo, ns)
    ysamp = ffn_final(x1s, g_ffn, g_fin, wu, wdn, ns, 512)

    y_prompt = yp[:, ROW0:]
    y_sample = ysamp.reshape(ns, 1, D_MODEL)
    new_shift_p = u3[:, tp - 1, :D_SHIFT][None]
    k_p = u3[:, FRONT_PAD:, COL_Q + 512:COL_Q + 1024].reshape(1, nb, tlen, N_HEAD, HEAD)
    v_p = u3[:, FRONT_PAD:, COL_Q + 1024:COL_Q + 1536].reshape(1, nb, tlen, N_HEAD, HEAD)
    lf_p = jnp.transpose(lf_t.reshape(nb, N_HEAD, tp), (0, 2, 1))[:, FRONT_PAD:][None]
    return (y_prompt, y_sample, new_shift_p, new_wkv_p, k_p, v_p, lf_p,
            us2[:, :D_SHIFT][None], s_new[None], k_s.reshape(1, ns, 1, N_HEAD, HEAD),
            v_s.reshape(1, ns, 1, N_HEAD, HEAD), lf_s.reshape(1, ns, 1, N_HEAD))
```

```python
import functools

import jax
import jax.numpy as jnp
from jax import lax
from jax.experimental import pallas as pl
from jax.experimental.pallas import tpu as pltpu

f32 = jnp.float32
bf16 = jnp.bfloat16

D_MODEL = 1024
N_META = 16
D_RWKV = 512
HEAD = 64
N_HEAD = 8
D_SHIFT = 1792
D_FF = 4096
RMS_EPS = 1e-6
GN_EPS = 64e-5
PAGE = 128

FRONT_PAD = 112
ROW0 = FRONT_PAD + N_META
CHUNK = 64
NEG = -1e30
LOG2E = 1.4426950408889634

COL_R = 0
COL_F = 1792
COL_G = 2048
COL_Q = 4096
D_INP = 5632

VMEM_LIMIT = 56 * 1024 * 1024

NN = (((1,), (0,)), ((), ()))
NT = (((1,), (1,)), ((), ()))
TN = (((0,), (0,)), ((), ()))


def _dot(a, b, dims=NN):
    return lax.dot_general(a, b, dims, preferred_element_type=f32)


def _split2(x):
    hi = x.astype(bf16)
    lo = (x - hi.astype(f32)).astype(bf16)
    return hi, lo


def _split3(x):
    hi = x.astype(bf16)
    r1 = x - hi.astype(f32)
    mid = r1.astype(bf16)
    lo = (r1 - mid.astype(f32)).astype(bf16)
    return hi, mid, lo


def _dot3(a, b, dims=NN):
    ah, al = _split2(a)
    bh, bl = _split2(b)
    return _dot(ah, bh, dims) + (_dot(ah, bl, dims) + _dot(al, bh, dims))


def _dot1(a, b, dims=NN):
    return _dot(a.astype(bf16), b.astype(bf16), dims)


def _dot_onesr(x, ones_bf16, dims=NN):
    hi, mid, lo = _split3(x)
    return _dot(hi, ones_bf16, dims) + (_dot(mid, ones_bf16, dims) + _dot(lo, ones_bf16, dims))


def _dot_onesl(ones_bf16, x, dims=NN):
    hi, mid, lo = _split3(x)
    return _dot(ones_bf16, hi, dims) + (_dot(ones_bf16, mid, dims) + _dot(ones_bf16, lo, dims))


def _params(sem, vmem=None):
    return pltpu.CompilerParams(dimension_semantics=sem, vmem_limit_bytes=vmem)


def _norm_matmul_kernel(x_ref, g_ref, w_ref, o_ref, h_ref):
    @pl.when(pl.program_id(1) == 0)
    def _():
        x = x_ref[...]
        ms = jnp.mean(x * x, axis=-1, keepdims=True)
        h_ref[...] = (x * lax.rsqrt(ms + RMS_EPS) * g_ref[...]).astype(bf16)

    o_ref[...] = _dot(h_ref[...], w_ref[...])


def norm_matmul(x, g, w, tm, tn):
    m, d = x.shape
    n = w.shape[1]
    return pl.pallas_call(
        _norm_matmul_kernel,
        grid=(m // tm, n // tn),
        in_specs=[pl.BlockSpec((tm, d), lambda i, j: (i, 0)),
                  pl.BlockSpec((1, d), lambda i, j: (0, 0)),
                  pl.BlockSpec((d, tn), lambda i, j: (0, j))],
        out_specs=pl.BlockSpec((tm, tn), lambda i, j: (i, j)),
        out_shape=jax.ShapeDtypeStruct((m, n), f32),
        scratch_shapes=[pltpu.VMEM((tm, d), bf16)],
        compiler_params=_params(("parallel", "arbitrary"), VMEM_LIMIT),
        name="norm_matmul",
    )(x, g, w)


def _prep_kernel(u_ref, up_ref, mu_ref, w0_ref, wd_ref, a0_ref, wa_ref, wg_ref, kk_ref, ka_ref, rk_ref,
                 bd_ref, r_o, lw_o, k_o, v_o, kkn_o, b_o, g_o, bonus_o, *, prev_is_tail):
    u = u_ref[0]
    if prev_is_tail:
        first = jnp.where(pl.program_id(1) == 0, 0.0, up_ref[0, 7:8, :])
        rowi = lax.broadcasted_iota(jnp.int32, u.shape, 0)
        up = jnp.where(rowi == 0, first, pltpu.roll(u, 1, 0))
    else:
        up = up_ref[0]
    us = u + mu_ref[...] * (up - u)
    r = us[:, 0:512]
    k = us[:, 512:1024]
    v = us[:, 1024:1536]
    zw = us[:, 1536:1600]
    za = us[:, 1600:1664]
    zg = us[:, 1664:1792]
    bd = bd_ref[...]

    z = -(w0_ref[...] + _dot3(jnp.tanh(zw), wd_ref[...]))
    w_raw = -(jnp.maximum(z, 0.0) + jnp.log1p(jnp.exp(-jnp.abs(z)))) - 0.5
    lw = -jnp.exp(w_raw)
    a = jax.nn.sigmoid(a0_ref[...] + _dot3(za, wa_ref[...]))
    g = _dot3(jax.nn.sigmoid(zg), wg_ref[...])
    kk = k * kk_ref[...]
    ss = _dot_onesr(kk * kk, bd)
    kkn = kk / jnp.maximum(jnp.sqrt(ss), 1e-12)
    k2 = k * (1.0 + (a - 1.0) * ka_ref[...])
    b = kkn * a
    bonus = _dot_onesr(r * k2 * rk_ref[...], bd) * v

    for h in range(N_HEAD):
        sl = slice(HEAD * h, HEAD * (h + 1))
        r_o[0, h] = r[:, sl]
        lw_o[0, h] = lw[:, sl]
        k_o[0, h] = k2[:, sl]
        v_o[0, h] = v[:, sl]
        kkn_o[0, h] = kkn[:, sl]
        b_o[0, h] = b[:, sl]
        g_o[0, h] = g[:, sl]
        bonus_o[0, h] = bonus[:, sl]


def rwkv_prep(u3, up3, pr, tm):
    nb, t = u3.shape[0], u3.shape[1]
    row = lambda i, j: (i, j, 0)
    if up3 is None:
        up_arr = u3
        up_spec = pl.BlockSpec((1, 8, D_SHIFT), lambda i, j: (i, jnp.maximum(j * (tm // 8) - 1, 0), 0))
    else:
        up_arr = up3
        up_spec = pl.BlockSpec((1, tm, D_SHIFT), row)
    cst = lambda i, j: (0, 0)
    hm = jax.ShapeDtypeStruct((nb, N_HEAD, t, HEAD), f32)
    hm_spec = pl.BlockSpec((1, N_HEAD, tm, HEAD), lambda i, j: (i, 0, j, 0))
    vec = lambda n: pl.BlockSpec((1, n), cst)
    return pl.pallas_call(
        functools.partial(_prep_kernel, prev_is_tail=up3 is None),
        grid=(nb, t // tm),
        in_specs=[pl.BlockSpec((1, tm, D_SHIFT), row), up_spec,
                  vec(D_SHIFT), vec(512), pl.BlockSpec((64, 512), cst), vec(512), pl.BlockSpec((64, 512), cst),
                  pl.BlockSpec((128, 512), cst), vec(512), vec(512), vec(512), pl.BlockSpec((512, 512), cst)],
        out_specs=[hm_spec] * 8,
        out_shape=[hm] * 8,
        compiler_params=_params(("parallel", "parallel"), VMEM_LIMIT),
        name="rwkv_prep",
    )(u3, up_arr, pr["mu"], pr["w0"], pr["wd"], pr["a0"], pr["wa"], pr["wg"], pr["k_k"], pr["k_a"], pr["r_k"],
      pr["bd"])


def _each(f, *lists):
    return [f(*xs) for xs in zip(*lists)]


def _wkv_chunk_kernel(r_ref, lw_ref, k_ref, v_ref, kk_ref, b_ref, pw_ref, qy_ref, *, nc):
    ii = lax.broadcasted_iota(jnp.int32, (CHUNK, CHUNK), 0)
    jj = lax.broadcasted_iota(jnp.int32, (CHUNK, CHUNK), 1)
    incl = jj <= ii
    strict = jj < ii
    ones_incl = incl.astype(bf16)
    same_blk = (ii >> 4) == (jj >> 4)
    eye = (ii == jj).astype(f32)
    rows = [slice(c * CHUNK, (c + 1) * CHUNK) for c in range(nc)]
    r = [r_ref[0, s, :] for s in rows]
    lw = [lw_ref[0, s, :] for s in rows]
    k = [k_ref[0, s, :] for s in rows]
    v = [v_ref[0, s, :] for s in rows]
    kk = [kk_ref[0, s, :] for s in rows]
    b = [b_ref[0, s, :] for s in rows]

    gcum = _each(lambda x: _dot_onesl(ones_incl, x), lw)
    gend = _each(lambda x: x[CHUNK - 1:CHUNK, :], gcum)
    kkt = _each(lambda x, gc, l: x * jnp.exp(gc - l), kk, gcum, lw)
    rt = _each(lambda x, gc: x * jnp.exp(gc), r, gcum)
    em = _each(lambda gc: jnp.exp(-gc), gcum)
    kh = _each(jnp.multiply, k, em)
    bh = _each(jnp.multiply, b, em)
    ec = _each(lambda ge, gc: jnp.exp(ge - gc), gend, gcum)
    kg = _each(jnp.multiply, k, ec)
    bg = _each(jnp.multiply, b, ec)

    lhs = _each(lambda x, y: jnp.concatenate([x, y], axis=0), kkt, rt)
    ab = _each(lambda x, y: _dot1(x, y, NT), lhs, bh)
    ak = _each(lambda x, y: _dot1(x, y, NT), lhs, kh)
    a_kb = _each(lambda x: jnp.where(strict, x[:CHUNK], 0.0), ab)
    a_rb = _each(lambda x: jnp.where(incl, x[CHUNK:], 0.0), ab)
    a_kk = _each(lambda x: jnp.where(strict, x[:CHUNK], 0.0), ak)
    a_rk = _each(lambda x: jnp.where(incl, x[CHUNK:], 0.0), ak)

    dg = _each(lambda x: jnp.where(same_blk, x, 0.0), a_kb)
    lo = _each(jnp.subtract, a_kb, dg)
    n1 = _each(jnp.negative, dg)
    n2 = _each(lambda x: _dot1(x, x), n1)
    n4 = _each(lambda x: _dot1(x, x), n2)
    n8 = _each(lambda x: _dot1(x, x), n4)
    t12 = _each(lambda x, y: _dot1(eye + x, eye + y), n1, n2)
    t48 = _each(lambda x, y: _dot1(eye + x, eye + y), n4, n8)
    td = _each(_dot1, t12, t48)
    x1 = _each(lambda x, y: -_dot1(x, y), td, lo)
    x2 = _each(lambda x: _dot1(x, x), x1)
    xx = _each(lambda x, y: _dot1(eye + x, eye + y), x1, x2)
    tinv = _each(_dot1, xx, td)

    akv = _each(_dot1, a_kk, v)
    w1u = _each(lambda t, x, y: _dot1(t, jnp.concatenate([x, y], axis=1)), tinv, kkt, akv)
    z = _each(lambda x, y, w: _dot1(jnp.concatenate([x.T, y], axis=0), w), bg, a_rb, w1u)
    kv = _each(lambda x, y, w: _dot1(jnp.concatenate([x.T, y], axis=0), w), kg, a_rk, v)
    for c in range(nc):
        base = jnp.concatenate([eye * jnp.exp(gend[c]), rt[c]], axis=0)
        pw_ref[0, c] = base - z[c][:, :CHUNK]
        qy_ref[0, c] = kv[c] - z[c][:, CHUNK:]


def wkv_chunks(r, lw, k, v, kk, b, nc):
    nbh, t, _ = r.shape
    nchunk = t // CHUNK
    in_spec = pl.BlockSpec((1, nc * CHUNK, HEAD), lambda i, j: (i, j, 0))
    out_spec = pl.BlockSpec((1, nc, 2 * CHUNK, HEAD), lambda i, j: (i, j, 0, 0))
    out = jax.ShapeDtypeStruct((nbh, nchunk, 2 * CHUNK, HEAD), f32)
    return pl.pallas_call(
        functools.partial(_wkv_chunk_kernel, nc=nc),
        grid=(nbh, nchunk // nc),
        in_specs=[in_spec] * 6,
        out_specs=[out_spec, out_spec],
        out_shape=[out, out],
        compiler_params=_params(("parallel", "parallel"), VMEM_LIMIT),
        name="wkv_chunks",
    )(r, lw, k, v, kk, b)


def _wkv_serial_kernel(pw_ref, qy_ref, y_ref, s_ref, st_ref, *, nbh, nc):
    @pl.when(pl.program_id(0) == 0)
    def _():
        st_ref[...] = jnp.zeros_like(st_ref)

    def body(c, carry):
        for i in range(nbh):
            z = _dot3(pw_ref[i, c], st_ref[i]) + qy_ref[i, c]
            st_ref[i] = z[:CHUNK]
            y_ref[i, pl.ds(pl.multiple_of(c * CHUNK, CHUNK), CHUNK), :] = z[CHUNK:]
        return carry

    lax.fori_loop(0, nc, body, 0)

    @pl.when(pl.program_id(0) == pl.num_programs(0) - 1)
    def _():
        s_ref[...] = st_ref[...]


def wkv_serial(pw, qy, nc):
    nbh, nchunk = pw.shape[0], pw.shape[1]
    blk = pl.BlockSpec((nbh, nc, 2 * CHUNK, HEAD), lambda j: (0, j, 0, 0))
    return pl.pallas_call(
        functools.partial(_wkv_serial_kernel, nbh=nbh, nc=nc),
        grid=(nchunk // nc,),
        in_specs=[blk, blk],
        out_specs=[pl.BlockSpec((nbh, nc * CHUNK, HEAD), lambda j: (0, j, 0)),
                   pl.BlockSpec((nbh, HEAD, HEAD), lambda j: (0, 0, 0))],
        out_shape=[jax.ShapeDtypeStruct((nbh, nchunk * CHUNK, HEAD), f32),
                   jax.ShapeDtypeStruct((nbh, HEAD, HEAD), f32)],
        scratch_shapes=[pltpu.VMEM((nbh, HEAD, HEAD), f32)],
        compiler_params=_params(("arbitrary",), VMEM_LIMIT),
        name="wkv_serial",
    )(pw, qy)


def _wkv_step_kernel(s_ref, r_ref, lw_ref, k_ref, kk_ref, b_ref, vc_ref, so_ref, y_ref):
    s = s_ref[0]
    skk = jnp.sum(s * kk_ref[0], axis=-1, keepdims=True)
    s1 = s * jnp.exp(lw_ref[0]) - skk * b_ref[0] + vc_ref[0] * k_ref[0]
    so_ref[0] = s1
    y_ref[0] = jnp.sum(s1 * r_ref[0], axis=-1, keepdims=True)


def wkv_step(s, r, lw, k, kk, b, vcol):
    n = s.shape[0]
    row = pl.BlockSpec((1, N_HEAD, 1, HEAD), lambda i: (i, 0, 0, 0))
    col = pl.BlockSpec((1, N_HEAD, HEAD, 1), lambda i: (i, 0, 0, 0))
    mat = pl.BlockSpec((1, N_HEAD, HEAD, HEAD), lambda i: (i, 0, 0, 0))
    return pl.pallas_call(
        _wkv_step_kernel,
        grid=(n,),
        in_specs=[mat, row, row, row, row, row, col],
        out_specs=[mat, col],
        out_shape=[jax.ShapeDtypeStruct(s.shape, f32), jax.ShapeDtypeStruct((n, N_HEAD, HEAD, 1), f32)],
        compiler_params=_params(("parallel",)),
        name="wkv_step",
    )(s, r, lw, k, kk, b, vcol)


def _post_kernel(y_ref, g_ref, bonus_ref, lnw_ref, lnb_ref, o_ref):
    parts = []
    for h in range(N_HEAD):
        y = y_ref[0, h]
        mu = jnp.mean(y, axis=-1, keepdims=True)
        yc = y - mu
        var = jnp.mean(yc * yc, axis=-1, keepdims=True)
        yn = yc * lax.rsqrt(var + GN_EPS)
        parts.append((yn * lnw_ref[h] + lnb_ref[h] + bonus_ref[0, h]) * g_ref[0, h])
    o_ref[0] = jnp.concatenate(parts, axis=-1)


def rwkv_post(y, g, bonus, lnw, lnb, tm):
    nb, _, t, _ = y.shape
    hm = pl.BlockSpec((1, N_HEAD, tm, HEAD), lambda i, j: (i, 0, j, 0))
    par = pl.BlockSpec((N_HEAD, 1, HEAD), lambda i, j: (0, 0, 0))
    return pl.pallas_call(
        _post_kernel,
        grid=(nb, t // tm),
        in_specs=[hm, hm, hm, par, par],
        out_specs=pl.BlockSpec((1, tm, D_RWKV), lambda i, j: (i, j, 0)),
        out_shape=jax.ShapeDtypeStruct((nb, t, D_RWKV), f32),
        compiler_params=_params(("parallel", "parallel"), VMEM_LIMIT),
        name="rwkv_post",
    )(y, g, bonus, lnw, lnb)


def _logf_kernel(fl_ref, b_ref, lf_ref, c_ref, *, nblk):
    ii = lax.broadcasted_iota(jnp.int32, (128, 128), 0)
    jj = lax.broadcasted_iota(jnp.int32, (128, 128), 1)
    upper = (ii <= jj).astype(bf16)
    carry = jnp.zeros((fl_ref.shape[0], 1), f32)
    for blk in range(nblk):
        sl = slice(128 * blk, 128 * (blk + 1))
        lf = jax.nn.log_sigmoid(fl_ref[:, sl] + b_ref[...])
        lf_ref[:, sl] = lf
        cs = _dot_onesr(lf, upper) + carry
        c_ref[:, sl] = cs
        carry = cs[:, 127:128]


def logf_cumsum(fl_t, bias):
    n, t = fl_t.shape
    out = jax.ShapeDtypeStruct((n, t), f32)
    return pl.pallas_call(
        functools.partial(_logf_kernel, nblk=t // 128),
        out_shape=[out, out],
        name="logf_cumsum",
    )(fl_t, bias)


SUM_ROWS = 16


def _fox_kernel(q_ref, k_ref, v_ref, ck_ref, o_ref, m_ref, acc_ref, *, tq, tk, nsub):
    qi = pl.program_id(2)
    ki = pl.program_id(3)

    @pl.when(ki == 0)
    def _():
        m_ref[...] = jnp.full_like(m_ref, NEG)
        acc_ref[...] = jnp.zeros_like(acc_ref)

    def step(masked):
        lane_lo = lax.broadcasted_iota(jnp.int32, (tq, 2 * HEAD), 1) < HEAD
        q = q_ref[0] * (HEAD ** -0.5 * LOG2E)
        qes = [jnp.where(lane_lo == (e == 0), q, 0.0).astype(bf16) for e in range(2)]
        ts = tk // nsub

        def scores(sub):
            kb = k_ref[0, sub * ts:(sub + 1) * ts, :].astype(bf16)
            return [_dot(kb, qes[e], NT) for e in range(2)]

        if masked:
            row = lax.broadcasted_iota(jnp.int32, (ts, tq), 0)
            col = lax.broadcasted_iota(jnp.int32, (ts, tq), 1)
        ones = jnp.ones((SUM_ROWS, ts), bf16)
        s_cur = scores(0)
        for sub in range(nsub):
            s_next = scores(sub + 1) if sub + 1 < nsub else None
            rs = slice(sub * ts, (sub + 1) * ts)
            vt = v_ref[0, rs, :].T.astype(bf16)
            ps, alphas = [], []
            for e in range(2):
                s = s_cur[e] - ck_ref[0, 0, rs, e:e + 1]
                if masked:
                    s = jnp.where(row + sub * ts <= col, s, NEG)
                m_old = m_ref[e]
                m_new = jnp.maximum(m_old, jnp.max(s, axis=0, keepdims=True))
                alphas.append(jnp.exp2(m_old - m_new))
                ps.append(jnp.exp2(s - m_new).astype(bf16))
                m_ref[e] = m_new
            for e in range(2):
                vte = jnp.concatenate([vt[HEAD * e:HEAD * (e + 1), :], ones], axis=0)
                acc_ref[e] = alphas[e] * acc_ref[e] + _dot(vte, ps[e])
            s_cur = s_next

    @pl.when(ki < qi)
    def _():
        step(False)

    @pl.when(ki == qi)
    def _():
        step(True)
        o = [acc_ref[e, 0:HEAD, :] / acc_ref[e, HEAD:HEAD + 1, :] for e in range(2)]
        o_ref[0] = jnp.concatenate(o, axis=0).T


def fox_prompt(u3, ck, tq, nsub):
    nb, t, _ = u3.shape
    nq = t // tq
    cq, ckk, cv = COL_Q // 128, (COL_Q + 512) // 128, (COL_Q + 1024) // 128
    return pl.pallas_call(
        functools.partial(_fox_kernel, tq=tq, tk=tq, nsub=nsub),
        grid=(nb, N_HEAD // 2, nq, nq),
        in_specs=[pl.BlockSpec((1, tq, 128), lambda b, h, i, j: (b, i, cq + h)),
                  pl.BlockSpec((1, tq, 128), lambda b, h, i, j: (b, jnp.minimum(i, j), ckk + h)),
                  pl.BlockSpec((1, tq, 128), lambda b, h, i, j: (b, jnp.minimum(i, j), cv + h)),
                  pl.BlockSpec((1, 1, tq, 2), lambda b, h, i, j: (b, h, jnp.minimum(i, j), 0))],
        out_specs=pl.BlockSpec((1, tq, 128), lambda b, h, i, j: (b, i, h)),
        out_shape=jax.ShapeDtypeStruct((nb, t, 512), f32),
        scratch_shapes=[pltpu.VMEM((2, 1, tq), f32), pltpu.VMEM((2, HEAD + SUM_ROWS, tq), f32)],
        compiler_params=_params(("parallel", "parallel", "parallel", "arbitrary"), VMEM_LIMIT),
        name="fox_prompt",
    )(u3, u3, u3, ck)


def _paged_kernel(pt_ref, q_ref, kn_ref, vn_ref, lfn_ref, *refs, g):
    k_refs, v_refs, lf_refs = refs[0:g], refs[g:2 * g], refs[2 * g:3 * g]
    o_ref = refs[3 * g]
    m_ref, l_ref, acc_ref, car_ref = refs[3 * g + 1:]
    step = pl.program_id(1)
    heads = [slice(HEAD * h, HEAD * (h + 1)) for h in range(N_HEAD)]
    q = q_ref[0]

    @pl.when(step == 0)
    def _():
        lane = lax.broadcasted_iota(jnp.int32, acc_ref.shape, 1)
        acc_ref[...] = jnp.where(lane == 0, vn_ref[0], 0.0)
        qk = q * kn_ref[0]
        for h in range(N_HEAD):
            m_ref[h] = jnp.sum(qk[heads[h]], axis=0, keepdims=True)
            l_ref[h] = jnp.ones((1, 1), f32)
        car_ref[...] = lfn_ref[0]

    ii = lax.broadcasted_iota(jnp.int32, (PAGE, PAGE), 0)
    jj = lax.broadcasted_iota(jnp.int32, (PAGE, PAGE), 1)
    later = (ii > jj).astype(bf16)
    carry = car_ref[...]
    biases = []
    for j in range(g):
        lfp = lf_refs[j][0]
        biases.append(_dot_onesr(lfp, later) + carry)
        carry = carry + jnp.sum(lfp, axis=-1, keepdims=True)
    car_ref[...] = carry

    for h in range(N_HEAD):
        qh = q[heads[h]]
        s = jnp.concatenate(
            [jnp.sum(k_refs[j][0, heads[h], :] * qh, axis=0, keepdims=True) + biases[j][h:h + 1, :]
             for j in range(g)], axis=0)
        m_old = m_ref[h]
        m_new = jnp.maximum(m_old, jnp.max(jnp.max(s, axis=-1, keepdims=True), axis=0, keepdims=True))
        alpha = jnp.exp(m_old - m_new)
        p = jnp.exp(s - m_new)
        l_ref[h] = alpha * l_ref[h] + jnp.sum(jnp.sum(p, axis=-1, keepdims=True), axis=0, keepdims=True)
        acc = alpha * acc_ref[heads[h], :]
        for j in range(g):
            acc = acc + v_refs[j][0, heads[h], :] * p[j:j + 1, :]
        acc_ref[heads[h], :] = acc
        m_ref[h] = m_new

    @pl.when(step == pl.num_programs(1) - 1)
    def _():
        for h in range(N_HEAD):
            o_ref[0, heads[h], :] = jnp.sum(acc_ref[heads[h], :], axis=-1, keepdims=True) / l_ref[h]


def fox_paged(page_table, q, knew, vnew, lfnew, kpool, vpool, lfpool, g):
    nb, npages = page_table.shape
    last = npages - 1
    cur = lambda b, i, pt: (b, 0, 0)
    pool = lambda j: (lambda b, i, pt: (pt[b, last - (i * g + j)], 0, 0))
    col = pl.BlockSpec((1, 512, 1), cur)
    return pl.pallas_call(
        functools.partial(_paged_kernel, g=g),
        grid_spec=pltpu.PrefetchScalarGridSpec(
            num_scalar_prefetch=1,
            grid=(nb, npages // g),
            in_specs=([col, col, col, pl.BlockSpec((1, N_HEAD, 1), cur)]
                      + [pl.BlockSpec((1, 512, PAGE), pool(j)) for j in range(g)]
                      + [pl.BlockSpec((1, 512, PAGE), pool(j)) for j in range(g)]
                      + [pl.BlockSpec((1, N_HEAD, PAGE), pool(j)) for j in range(g)]),
            out_specs=col,
            scratch_shapes=[pltpu.VMEM((N_HEAD, 1, 1), f32), pltpu.VMEM((N_HEAD, 1, 1), f32),
                            pltpu.VMEM((512, PAGE), f32), pltpu.VMEM((N_HEAD, 1), f32)]),
        out_shape=jax.ShapeDtypeStruct((nb, 512, 1), f32),
        compiler_params=_params(("parallel", "arbitrary"), VMEM_LIMIT),
        name="fox_paged",
    )(page_table, q, knew, vnew, lfnew, *([kpool] * g), *([vpool] * g), *([lfpool] * g))


def _merge_kernel(x_ref, or_ref, of_ref, ug_ref, wr_ref, wf_ref, wo_ref, o_ref):
    br = _dot(or_ref[...].astype(bf16), wr_ref[...])
    bf = _dot(of_ref[...].astype(bf16), wf_ref[...])
    ug = ug_ref[...]
    merged = jax.nn.sigmoid(ug[:, :D_MODEL]) * br + jax.nn.sigmoid(ug[:, D_MODEL:]) * bf
    o_ref[...] = x_ref[...] + _dot(merged.astype(bf16), wo_ref[...])


def merge_out(x, o_r, o_f, u, wr, wf, wo, tm):
    m = x.shape[0]
    row = lambda i: (i, 0)
    cst = lambda i: (0, 0)
    return pl.pallas_call(
        _merge_kernel,
        grid=(m // tm,),
        in_specs=[pl.BlockSpec((tm, D_MODEL), row), pl.BlockSpec((tm, 512), row), pl.BlockSpec((tm, 512), row),
                  pl.BlockSpec((tm, 2 * D_MODEL), lambda i: (i, COL_G // (2 * D_MODEL))),
                  pl.BlockSpec((512, D_MODEL), cst), pl.BlockSpec((512, D_MODEL), cst),
                  pl.BlockSpec((D_MODEL, D_MODEL), cst)],
        out_specs=pl.BlockSpec((tm, D_MODEL), row),
        out_shape=jax.ShapeDtypeStruct((m, D_MODEL), f32),
        compiler_params=_params(("parallel",), VMEM_LIMIT),
        name="merge_out",
    )(x, o_r, o_f, u, wr, wf, wo)


def _ffn_kernel(x_ref, g2_ref, gf_ref, wu_ref, wd_ref, o_ref, h_ref, acc_ref):
    j = pl.program_id(1)

    @pl.when(j == 0)
    def _():
        x = x_ref[...]
        ms = jnp.mean(x * x, axis=-1, keepdims=True)
        h_ref[...] = (x * lax.rsqrt(ms + RMS_EPS) * g2_ref[...]).astype(bf16)
        acc_ref[...] = jnp.zeros_like(acc_ref)

    hid = jnp.maximum(_dot(h_ref[...], wu_ref[...]), 0.0)
    acc_ref[...] += _dot((hid * hid).astype(bf16), wd_ref[...])

    @pl.when(j == pl.num_programs(1) - 1)
    def _():
        x2 = x_ref[...] + acc_ref[...]
        ms = jnp.mean(x2 * x2, axis=-1, keepdims=True)
        o_ref[...] = x2 * lax.rsqrt(ms + RMS_EPS) * gf_ref[...]


def ffn_final(x, g2, gf, wu, wd, tm, tf):
    m = x.shape[0]
    return pl.pallas_call(
        _ffn_kernel,
        grid=(m // tm, D_FF // tf),
        in_specs=[pl.BlockSpec((tm, D_MODEL), lambda i, j: (i, 0)),
                  pl.BlockSpec((1, D_MODEL), lambda i, j: (0, 0)), pl.BlockSpec((1, D_MODEL), lambda i, j: (0, 0)),
                  pl.BlockSpec((D_MODEL, tf), lambda i, j: (0, j)), pl.BlockSpec((tf, D_MODEL), lambda i, j: (j, 0))],
        out_specs=pl.BlockSpec((tm, D_MODEL), lambda i, j: (i, 0)),
        out_shape=jax.ShapeDtypeStruct((m, D_MODEL), f32),
        scratch_shapes=[pltpu.VMEM((tm, D_MODEL), bf16), pltpu.VMEM((tm, D_MODEL), f32)],
        compiler_params=_params(("parallel", "arbitrary"), VMEM_LIMIT),
        name="ffn_final",
    )(x, g2, gf, wu, wd)


def kernel(x_prompt, x_sample, state_shift, state_wkv, cache_k, cache_v, cache_logf, page_table, meta_tokens,
           norm_mix, w_in, mu_shift, w0, w_decay_up, a0, w_a_up, w_g_up, k_k, k_a, r_k, ln_x_w, ln_x_b, b_forget,
           w_br_rwkv, w_br_fox, w_out, norm_ffn, w_ffn_up, w_ffn_down, norm_final):
    assert norm_mix.shape[0] == 1, "single layer"
    nb, seq, _ = x_prompt.shape
    ns = x_sample.shape[0]
    tp = FRONT_PAD + N_META + seq
    tlen = N_META + seq
    tm = 640

    wi = w_in[0]
    wi = jnp.concatenate([wi[:, 0:1792], wi[:, 3328:3336], jnp.zeros((D_MODEL, 248), f32),
                          wi[:, 3336:5384], wi[:, 1792:3328]], axis=1).astype(bf16)
    row2 = lambda a: a.reshape(1, -1)
    lane_head = jnp.arange(512, dtype=jnp.int32) // HEAD
    pr = dict(mu=row2(mu_shift[0]), w0=row2(w0[0]), wd=w_decay_up[0], a0=row2(a0[0]), wa=w_a_up[0], wg=w_g_up[0],
              k_k=row2(k_k[0]), k_a=row2(k_a[0]), r_k=row2(r_k[0]),
              bd=(lane_head[:, None] == lane_head[None, :]).astype(bf16))
    lnw = ln_x_w[0].reshape(N_HEAD, 1, HEAD)
    lnb = ln_x_b[0].reshape(N_HEAD, 1, HEAD)
    g_mix, g_ffn, g_fin = row2(norm_mix[0]), row2(norm_ffn[0]), row2(norm_final)
    wr, wf, wo = w_br_rwkv[0].astype(bf16), w_br_fox[0].astype(bf16), w_out[0].astype(bf16)
    wu, wdn = w_ffn_up[0].astype(bf16), w_ffn_down[0].astype(bf16)

    meta = jnp.broadcast_to(meta_tokens[None], (nb, N_META, D_MODEL))
    xp = jnp.concatenate([jnp.zeros((nb, FRONT_PAD, D_MODEL), f32), meta, x_prompt], axis=1)
    xp2 = xp.reshape(nb * tp, D_MODEL)
    u2 = norm_matmul(xp2, g_mix, wi, 1664, 1408)
    u3 = u2.reshape(nb, tp, D_INP)
    us2 = norm_matmul(x_sample.reshape(ns, D_MODEL), g_mix, wi, ns, 1408)

    r, lw, k2, v, kkn, b, g, bonus = rwkv_prep(u3, None, pr, 320)
    flat = lambda a: a.reshape(nb * N_HEAD, tp, HEAD)
    pw, qy = wkv_chunks(flat(r), flat(lw), flat(k2), flat(v), flat(kkn), flat(b), 13)
    y, s_kv = wkv_serial(pw, qy, 5)
    o_r = rwkv_post(y.reshape(nb, N_HEAD, tp, HEAD), g, bonus, lnw, lnb, tm)
    new_wkv_p = jnp.swapaxes(s_kv, 1, 2).reshape(1, nb, N_HEAD, HEAD, HEAD)

    us3 = us2.reshape(1, ns, D_INP)
    sp = rwkv_prep(us3, state_shift[0].reshape(1, ns, D_SHIFT), pr, ns)
    rs, lws, ks, vs, kks, bs, gs, bonus_s = sp
    rowify = lambda a: jnp.transpose(a[0], (1, 0, 2)).reshape(ns, N_HEAD, 1, HEAD)
    s_new, ys = wkv_step(state_wkv[0], rowify(rs), rowify(lws), rowify(ks), rowify(kks), rowify(bs),
                         rowify(vs).reshape(ns, N_HEAD, HEAD, 1))
    ys_hm = jnp.transpose(ys.reshape(ns, N_HEAD, HEAD), (1, 0, 2))[None]
    o_r_s = rwkv_post(ys_hm, gs, bonus_s, lnw, lnb, ns)[0]

    fl_t = jnp.transpose(u3[:, :, COL_F:COL_F + N_HEAD], (0, 2, 1)).reshape(nb * N_HEAD, tp)
    bias_rows = jnp.tile(b_forget[0], nb).reshape(nb * N_HEAD, 1)
    lf_t, c_t = logf_cumsum(fl_t, bias_rows)
    pad_key = jnp.arange(tp, dtype=jnp.int32)[None, :] < FRONT_PAD
    ck = jnp.where(pad_key, -NEG, c_t * LOG2E).reshape(nb, N_HEAD // 2, 2, tp)
    o_f = fox_prompt(u3, jnp.swapaxes(ck, 2, 3), tm, 2)

    qs = us2[:, COL_Q:COL_Q + 512] * (HEAD ** -0.5)
    k_s = us2[:, COL_Q + 512:COL_Q + 1024]
    v_s = us2[:, COL_Q + 1024:COL_Q + 1536]
    lf_s = jax.nn.log_sigmoid(us2[:, COL_F:COL_F + N_HEAD] + b_forget[0][None, :])
    n_pool = cache_k.shape[1]
    kpool = jnp.transpose(cache_k[0], (0, 2, 3, 1)).reshape(n_pool, 512, PAGE)
    vpool = jnp.transpose(cache_v[0], (0, 2, 3, 1)).reshape(n_pool, 512, PAGE)
    o_f_s = fox_paged(page_table, qs.reshape(ns, 512, 1), k_s.reshape(ns, 512, 1), v_s.reshape(ns, 512, 1),
                      lf_s.reshape(ns, N_HEAD, 1), kpool, vpool, jnp.swapaxes(cache_logf[0], 1, 2),
                      8).reshape(ns, 512)

    x1 = merge_out(xp2, o_r.reshape(nb * tp, 512), o_f.reshape(nb * tp, 512), u2, wr, wf, wo, tm)
    yp = ffn_final(x1, g_ffn, g_fin, wu, wdn, 1664, 512).reshape(nb, tp, D_MODEL)
    x1s = merge_out(x_sample.reshape(ns, D_MODEL), o_r_s, o_f_s, us2, wr, wf, wo, ns)
    ysamp = ffn_final(x1s, g_ffn, g_fin, wu, wdn, ns, 512)

    y_prompt = yp[:, ROW0:]
    y_sample = ysamp.reshape(ns, 1, D_MODEL)
    new_shift_p = u3[:, tp - 1, :D_SHIFT][None]
    k_p = u3[:, FRONT_PAD:, COL_Q + 512:COL_Q + 1024].reshape(1, nb, tlen, N_HEAD, HEAD)
    v_p = u3[:, FRONT_PAD:, COL_Q + 1024:COL_Q + 1536].reshape(1, nb, tlen, N_HEAD, HEAD)
    lf_p = jnp.transpose(lf_t.reshape(nb, N_HEAD, tp), (0, 2, 1))[:, FRONT_PAD:][None]
    return (y_prompt, y_sample, new_shift_p, new_wkv_p, k_p, v_p, lf_p,
            us2[:, :D_SHIFT][None], s_new[None], k_s.reshape(1, ns, 1, N_HEAD, HEAD),
            v_s.reshape(1, ns, 1, N_HEAD, HEAD), lf_s.reshape(1, ns, 1, N_HEAD))
```

```python
import functools

import jax
import jax.numpy as jnp
from jax import lax
from jax.experimental import pallas as pl
from jax.experimental.pallas import tpu as pltpu

f32 = jnp.float32
bf16 = jnp.bfloat16

D_MODEL = 1024
N_META = 16
D_RWKV = 512
HEAD = 64
N_HEAD = 8
D_SHIFT = 1792
D_FF = 4096
RMS_EPS = 1e-6
GN_EPS = 64e-5
PAGE = 128

FRONT_PAD = 112
ROW0 = FRONT_PAD + N_META
CHUNK = 64
NEG = -1e30
LOG2E = 1.4426950408889634

COL_R = 0
COL_F = 1792
COL_G = 2048
COL_Q = 4096
D_INP = 5632

VMEM_LIMIT = 56 * 1024 * 1024

NN = (((1,), (0,)), ((), ()))
NT = (((1,), (1,)), ((), ()))
TN = (((0,), (0,)), ((), ()))


def _dot(a, b, dims=NN):
    return lax.dot_general(a, b, dims, preferred_element_type=f32)


def _split2(x):
    hi = x.astype(bf16)
    lo = (x - hi.astype(f32)).astype(bf16)
    return hi, lo


def _split3(x):
    hi = x.astype(bf16)
    r1 = x - hi.astype(f32)
    mid = r1.astype(bf16)
    lo = (r1 - mid.astype(f32)).astype(bf16)
    return hi, mid, lo


def _dot3(a, b, dims=NN):
    ah, al = _split2(a)
    bh, bl = _split2(b)
    return _dot(ah, bh, dims) + (_dot(ah, bl, dims) + _dot(al, bh, dims))


def _dot1(a, b, dims=NN):
    return _dot(a.astype(bf16), b.astype(bf16), dims)


def _dot_onesr(x, ones_bf16, dims=NN):
    hi, mid, lo = _split3(x)
    return _dot(hi, ones_bf16, dims) + (_dot(mid, ones_bf16, dims) + _dot(lo, ones_bf16, dims))


def _dot_onesl(ones_bf16, x, dims=NN):
    hi, mid, lo = _split3(x)
    return _dot(ones_bf16, hi, dims) + (_dot(ones_bf16, mid, dims) + _dot(ones_bf16, lo, dims))


def _params(sem, vmem=None):
    return pltpu.CompilerParams(dimension_semantics=sem, vmem_limit_bytes=vmem)


def _norm_matmul_kernel(x_ref, g_ref, w_ref, o_ref, h_ref):
    @pl.when(pl.program_id(1) == 0)
    def _():
        x = x_ref[...]
        ms = jnp.mean(x * x, axis=-1, keepdims=True)
        h_ref[...] = (x * lax.rsqrt(ms + RMS_EPS) * g_ref[...]).astype(bf16)

    o_ref[...] = _dot(h_ref[...], w_ref[...])


def norm_matmul(x, g, w, tm, tn):
    m, d = x.shape
    n = w.shape[1]
    return pl.pallas_call(
        _norm_matmul_kernel,
        grid=(m // tm, n // tn),
        in_specs=[pl.BlockSpec((tm, d), lambda i, j: (i, 0)),
                  pl.BlockSpec((1, d), lambda i, j: (0, 0)),
                  pl.BlockSpec((d, tn), lambda i, j: (0, j))],
        out_specs=pl.BlockSpec((tm, tn), lambda i, j: (i, j)),
        out_shape=jax.ShapeDtypeStruct((m, n), f32),
        scratch_shapes=[pltpu.VMEM((tm, d), bf16)],
        compiler_params=_params(("parallel", "arbitrary"), VMEM_LIMIT),
        name="norm_matmul",
    )(x, g, w)


def _prep_kernel(u_ref, up_ref, mu_ref, w0_ref, wd_ref, a0_ref, wa_ref, wg_ref, kk_ref, ka_ref, rk_ref,
                 bd_ref, r_o, lw_o, k_o, v_o, kkn_o, b_o, g_o, bonus_o, *, prev_is_tail):
    u = u_ref[0]
    if prev_is_tail:
        first = jnp.where(pl.program_id(1) == 0, 0.0, up_ref[0, 7:8, :])
        rowi = lax.broadcasted_iota(jnp.int32, u.shape, 0)
        up = jnp.where(rowi == 0, first, pltpu.roll(u, 1, 0))
    else:
        up = up_ref[0]
    us = u + mu_ref[...] * (up - u)
    r = us[:, 0:512]
    k = us[:, 512:1024]
    v = us[:, 1024:1536]
    zw = us[:, 1536:1600]
    za = us[:, 1600:1664]
    zg = us[:, 1664:1792]
    bd = bd_ref[...]

    z = -(w0_ref[...] + _dot3(jnp.tanh(zw), wd_ref[...]))
    w_raw = -(jnp.maximum(z, 0.0) + jnp.log1p(jnp.exp(-jnp.abs(z)))) - 0.5
    lw = -jnp.exp(w_raw)
    a = jax.nn.sigmoid(a0_ref[...] + _dot3(za, wa_ref[...]))
    g = _dot3(jax.nn.sigmoid(zg), wg_ref[...])
    kk = k * kk_ref[...]
    ss = _dot_onesr(kk * kk, bd)
    kkn = kk / jnp.maximum(jnp.sqrt(ss), 1e-12)
    k2 = k * (1.0 + (a - 1.0) * ka_ref[...])
    b = kkn * a
    bonus = _dot_onesr(r * k2 * rk_ref[...], bd) * v

    for h in range(N_HEAD):
        sl = slice(HEAD * h, HEAD * (h + 1))
        r_o[0, h] = r[:, sl]
        lw_o[0, h] = lw[:, sl]
        k_o[0, h] = k2[:, sl]
        v_o[0, h] = v[:, sl]
        kkn_o[0, h] = kkn[:, sl]
        b_o[0, h] = b[:, sl]
        g_o[0, h] = g[:, sl]
        bonus_o[0, h] = bonus[:, sl]


def rwkv_prep(u3, up3, pr, tm):
    nb, t = u3.shape[0], u3.shape[1]
    row = lambda i, j: (i, j, 0)
    if up3 is None:
        up_arr = u3
        up_spec = pl.BlockSpec((1, 8, D_SHIFT), lambda i, j: (i, jnp.maximum(j * (tm // 8) - 1, 0), 0))
    else:
        up_arr = up3
        up_spec = pl.BlockSpec((1, tm, D_SHIFT), row)
    cst = lambda i, j: (0, 0)
    hm = jax.ShapeDtypeStruct((nb, N_HEAD, t, HEAD), f32)
    hm_spec = pl.BlockSpec((1, N_HEAD, tm, HEAD), lambda i, j: (i, 0, j, 0))
    vec = lambda n: pl.BlockSpec((1, n), cst)
    return pl.pallas_call(
        functools.partial(_prep_kernel, prev_is_tail=up3 is None),
        grid=(nb, t // tm),
        in_specs=[pl.BlockSpec((1, tm, D_SHIFT), row), up_spec,
                  vec(D_SHIFT), vec(512), pl.BlockSpec((64, 512), cst), vec(512), pl.BlockSpec((64, 512), cst),
                  pl.BlockSpec((128, 512), cst), vec(512), vec(512), vec(512), pl.BlockSpec((512, 512), cst)],
        out_specs=[hm_spec] * 8,
        out_shape=[hm] * 8,
        compiler_params=_params(("parallel", "parallel"), VMEM_LIMIT),
        name="rwkv_prep",
    )(u3, up_arr, pr["mu"], pr["w0"], pr["wd"], pr["a0"], pr["wa"], pr["wg"], pr["k_k"], pr["k_a"], pr["r_k"],
      pr["bd"])


def _each(f, *lists):
    return [f(*xs) for xs in zip(*lists)]


def _wkv_chunk_kernel(r_ref, lw_ref, k_ref, v_ref, kk_ref, b_ref, pw_ref, qy_ref, *, nc):
    ii = lax.broadcasted_iota(jnp.int32, (CHUNK, CHUNK), 0)
    jj = lax.broadcasted_iota(jnp.int32, (CHUNK, CHUNK), 1)
    incl = jj <= ii
    strict = jj < ii
    ones_incl = incl.astype(bf16)
    same_blk = (ii >> 4) == (jj >> 4)
    eye = (ii == jj).astype(f32)
    rows = [slice(c * CHUNK, (c + 1) * CHUNK) for c in range(nc)]
    r = [r_ref[0, s, :] for s in rows]
    lw = [lw_ref[0, s, :] for s in rows]
    k = [k_ref[0, s, :] for s in rows]
    v = [v_ref[0, s, :] for s in rows]
    kk = [kk_ref[0, s, :] for s in rows]
    b = [b_ref[0, s, :] for s in rows]

    gcum = _each(lambda x: _dot_onesl(ones_incl, x), lw)
    gend = _each(lambda x: x[CHUNK - 1:CHUNK, :], gcum)
    kkt = _each(lambda x, gc, l: x * jnp.exp(gc - l), kk, gcum, lw)
    rt = _each(lambda x, gc: x * jnp.exp(gc), r, gcum)
    em = _each(lambda gc: jnp.exp(-gc), gcum)
    kh = _each(jnp.multiply, k, em)
    bh = _each(jnp.multiply, b, em)
    ec = _each(lambda ge, gc: jnp.exp(ge - gc), gend, gcum)
    kg = _each(jnp.multiply, k, ec)
    bg = _each(jnp.multiply, b, ec)

    lhs = _each(lambda x, y: jnp.concatenate([x, y], axis=0), kkt, rt)
    ab = _each(lambda x, y: _dot1(x, y, NT), lhs, bh)
    ak = _each(lambda x, y: _dot1(x, y, NT), lhs, kh)
    a_kb = _each(lambda x: jnp.where(strict, x[:CHUNK], 0.0), ab)
    a_rb = _each(lambda x: jnp.where(incl, x[CHUNK:], 0.0), ab)
    a_kk = _each(lambda x: jnp.where(strict, x[:CHUNK], 0.0), ak)
    a_rk = _each(lambda x: jnp.where(incl, x[CHUNK:], 0.0), ak)

    dg = _each(lambda x: jnp.where(same_blk, x, 0.0), a_kb)
    lo = _each(jnp.subtract, a_kb, dg)
    n1 = _each(jnp.negative, dg)
    n2 = _each(lambda x: _dot1(x, x), n1)
    n4 = _each(lambda x: _dot1(x, x), n2)
    n8 = _each(lambda x: _dot1(x, x), n4)
    t12 = _each(lambda x, y: _dot1(eye + x, eye + y), n1, n2)
    t48 = _each(lambda x, y: _dot1(eye + x, eye + y), n4, n8)
    td = _each(_dot1, t12, t48)
    x1 = _each(lambda x, y: -_dot1(x, y), td, lo)
    x2 = _each(lambda x: _dot1(x, x), x1)
    xx = _each(lambda x, y: _dot1(eye + x, eye + y), x1, x2)
    tinv = _each(_dot1, xx, td)

    akv = _each(_dot1, a_kk, v)
    w1u = _each(lambda t, x, y: _dot1(t, jnp.concatenate([x, y], axis=1)), tinv, kkt, akv)
    z = _each(lambda x, y, w: _dot1(jnp.concatenate([x.T, y], axis=0), w), bg, a_rb, w1u)
    kv = _each(lambda x, y, w: _dot1(jnp.concatenate([x.T, y], axis=0), w), kg, a_rk, v)
    for c in range(nc):
        base = jnp.concatenate([eye * jnp.exp(gend[c]), rt[c]], axis=0)
        pw_ref[0, c] = base - z[c][:, :CHUNK]
        qy_ref[0, c] = kv[c] - z[c][:, CHUNK:]


def wkv_chunks(r, lw, k, v, kk, b, nc):
    nbh, t, _ = r.shape
    nchunk = t // CHUNK
    in_spec = pl.BlockSpec((1, nc * CHUNK, HEAD), lambda i, j: (i, j, 0))
    out_spec = pl.BlockSpec((1, nc, 2 * CHUNK, HEAD), lambda i, j: (i, j, 0, 0))
    out = jax.ShapeDtypeStruct((nbh, nchunk, 2 * CHUNK, HEAD), f32)
    return pl.pallas_call(
        functools.partial(_wkv_chunk_kernel, nc=nc),
        grid=(nbh, nchunk // nc),
        in_specs=[in_spec] * 6,
        out_specs=[out_spec, out_spec],
        out_shape=[out, out],
        compiler_params=_params(("parallel", "parallel"), VMEM_LIMIT),
        name="wkv_chunks",
    )(r, lw, k, v, kk, b)


def _wkv_serial_kernel(pw_ref, qy_ref, y_ref, s_ref, st_ref, *, nbh, nc):
    @pl.when(pl.program_id(0) == 0)
    def _():
        st_ref[...] = jnp.zeros_like(st_ref)

    def body(c, carry):
        for i in range(nbh):
            z = _dot3(pw_ref[i, c], st_ref[i]) + qy_ref[i, c]
            st_ref[i] = z[:CHUNK]
            y_ref[i, pl.ds(pl.multiple_of(c * CHUNK, CHUNK), CHUNK), :] = z[CHUNK:]
        return carry

    lax.fori_loop(0, nc, body, 0)

    @pl.when(pl.program_id(0) == pl.num_programs(0) - 1)
    def _():
        s_ref[...] = st_ref[...]


def wkv_serial(pw, qy, nc):
    nbh, nchunk = pw.shape[0], pw.shape[1]
    blk = pl.BlockSpec((nbh, nc, 2 * CHUNK, HEAD), lambda j: (0, j, 0, 0))
    return pl.pallas_call(
        functools.partial(_wkv_serial_kernel, nbh=nbh, nc=nc),
        grid=(nchunk // nc,),
        in_specs=[blk, blk],
        out_specs=[pl.BlockSpec((nbh, nc * CHUNK, HEAD), lambda j: (0, j, 0)),
                   pl.BlockSpec((nbh, HEAD, HEAD), lambda j: (0, 0, 0))],
        out_shape=[jax.ShapeDtypeStruct((nbh, nchunk * CHUNK, HEAD), f32),
                   jax.ShapeDtypeStruct((nbh, HEAD, HEAD), f32)],
        scratch_shapes=[pltpu.VMEM((nbh, HEAD, HEAD), f32)],
        compiler_params=_params(("arbitrary",), VMEM_LIMIT),
        name="wkv_serial",
    )(pw, qy)


def _wkv_step_kernel(s_ref, r_ref, lw_ref, k_ref, kk_ref, b_ref, vc_ref, so_ref, y_ref):
    s = s_ref[0]
    skk = jnp.sum(s * kk_ref[0], axis=-1, keepdims=True)
    s1 = s * jnp.exp(lw_ref[0]) - skk * b_ref[0] + vc_ref[0] * k_ref[0]
    so_ref[0] = s1
    y_ref[0] = jnp.sum(s1 * r_ref[0], axis=-1, keepdims=True)


def wkv_step(s, r, lw, k, kk, b, vcol):
    n = s.shape[0]
    row = pl.BlockSpec((1, N_HEAD, 1, HEAD), lambda i: (i, 0, 0, 0))
    col = pl.BlockSpec((1, N_HEAD, HEAD, 1), lambda i: (i, 0, 0, 0))
    mat = pl.BlockSpec((1, N_HEAD, HEAD, HEAD), lambda i: (i, 0, 0, 0))
    return pl.pallas_call(
        _wkv_step_kernel,
        grid=(n,),
        in_specs=[mat, row, row, row, row, row, col],
        out_specs=[mat, col],
        out_shape=[jax.ShapeDtypeStruct(s.shape, f32), jax.ShapeDtypeStruct((n, N_HEAD, HEAD, 1), f32)],
        compiler_params=_params(("parallel",)),
        name="wkv_step",
    )(s, r, lw, k, kk, b, vcol)


def _post_kernel(y_ref, g_ref, bonus_ref, lnw_ref, lnb_ref, o_ref):
    parts = []
    for h in range(N_HEAD):
        y = y_ref[0, h]
        mu = jnp.mean(y, axis=-1, keepdims=True)
        yc = y - mu
        var = jnp.mean(yc * yc, axis=-1, keepdims=True)
        yn = yc * lax.rsqrt(var + GN_EPS)
        parts.append((yn * lnw_ref[h] + lnb_ref[h] + bonus_ref[0, h]) * g_ref[0, h])
    o_ref[0] = jnp.concatenate(parts, axis=-1)


def rwkv_post(y, g, bonus, lnw, lnb, tm):
    nb, _, t, _ = y.shape
    hm = pl.BlockSpec((1, N_HEAD, tm, HEAD), lambda i, j: (i, 0, j, 0))
    par = pl.BlockSpec((N_HEAD, 1, HEAD), lambda i, j: (0, 0, 0))
    return pl.pallas_call(
        _post_kernel,
        grid=(nb, t // tm),
        in_specs=[hm, hm, hm, par, par],
        out_specs=pl.BlockSpec((1, tm, D_RWKV), lambda i, j: (i, j, 0)),
        out_shape=jax.ShapeDtypeStruct((nb, t, D_RWKV), f32),
        compiler_params=_params(("parallel", "parallel"), VMEM_LIMIT),
        name="rwkv_post",
    )(y, g, bonus, lnw, lnb)


def _logf_kernel(fl_ref, b_ref, lf_ref, c_ref, *, nblk):
    ii = lax.broadcasted_iota(jnp.int32, (128, 128), 0)
    jj = lax.broadcasted_iota(jnp.int32, (128, 128), 1)
    upper = (ii <= jj).astype(bf16)
    carry = jnp.zeros((fl_ref.shape[0], 1), f32)
    for blk in range(nblk):
        sl = slice(128 * blk, 128 * (blk + 1))
        lf = jax.nn.log_sigmoid(fl_ref[:, sl] + b_ref[...])
        lf_ref[:, sl] = lf
        cs = _dot_onesr(lf, upper) + carry
        c_ref[:, sl] = cs
        carry = cs[:, 127:128]


def logf_cumsum(fl_t, bias):
    n, t = fl_t.shape
    out = jax.ShapeDtypeStruct((n, t), f32)
    return pl.pallas_call(
        functools.partial(_logf_kernel, nblk=t // 128),
        out_shape=[out, out],
        name="logf_cumsum",
    )(fl_t, bias)


SUM_ROWS = 16


def _fox_kernel(qi_ref, ki_ref, q_ref, k_ref, v_ref, ck_ref, o_ref, m_ref, acc_ref, *, tq, tk, nsub):
    step_id = pl.program_id(2)
    qi = qi_ref[step_id]
    ki = ki_ref[step_id]

    @pl.when(ki == 0)
    def _():
        m_ref[...] = jnp.full_like(m_ref, NEG)
        acc_ref[...] = jnp.zeros_like(acc_ref)

    def step(masked):
        lane_lo = lax.broadcasted_iota(jnp.int32, (tq, 2 * HEAD), 1) < HEAD
        q = q_ref[0] * (HEAD ** -0.5 * LOG2E)
        qes = [jnp.where(lane_lo == (e == 0), q, 0.0).astype(bf16) for e in range(2)]
        ts = tk // nsub

        def first_col(sub):
            return (sub * ts) // 128 * 128 if masked else 0

        def scores(sub):
            rs = slice(sub * ts, (sub + 1) * ts)
            lo = first_col(sub)
            kb = k_ref[0, rs, :].astype(bf16)
            return [_dot(kb, qes[e][lo:, :], NT) for e in range(2)]

        s_cur = scores(0)
        for sub in range(nsub):
            s_next = scores(sub + 1) if sub + 1 < nsub else None
            rs = slice(sub * ts, (sub + 1) * ts)
            lo = first_col(sub)
            vt = v_ref[0, rs, :].T.astype(bf16)
            ones = jnp.ones((SUM_ROWS, ts), bf16)
            ps, alphas = [], []
            for e in range(2):
                s = s_cur[e] - ck_ref[0, 0, rs, e:e + 1]
                if masked:
                    row = lax.broadcasted_iota(jnp.int32, s.shape, 0) + sub * ts
                    col = lax.broadcasted_iota(jnp.int32, s.shape, 1) + lo
                    s = jnp.where(row <= col, s, NEG)
                m_old = m_ref[e, :, lo:]
                m_new = jnp.maximum(m_old, jnp.max(s, axis=0, keepdims=True))
                alphas.append(jnp.exp2(m_old - m_new))
                ps.append(jnp.exp2(s - m_new).astype(bf16))
                m_ref[e, :, lo:] = m_new
            for e in range(2):
                vte = jnp.concatenate([vt[HEAD * e:HEAD * (e + 1), :], ones], axis=0)
                acc_ref[e, :, lo:] = alphas[e] * acc_ref[e, :, lo:] + _dot(vte, ps[e])
            s_cur = s_next

    @pl.when(ki < qi)
    def _():
        step(False)

    @pl.when(ki == qi)
    def _():
        step(True)
        o = [acc_ref[e, 0:HEAD, :] / acc_ref[e, HEAD:HEAD + 1, :] for e in range(2)]
        o_ref[0] = jnp.concatenate(o, axis=0).T


def fox_prompt(u3, ck, tq, nsub):
    nb, t, _ = u3.shape
    nq = t // tq
    cq, ckk, cv = COL_Q // 128, (COL_Q + 512) // 128, (COL_Q + 1024) // 128
    pairs = [(i, j) for i in range(nq) for j in range(i + 1)]
    qi_tab = jnp.asarray([p[0] for p in pairs], jnp.int32)
    ki_tab = jnp.asarray([p[1] for p in pairs], jnp.int32)
    return pl.pallas_call(
        functools.partial(_fox_kernel, tq=tq, tk=tq, nsub=nsub),
        grid_spec=pltpu.PrefetchScalarGridSpec(
            num_scalar_prefetch=2,
            grid=(nb, N_HEAD // 2, len(pairs)),
            in_specs=[pl.BlockSpec((1, tq, 128), lambda b, h, s, qt, kt: (b, qt[s], cq + h)),
                      pl.BlockSpec((1, tq, 128), lambda b, h, s, qt, kt: (b, kt[s], ckk + h)),
                      pl.BlockSpec((1, tq, 128), lambda b, h, s, qt, kt: (b, kt[s], cv + h)),
                      pl.BlockSpec((1, 1, tq, 2), lambda b, h, s, qt, kt: (b, h, kt[s], 0))],
            out_specs=pl.BlockSpec((1, tq, 128), lambda b, h, s, qt, kt: (b, qt[s], h)),
            scratch_shapes=[pltpu.VMEM((2, 1, tq), f32), pltpu.VMEM((2, HEAD + SUM_ROWS, tq), f32)]),
        out_shape=jax.ShapeDtypeStruct((nb, t, 512), f32),
        compiler_params=_params(("parallel", "parallel", "arbitrary"), VMEM_LIMIT),
        name="fox_prompt",
    )(qi_tab, ki_tab, u3, u3, u3, ck)


def _paged_kernel(pt_ref, q_ref, kn_ref, vn_ref, lfn_ref, *refs, g):
    k_refs, v_refs, lf_refs = refs[0:g], refs[g:2 * g], refs[2 * g:3 * g]
    o_ref = refs[3 * g]
    m_ref, l_ref, acc_ref, car_ref = refs[3 * g + 1:]
    step = pl.program_id(1)
    heads = [slice(HEAD * h, HEAD * (h + 1)) for h in range(N_HEAD)]
    q = q_ref[0]

    @pl.when(step == 0)
    def _():
        lane = lax.broadcasted_iota(jnp.int32, acc_ref.shape, 1)
        acc_ref[...] = jnp.where(lane == 0, vn_ref[0], 0.0)
        qk = q * kn_ref[0]
        for h in range(N_HEAD):
            m_ref[h] = jnp.sum(qk[heads[h]], axis=0, keepdims=True)
            l_ref[h] = jnp.ones((1, 1), f32)
        car_ref[...] = lfn_ref[0]

    ii = lax.broadcasted_iota(jnp.int32, (PAGE, PAGE), 0)
    jj = lax.broadcasted_iota(jnp.int32, (PAGE, PAGE), 1)
    later = (ii > jj).astype(bf16)
    carry = car_ref[...]
    biases = []
    for j in range(g):
        lfp = lf_refs[j][0]
        biases.append(_dot_onesr(lfp, later) + carry)
        carry = carry + jnp.sum(lfp, axis=-1, keepdims=True)
    car_ref[...] = carry

    ss = [jnp.concatenate(
        [jnp.sum(k_refs[j][0, heads[h], :] * q[heads[h]], axis=0, keepdims=True) + biases[j][h:h + 1, :]
         for j in range(g)], axis=0) for h in range(N_HEAD)]
    m_olds = [m_ref[h] for h in range(N_HEAD)]
    m_news = [jnp.maximum(m, jnp.max(jnp.max(s, axis=-1, keepdims=True), axis=0, keepdims=True))
              for m, s in zip(m_olds, ss)]
    alphas = [jnp.exp(mo - mn) for mo, mn in zip(m_olds, m_news)]
    ps = [jnp.exp(s - mn) for s, mn in zip(ss, m_news)]
    for h in range(N_HEAD):
        l_ref[h] = alphas[h] * l_ref[h] + jnp.sum(jnp.sum(ps[h], axis=-1, keepdims=True), axis=0, keepdims=True)
        m_ref[h] = m_news[h]
    for h in range(N_HEAD):
        acc = alphas[h] * acc_ref[heads[h], :]
        for j in range(g):
            acc = acc + v_refs[j][0, heads[h], :] * ps[h][j:j + 1, :]
        acc_ref[heads[h], :] = acc

    @pl.when(step == pl.num_programs(1) - 1)
    def _():
        for h in range(N_HEAD):
            o_ref[0, heads[h], :] = jnp.sum(acc_ref[heads[h], :], axis=-1, keepdims=True) / l_ref[h]


def fox_paged(page_table, q, knew, vnew, lfnew, kpool, vpool, lfpool, g):
    nb, npages = page_table.shape
    last = npages - 1
    cur = lambda b, i, pt: (b, 0, 0)
    pool = lambda j: (lambda b, i, pt: (pt[b, last - (i * g + j)], 0, 0))
    col = pl.BlockSpec((1, 512, 1), cur)
    return pl.pallas_call(
        functools.partial(_paged_kernel, g=g),
        grid_spec=pltpu.PrefetchScalarGridSpec(
            num_scalar_prefetch=1,
            grid=(nb, npages // g),
            in_specs=([col, col, col, pl.BlockSpec((1, N_HEAD, 1), cur)]
                      + [pl.BlockSpec((1, 512, PAGE), pool(j)) for j in range(g)]
                      + [pl.BlockSpec((1, 512, PAGE), pool(j)) for j in range(g)]
                      + [pl.BlockSpec((1, N_HEAD, PAGE), pool(j)) for j in range(g)]),
            out_specs=col,
            scratch_shapes=[pltpu.VMEM((N_HEAD, 1, 1), f32), pltpu.VMEM((N_HEAD, 1, 1), f32),
                            pltpu.VMEM((512, PAGE), f32), pltpu.VMEM((N_HEAD, 1), f32)]),
        out_shape=jax.ShapeDtypeStruct((nb, 512, 1), f32),
        compiler_params=_params(("parallel", "arbitrary"), VMEM_LIMIT),
        name="fox_paged",
    )(page_table, q, knew, vnew, lfnew, *([kpool] * g), *([vpool] * g), *([lfpool] * g))


def _merge_kernel(x_ref, or_ref, of_ref, ug_ref, wr_ref, wf_ref, wo_ref, o_ref):
    br = _dot(or_ref[...].astype(bf16), wr_ref[...])
    bf = _dot(of_ref[...].astype(bf16), wf_ref[...])
    ug = ug_ref[...]
    merged = jax.nn.sigmoid(ug[:, :D_MODEL]) * br + jax.nn.sigmoid(ug[:, D_MODEL:]) * bf
    o_ref[...] = x_ref[...] + _dot(merged.astype(bf16), wo_ref[...])


def merge_out(x, o_r, o_f, u, wr, wf, wo, tm):
    m = x.shape[0]
    row = lambda i: (i, 0)
    cst = lambda i: (0, 0)
    return pl.pallas_call(
        _merge_kernel,
        grid=(m // tm,),
        in_specs=[pl.BlockSpec((tm, D_MODEL), row), pl.BlockSpec((tm, 512), row), pl.BlockSpec((tm, 512), row),
                  pl.BlockSpec((tm, 2 * D_MODEL), lambda i: (i, COL_G // (2 * D_MODEL))),
                  pl.BlockSpec((512, D_MODEL), cst), pl.BlockSpec((512, D_MODEL), cst),
                  pl.BlockSpec((D_MODEL, D_MODEL), cst)],
        out_specs=pl.BlockSpec((tm, D_MODEL), row),
        out_shape=jax.ShapeDtypeStruct((m, D_MODEL), f32),
        compiler_params=_params(("parallel",), VMEM_LIMIT),
        name="merge_out",
    )(x, o_r, o_f, u, wr, wf, wo)


def _ffn_kernel(x_ref, g2_ref, gf_ref, wu_ref, wd_ref, o_ref, h_ref, acc_ref):
    j = pl.program_id(1)

    @pl.when(j == 0)
    def _():
        x = x_ref[...]
        ms = jnp.mean(x * x, axis=-1, keepdims=True)
        h_ref[...] = (x * lax.rsqrt(ms + RMS_EPS) * g2_ref[...]).astype(bf16)
        acc_ref[...] = jnp.zeros_like(acc_ref)

    hid = jnp.maximum(_dot(h_ref[...], wu_ref[...]), 0.0)
    acc_ref[...] += _dot((hid * hid).astype(bf16), wd_ref[...])

    @pl.when(j == pl.num_programs(1) - 1)
    def _():
        x2 = x_ref[...] + acc_ref[...]
        ms = jnp.mean(x2 * x2, axis=-1, keepdims=True)
        o_ref[...] = x2 * lax.rsqrt(ms + RMS_EPS) * gf_ref[...]


def ffn_final(x, g2, gf, wu, wd, tm, tf):
    m = x.shape[0]
    return pl.pallas_call(
        _ffn_kernel,
        grid=(m // tm, D_FF // tf),
        in_specs=[pl.BlockSpec((tm, D_MODEL), lambda i, j: (i, 0)),
                  pl.BlockSpec((1, D_MODEL), lambda i, j: (0, 0)), pl.BlockSpec((1, D_MODEL), lambda i, j: (0, 0)),
                  pl.BlockSpec((D_MODEL, tf), lambda i, j: (0, j)), pl.BlockSpec((tf, D_MODEL), lambda i, j: (j, 0))],
        out_specs=pl.BlockSpec((tm, D_MODEL), lambda i, j: (i, 0)),
        out_shape=jax.ShapeDtypeStruct((m, D_MODEL), f32),
        scratch_shapes=[pltpu.VMEM((tm, D_MODEL), bf16), pltpu.VMEM((tm, D_MODEL), f32)],
        compiler_params=_params(("parallel", "arbitrary"), VMEM_LIMIT),
        name="ffn_final",
    )(x, g2, gf, wu, wd)


def kernel(x_prompt, x_sample, state_shift, state_wkv, cache_k, cache_v, cache_logf, page_table, meta_tokens,
           norm_mix, w_in, mu_shift, w0, w_decay_up, a0, w_a_up, w_g_up, k_k, k_a, r_k, ln_x_w, ln_x_b, b_forget,
           w_br_rwkv, w_br_fox, w_out, norm_ffn, w_ffn_up, w_ffn_down, norm_final):
    assert norm_mix.shape[0] == 1, "single layer"
    nb, seq, _ = x_prompt.shape
    ns = x_sample.shape[0]
    tp = FRONT_PAD + N_META + seq
    tlen = N_META + seq
    tm = 640

    wi = w_in[0]
    wi = jnp.concatenate([wi[:, 0:1792], wi[:, 3328:3336], jnp.zeros((D_MODEL, 248), f32),
                          wi[:, 3336:5384], wi[:, 1792:3328]], axis=1).astype(bf16)
    row2 = lambda a: a.reshape(1, -1)
    lane_head = jnp.arange(512, dtype=jnp.int32) // HEAD
    pr = dict(mu=row2(mu_shift[0]), w0=row2(w0[0]), wd=w_decay_up[0], a0=row2(a0[0]), wa=w_a_up[0], wg=w_g_up[0],
              k_k=row2(k_k[0]), k_a=row2(k_a[0]), r_k=row2(r_k[0]),
              bd=(lane_head[:, None] == lane_head[None, :]).astype(bf16))
    lnw = ln_x_w[0].reshape(N_HEAD, 1, HEAD)
    lnb = ln_x_b[0].reshape(N_HEAD, 1, HEAD)
    g_mix, g_ffn, g_fin = row2(norm_mix[0]), row2(norm_ffn[0]), row2(norm_final)
    wr, wf, wo = w_br_rwkv[0].astype(bf16), w_br_fox[0].astype(bf16), w_out[0].astype(bf16)
    wu, wdn = w_ffn_up[0].astype(bf16), w_ffn_down[0].astype(bf16)

    meta = jnp.broadcast_to(meta_tokens[None], (nb, N_META, D_MODEL))
    xp = jnp.concatenate([jnp.zeros((nb, FRONT_PAD, D_MODEL), f32), meta, x_prompt], axis=1)
    xp2 = xp.reshape(nb * tp, D_MODEL)
    u2 = norm_matmul(xp2, g_mix, wi, 1664, 1408)
    u3 = u2.reshape(nb, tp, D_INP)
    us2 = norm_matmul(x_sample.reshape(ns, D_MODEL), g_mix, wi, ns, 1408)

    r, lw, k2, v, kkn, b, g, bonus = rwkv_prep(u3, None, pr, 320)
    flat = lambda a: a.reshape(nb * N_HEAD, tp, HEAD)
    pw, qy = wkv_chunks(flat(r), flat(lw), flat(k2), flat(v), flat(kkn), flat(b), 13)
    y, s_kv = wkv_serial(pw, qy, 5)
    o_r = rwkv_post(y.reshape(nb, N_HEAD, tp, HEAD), g, bonus, lnw, lnb, tm)
    new_wkv_p = jnp.swapaxes(s_kv, 1, 2).reshape(1, nb, N_HEAD, HEAD, HEAD)

    us3 = us2.reshape(1, ns, D_INP)
    sp = rwkv_prep(us3, state_shift[0].reshape(1, ns, D_SHIFT), pr, ns)
    rs, lws, ks, vs, kks, bs, gs, bonus_s = sp
    rowify = lambda a: jnp.transpose(a[0], (1, 0, 2)).reshape(ns, N_HEAD, 1, HEAD)
    s_new, ys = wkv_step(state_wkv[0], rowify(rs), rowify(lws), rowify(ks), rowify(kks), rowify(bs),
                         rowify(vs).reshape(ns, N_HEAD, HEAD, 1))
    ys_hm = jnp.transpose(ys.reshape(ns, N_HEAD, HEAD), (1, 0, 2))[None]
    o_r_s = rwkv_post(ys_hm, gs, bonus_s, lnw, lnb, ns)[0]

    fl_t = jnp.transpose(u3[:, :, COL_F:COL_F + N_HEAD], (0, 2, 1)).reshape(nb * N_HEAD, tp)
    bias_rows = jnp.tile(b_forget[0], nb).reshape(nb * N_HEAD, 1)
    lf_t, c_t = logf_cumsum(fl_t, bias_rows)
    pad_key = jnp.arange(tp, dtype=jnp.int32)[None, :] < FRONT_PAD
    ck = jnp.where(pad_key, -NEG, c_t * LOG2E).reshape(nb, N_HEAD // 2, 2, tp)
    o_f = fox_prompt(u3, jnp.swapaxes(ck, 2, 3), 1664, 8)

    qs = us2[:, COL_Q:COL_Q + 512] * (HEAD ** -0.5)
    k_s = us2[:, COL_Q + 512:COL_Q + 1024]
    v_s = us2[:, COL_Q + 1024:COL_Q + 1536]
    lf_s = jax.nn.log_sigmoid(us2[:, COL_F:COL_F + N_HEAD] + b_forget[0][None, :])
    n_pool = cache_k.shape[1]
    kpool = jnp.transpose(cache_k[0], (0, 2, 3, 1)).reshape(n_pool, 512, PAGE)
    vpool = jnp.transpose(cache_v[0], (0, 2, 3, 1)).reshape(n_pool, 512, PAGE)
    o_f_s = fox_paged(page_table, qs.reshape(ns, 512, 1), k_s.reshape(ns, 512, 1), v_s.reshape(ns, 512, 1),
                      lf_s.reshape(ns, N_HEAD, 1), kpool, vpool, jnp.swapaxes(cache_logf[0], 1, 2),
                      16).reshape(ns, 512)

    x1 = merge_out(xp2, o_r.reshape(nb * tp, 512), o_f.reshape(nb * tp, 512), u2, wr, wf, wo, tm)
    yp = ffn_final(x1, g_ffn, g_fin, wu, wdn, 1664, 512).reshape(nb, tp, D_MODEL)
    x1s = merge_out(x_sample.reshape(ns, D_MODEL), o_r_s, o_f_s, us2, wr, wf, wo, ns)
    ysamp = ffn_final(x1s, g_ffn, g_fin, wu, wdn, ns, 512)

    y_prompt = yp[:, ROW0:]
    y_sample = ysamp.reshape(ns, 1, D_MODEL)
    new_shift_p = u3[:, tp - 1, :D_SHIFT][None]
    k_p = u3[:, FRONT_PAD:, COL_Q + 512:COL_Q + 1024].reshape(1, nb, tlen, N_HEAD, HEAD)
    v_p = u3[:, FRONT_PAD:, COL_Q + 1024:COL_Q + 1536].reshape(1, nb, tlen, N_HEAD, HEAD)
    lf_p = jnp.transpose(lf_t.reshape(nb, N_HEAD, tp), (0, 2, 1))[:, FRONT_PAD:][None]
    return (y_prompt, y_sample, new_shift_p, new_wkv_p, k_p, v_p, lf_p,
            us2[:, :D_SHIFT][None], s_new[None], k_s.reshape(1, ns, 1, N_HEAD, HEAD),
            v_s.reshape(1, ns, 1, N_HEAD, HEAD), lf_s.reshape(1, ns, 1, N_HEAD))
```

```python
import functools

import jax
import jax.numpy as jnp
from jax import lax
from jax.experimental import pallas as pl
from jax.experimental.pallas import tpu as pltpu

f32 = jnp.float32
bf16 = jnp.bfloat16

D_MODEL = 1024
N_META = 16
D_RWKV = 512
HEAD = 64
N_HEAD = 8
D_SHIFT = 1792
D_FF = 4096
RMS_EPS = 1e-6
GN_EPS = 64e-5
PAGE = 128

FRONT_PAD = 112
ROW0 = FRONT_PAD + N_META
CHUNK = 64
NEG = -1e30
LOG2E = 1.4426950408889634

COL_R = 0
COL_F = 1792
COL_G = 2048
COL_Q = 4096
D_INP = 5632

VMEM_LIMIT = 56 * 1024 * 1024

NN = (((1,), (0,)), ((), ()))
NT = (((1,), (1,)), ((), ()))
TN = (((0,), (0,)), ((), ()))


def _dot(a, b, dims=NN):
    return lax.dot_general(a, b, dims, preferred_element_type=f32)


def _split2(x):
    hi = x.astype(bf16)
    lo = (x - hi.astype(f32)).astype(bf16)
    return hi, lo


def _split3(x):
    hi = x.astype(bf16)
    r1 = x - hi.astype(f32)
    mid = r1.astype(bf16)
    lo = (r1 - mid.astype(f32)).astype(bf16)
    return hi, mid, lo


def _dot3(a, b, dims=NN):
    ah, al = _split2(a)
    bh, bl = _split2(b)
    return _dot(ah, bh, dims) + (_dot(ah, bl, dims) + _dot(al, bh, dims))


def _dot1(a, b, dims=NN):
    return _dot(a.astype(bf16), b.astype(bf16), dims)


def _dot_onesr(x, ones_bf16, dims=NN):
    hi, mid, lo = _split3(x)
    return _dot(hi, ones_bf16, dims) + (_dot(mid, ones_bf16, dims) + _dot(lo, ones_bf16, dims))


def _dot_onesl(ones_bf16, x, dims=NN):
    hi, mid, lo = _split3(x)
    return _dot(ones_bf16, hi, dims) + (_dot(ones_bf16, mid, dims) + _dot(ones_bf16, lo, dims))


def _params(sem, vmem=None):
    return pltpu.CompilerParams(dimension_semantics=sem, vmem_limit_bytes=vmem)


def _norm_matmul_kernel(x_ref, g_ref, w_ref, o_ref, h_ref):
    @pl.when(pl.program_id(1) == 0)
    def _():
        x = x_ref[...]
        ms = jnp.mean(x * x, axis=-1, keepdims=True)
        h_ref[...] = (x * lax.rsqrt(ms + RMS_EPS) * g_ref[...]).astype(bf16)

    o_ref[...] = _dot(h_ref[...], w_ref[...])


def norm_matmul(x, g, w, tm, tn):
    m, d = x.shape
    n = w.shape[1]
    return pl.pallas_call(
        _norm_matmul_kernel,
        grid=(m // tm, n // tn),
        in_specs=[pl.BlockSpec((tm, d), lambda i, j: (i, 0)),
                  pl.BlockSpec((1, d), lambda i, j: (0, 0)),
                  pl.BlockSpec((d, tn), lambda i, j: (0, j))],
        out_specs=pl.BlockSpec((tm, tn), lambda i, j: (i, j)),
        out_shape=jax.ShapeDtypeStruct((m, n), f32),
        scratch_shapes=[pltpu.VMEM((tm, d), bf16)],
        compiler_params=_params(("parallel", "arbitrary"), VMEM_LIMIT),
        name="norm_matmul",
    )(x, g, w)


def _prep_kernel(u_ref, up_ref, mu_ref, w0_ref, wd_ref, a0_ref, wa_ref, wg_ref, kk_ref, ka_ref, rk_ref,
                 bd_ref, r_o, lw_o, k_o, v_o, kkn_o, b_o, g_o, bonus_o, *, prev_is_tail):
    u = u_ref[0]
    if prev_is_tail:
        first = jnp.where(pl.program_id(1) == 0, 0.0, up_ref[0, 7:8, :])
        rowi = lax.broadcasted_iota(jnp.int32, u.shape, 0)
        up = jnp.where(rowi == 0, first, pltpu.roll(u, 1, 0))
    else:
        up = up_ref[0]
    us = u + mu_ref[...] * (up - u)
    r = us[:, 0:512]
    k = us[:, 512:1024]
    v = us[:, 1024:1536]
    zw = us[:, 1536:1600]
    za = us[:, 1600:1664]
    zg = us[:, 1664:1792]
    bd = bd_ref[...]

    z = -(w0_ref[...] + _dot3(jnp.tanh(zw), wd_ref[...]))
    w_raw = -(jnp.maximum(z, 0.0) + jnp.log1p(jnp.exp(-jnp.abs(z)))) - 0.5
    lw = -jnp.exp(w_raw)
    a = jax.nn.sigmoid(a0_ref[...] + _dot3(za, wa_ref[...]))
    g = _dot3(jax.nn.sigmoid(zg), wg_ref[...])
    kk = k * kk_ref[...]
    ss = _dot_onesr(kk * kk, bd)
    kkn = kk / jnp.maximum(jnp.sqrt(ss), 1e-12)
    k2 = k * (1.0 + (a - 1.0) * ka_ref[...])
    b = kkn * a
    bonus = _dot_onesr(r * k2 * rk_ref[...], bd) * v

    r_o[0] = r
    lw_o[0] = lw
    k_o[0] = k2
    v_o[0] = v
    kkn_o[0] = kkn
    b_o[0] = b
    g_o[0] = g
    bonus_o[0] = bonus


def rwkv_prep(u3, up3, pr, tm):
    nb, t = u3.shape[0], u3.shape[1]
    row = lambda i, j: (i, j, 0)
    if up3 is None:
        up_arr = u3
        up_spec = pl.BlockSpec((1, 8, D_SHIFT), lambda i, j: (i, jnp.maximum(j * (tm // 8) - 1, 0), 0))
    else:
        up_arr = up3
        up_spec = pl.BlockSpec((1, tm, D_SHIFT), row)
    cst = lambda i, j: (0, 0)
    out = jax.ShapeDtypeStruct((nb, t, D_RWKV), f32)
    out_spec = pl.BlockSpec((1, tm, D_RWKV), row)
    vec = lambda n: pl.BlockSpec((1, n), cst)
    return pl.pallas_call(
        functools.partial(_prep_kernel, prev_is_tail=up3 is None),
        grid=(nb, t // tm),
        in_specs=[pl.BlockSpec((1, tm, D_SHIFT), row), up_spec,
                  vec(D_SHIFT), vec(512), pl.BlockSpec((64, 512), cst), vec(512), pl.BlockSpec((64, 512), cst),
                  pl.BlockSpec((128, 512), cst), vec(512), vec(512), vec(512), pl.BlockSpec((512, 512), cst)],
        out_specs=[out_spec] * 8,
        out_shape=[out] * 8,
        compiler_params=_params(("parallel", "parallel"), VMEM_LIMIT),
        name="rwkv_prep",
    )(u3, up_arr, pr["mu"], pr["w0"], pr["wd"], pr["a0"], pr["wa"], pr["wg"], pr["k_k"], pr["k_a"], pr["r_k"],
      pr["bd"])


def _each(f, *lists):
    return [f(*xs) for xs in zip(*lists)]


def _wkv_chunk_kernel(r_ref, lw_ref, k_ref, v_ref, kk_ref, b_ref, pw_ref, qy_ref, *, nc):
    ii = lax.broadcasted_iota(jnp.int32, (CHUNK, CHUNK), 0)
    jj = lax.broadcasted_iota(jnp.int32, (CHUNK, CHUNK), 1)
    incl = jj <= ii
    strict = jj < ii
    ones_incl = incl.astype(bf16)
    same_blk = (ii >> 4) == (jj >> 4)
    eye = (ii == jj).astype(f32)
    units = [(c, h) for c in range(nc) for h in range(N_HEAD)]
    tile = lambda ref: [ref[0, c * CHUNK:(c + 1) * CHUNK, h * HEAD:(h + 1) * HEAD] for c, h in units]
    r, lw, k, v, kk, b = (tile(ref) for ref in (r_ref, lw_ref, k_ref, v_ref, kk_ref, b_ref))

    gcum = _each(lambda x: _dot_onesl(ones_incl, x), lw)
    gend = _each(lambda x: x[CHUNK - 1:CHUNK, :], gcum)
    kkt = _each(lambda x, gc, l: x * jnp.exp(gc - l), kk, gcum, lw)
    rt = _each(lambda x, gc: x * jnp.exp(gc), r, gcum)
    em = _each(lambda gc: jnp.exp(-gc), gcum)
    kh = _each(jnp.multiply, k, em)
    bh = _each(jnp.multiply, b, em)
    ec = _each(lambda ge, gc: jnp.exp(ge - gc), gend, gcum)
    kg = _each(jnp.multiply, k, ec)
    bg = _each(jnp.multiply, b, ec)

    lhs = _each(lambda x, y: jnp.concatenate([x, y], axis=0), kkt, rt)
    ab = _each(lambda x, y: _dot1(x, y, NT), lhs, bh)
    ak = _each(lambda x, y: _dot1(x, y, NT), lhs, kh)
    a_kb = _each(lambda x: jnp.where(strict, x[:CHUNK], 0.0), ab)
    a_rb = _each(lambda x: jnp.where(incl, x[CHUNK:], 0.0), ab)
    a_kk = _each(lambda x: jnp.where(strict, x[:CHUNK], 0.0), ak)
    a_rk = _each(lambda x: jnp.where(incl, x[CHUNK:], 0.0), ak)

    dg = _each(lambda x: jnp.where(same_blk, x, 0.0), a_kb)
    lo = _each(jnp.subtract, a_kb, dg)
    n1 = _each(jnp.negative, dg)
    n2 = _each(lambda x: _dot1(x, x), n1)
    n4 = _each(lambda x: _dot1(x, x), n2)
    n8 = _each(lambda x: _dot1(x, x), n4)
    t12 = _each(lambda x, y: _dot1(eye + x, eye + y), n1, n2)
    t48 = _each(lambda x, y: _dot1(eye + x, eye + y), n4, n8)
    td = _each(_dot1, t12, t48)
    x1 = _each(lambda x, y: -_dot1(x, y), td, lo)
    x2 = _each(lambda x: _dot1(x, x), x1)
    xx = _each(lambda x, y: _dot1(eye + x, eye + y), x1, x2)
    tinv = _each(_dot1, xx, td)

    akv = _each(_dot1, a_kk, v)
    w1u = _each(lambda t, x, y: _dot1(t, jnp.concatenate([x, y], axis=1)), tinv, kkt, akv)
    z = _each(lambda x, y, w: _dot1(jnp.concatenate([x.T, y], axis=0), w), bg, a_rb, w1u)
    kv = _each(lambda x, y, w: _dot1(jnp.concatenate([x.T, y], axis=0), w), kg, a_rk, v)
    for i, (c, h) in enumerate(units):
        base = jnp.concatenate([eye * jnp.exp(gend[i]), rt[i]], axis=0)
        pw_ref[0, h, c] = base - z[i][:, :CHUNK]
        qy_ref[0, h, c] = kv[i] - z[i][:, CHUNK:]


def wkv_chunks(r, lw, k, v, kk, b, nc):
    nb, t, _ = r.shape
    nchunk = t // CHUNK
    in_spec = pl.BlockSpec((1, nc * CHUNK, D_RWKV), lambda i, j: (i, j, 0))
    out_spec = pl.BlockSpec((1, N_HEAD, nc, 2 * CHUNK, HEAD), lambda i, j: (i, 0, j, 0, 0))
    out = jax.ShapeDtypeStruct((nb, N_HEAD, nchunk, 2 * CHUNK, HEAD), f32)
    return pl.pallas_call(
        functools.partial(_wkv_chunk_kernel, nc=nc),
        grid=(nb, nchunk // nc),
        in_specs=[in_spec] * 6,
        out_specs=[out_spec, out_spec],
        out_shape=[out, out],
        compiler_params=_params(("parallel", "parallel"), VMEM_LIMIT),
        name="wkv_chunks",
    )(r, lw, k, v, kk, b)


def _wkv_serial_kernel(pw_ref, qy_ref, y_ref, s_ref, st_ref, *, nbh, nc):
    @pl.when(pl.program_id(0) == 0)
    def _():
        st_ref[...] = jnp.zeros_like(st_ref)

    def body(c, carry):
        for i in range(nbh):
            z = _dot3(pw_ref[i, c], st_ref[i]) + qy_ref[i, c]
            st_ref[i] = z[:CHUNK]
            y_ref[i, pl.ds(pl.multiple_of(c * CHUNK, CHUNK), CHUNK), :] = z[CHUNK:]
        return carry

    lax.fori_loop(0, nc, body, 0)

    @pl.when(pl.program_id(0) == pl.num_programs(0) - 1)
    def _():
        s_ref[...] = st_ref[...]


def wkv_serial(pw, qy, nc):
    nbh, nchunk = pw.shape[0], pw.shape[1]
    blk = pl.BlockSpec((nbh, nc, 2 * CHUNK, HEAD), lambda j: (0, j, 0, 0))
    return pl.pallas_call(
        functools.partial(_wkv_serial_kernel, nbh=nbh, nc=nc),
        grid=(nchunk // nc,),
        in_specs=[blk, blk],
        out_specs=[pl.BlockSpec((nbh, nc * CHUNK, HEAD), lambda j: (0, j, 0)),
                   pl.BlockSpec((nbh, HEAD, HEAD), lambda j: (0, 0, 0))],
        out_shape=[jax.ShapeDtypeStruct((nbh, nchunk * CHUNK, HEAD), f32),
                   jax.ShapeDtypeStruct((nbh, HEAD, HEAD), f32)],
        scratch_shapes=[pltpu.VMEM((nbh, HEAD, HEAD), f32)],
        compiler_params=_params(("arbitrary",), VMEM_LIMIT),
        name="wkv_serial",
    )(pw, qy)


def _wkv_step_kernel(s_ref, r_ref, lw_ref, k_ref, kk_ref, b_ref, vc_ref, so_ref, y_ref):
    s = s_ref[0]
    skk = jnp.sum(s * kk_ref[0], axis=-1, keepdims=True)
    s1 = s * jnp.exp(lw_ref[0]) - skk * b_ref[0] + vc_ref[0] * k_ref[0]
    so_ref[0] = s1
    y_ref[0] = jnp.sum(s1 * r_ref[0], axis=-1, keepdims=True)


def wkv_step(s, r, lw, k, kk, b, vcol):
    n = s.shape[0]
    row = pl.BlockSpec((1, N_HEAD, 1, HEAD), lambda i: (i, 0, 0, 0))
    col = pl.BlockSpec((1, N_HEAD, HEAD, 1), lambda i: (i, 0, 0, 0))
    mat = pl.BlockSpec((1, N_HEAD, HEAD, HEAD), lambda i: (i, 0, 0, 0))
    return pl.pallas_call(
        _wkv_step_kernel,
        grid=(n,),
        in_specs=[mat, row, row, row, row, row, col],
        out_specs=[mat, col],
        out_shape=[jax.ShapeDtypeStruct(s.shape, f32), jax.ShapeDtypeStruct((n, N_HEAD, HEAD, 1), f32)],
        compiler_params=_params(("parallel",)),
        name="wkv_step",
    )(s, r, lw, k, kk, b, vcol)


def _post_kernel(y_ref, g_ref, bonus_ref, lnw_ref, lnb_ref, o_ref):
    parts = []
    for h in range(N_HEAD):
        y = y_ref[0, h]
        mu = jnp.mean(y, axis=-1, keepdims=True)
        yc = y - mu
        var = jnp.mean(yc * yc, axis=-1, keepdims=True)
        parts.append(yc * lax.rsqrt(var + GN_EPS))
    yn = jnp.concatenate(parts, axis=-1)
    o_ref[0] = (yn * lnw_ref[...] + lnb_ref[...] + bonus_ref[0]) * g_ref[0]


def rwkv_post(y, g, bonus, lnw, lnb, tm):
    nb, _, t, _ = y.shape
    hm = pl.BlockSpec((1, N_HEAD, tm, HEAD), lambda i, j: (i, 0, j, 0))
    tok = pl.BlockSpec((1, tm, D_RWKV), lambda i, j: (i, j, 0))
    par = pl.BlockSpec((1, D_RWKV), lambda i, j: (0, 0))
    return pl.pallas_call(
        _post_kernel,
        grid=(nb, t // tm),
        in_specs=[hm, tok, tok, par, par],
        out_specs=tok,
        out_shape=jax.ShapeDtypeStruct((nb, t, D_RWKV), f32),
        compiler_params=_params(("parallel", "parallel"), VMEM_LIMIT),
        name="rwkv_post",
    )(y, g, bonus, lnw, lnb)


def _logf_kernel(fl_ref, b_ref, lf_ref, c_ref, *, nblk):
    ii = lax.broadcasted_iota(jnp.int32, (128, 128), 0)
    jj = lax.broadcasted_iota(jnp.int32, (128, 128), 1)
    upper = (ii <= jj).astype(bf16)
    carry = jnp.zeros((fl_ref.shape[0], 1), f32)
    for blk in range(nblk):
        sl = slice(128 * blk, 128 * (blk + 1))
        lf = jax.nn.log_sigmoid(fl_ref[:, sl] + b_ref[...])
        lf_ref[:, sl] = lf
        cs = _dot_onesr(lf, upper) + carry
        c_ref[:, sl] = cs
        carry = cs[:, 127:128]


def logf_cumsum(fl_t, bias):
    n, t = fl_t.shape
    out = jax.ShapeDtypeStruct((n, t), f32)
    return pl.pallas_call(
        functools.partial(_logf_kernel, nblk=t // 128),
        out_shape=[out, out],
        name="logf_cumsum",
    )(fl_t, bias)


SUM_ROWS = 16


def _fox_kernel(qi_ref, ki_ref, q_ref, k_ref, v_ref, ck_ref, o_ref, m_ref, acc_ref, *, tq, tk, nsub):
    step_id = pl.program_id(2)
    qi = qi_ref[step_id]
    ki = ki_ref[step_id]

    @pl.when(ki == 0)
    def _():
        m_ref[...] = jnp.full_like(m_ref, NEG)
        acc_ref[...] = jnp.zeros_like(acc_ref)

    def step(masked):
        lane_lo = lax.broadcasted_iota(jnp.int32, (tq, 2 * HEAD), 1) < HEAD
        q = q_ref[0] * (HEAD ** -0.5 * LOG2E)
        qes = [jnp.where(lane_lo == (e == 0), q, 0.0).astype(bf16) for e in range(2)]
        ts = tk // nsub
        ckt = ck_ref[0, 0].T

        def first_col(sub):
            return (sub * ts) // 128 * 128 if masked else 0

        def scores(sub):
            rs = slice(sub * ts, (sub + 1) * ts)
            lo = first_col(sub)
            kb = k_ref[0, rs, :].astype(bf16)
            return [_dot(kb, qes[e][lo:, :], NT) for e in range(2)]

        s_cur = scores(0)
        for sub in range(nsub):
            s_next = scores(sub + 1) if sub + 1 < nsub else None
            rs = slice(sub * ts, (sub + 1) * ts)
            lo = first_col(sub)
            vt = v_ref[0, rs, :].T.astype(bf16)
            ones = jnp.ones((SUM_ROWS, ts), bf16)
            ps, alphas = [], []
            for e in range(2):
                s = s_cur[e] - ckt[rs, e:e + 1]
                if masked:
                    row = lax.broadcasted_iota(jnp.int32, s.shape, 0) + sub * ts
                    col = lax.broadcasted_iota(jnp.int32, s.shape, 1) + lo
                    s = jnp.where(row <= col, s, NEG)
                m_old = m_ref[e, :, lo:]
                m_new = jnp.maximum(m_old, jnp.max(s, axis=0, keepdims=True))
                alphas.append(jnp.exp2(m_old - m_new))
                ps.append(jnp.exp2(s - m_new).astype(bf16))
                m_ref[e, :, lo:] = m_new
            for e in range(2):
                vte = jnp.concatenate([vt[HEAD * e:HEAD * (e + 1), :], ones], axis=0)
                acc_ref[e, :, lo:] = alphas[e] * acc_ref[e, :, lo:] + _dot(vte, ps[e])
            s_cur = s_next

    @pl.when(ki < qi)
    def _():
        step(False)

    @pl.when(ki == qi)
    def _():
        step(True)
        o = [acc_ref[e, 0:HEAD, :] / acc_ref[e, HEAD:HEAD + 1, :] for e in range(2)]
        o_ref[0] = jnp.concatenate(o, axis=0).T


def fox_prompt(u3, ck, tq, nsub):
    nb, t, _ = u3.shape
    nq = t // tq
    cq, ckk, cv = COL_Q // 128, (COL_Q + 512) // 128, (COL_Q + 1024) // 128
    pairs = [(i, j) for i in range(nq) for j in range(i + 1)]
    qi_tab = jnp.asarray([p[0] for p in pairs], jnp.int32)
    ki_tab = jnp.asarray([p[1] for p in pairs], jnp.int32)
    return pl.pallas_call(
        functools.partial(_fox_kernel, tq=tq, tk=tq, nsub=nsub),
        grid_spec=pltpu.PrefetchScalarGridSpec(
            num_scalar_prefetch=2,
            grid=(nb, N_HEAD // 2, len(pairs)),
            in_specs=[pl.BlockSpec((1, tq, 128), lambda b, h, s, qt, kt: (b, qt[s], cq + h)),
                      pl.BlockSpec((1, tq, 128), lambda b, h, s, qt, kt: (b, kt[s], ckk + h)),
                      pl.BlockSpec((1, tq, 128), lambda b, h, s, qt, kt: (b, kt[s], cv + h)),
                      pl.BlockSpec((1, 1, 8, tq), lambda b, h, s, qt, kt: (b, h, 0, kt[s]))],
            out_specs=pl.BlockSpec((1, tq, 128), lambda b, h, s, qt, kt: (b, qt[s], h)),
            scratch_shapes=[pltpu.VMEM((2, 1, tq), f32), pltpu.VMEM((2, HEAD + SUM_ROWS, tq), f32)]),
        out_shape=jax.ShapeDtypeStruct((nb, t, 512), f32),
        compiler_params=_params(("parallel", "parallel", "arbitrary"), VMEM_LIMIT),
        name="fox_prompt",
    )(qi_tab, ki_tab, u3, u3, u3, ck)


def _paged_kernel(pt_ref, q_ref, kn_ref, vn_ref, lfn_ref, *refs, g):
    k_refs, v_refs, lf_refs = refs[0:g], refs[g:2 * g], refs[2 * g:3 * g]
    o_ref = refs[3 * g]
    m_ref, l_ref, acc_ref, car_ref = refs[3 * g + 1:]
    step = pl.program_id(1)
    heads = [slice(HEAD * h, HEAD * (h + 1)) for h in range(N_HEAD)]
    q = q_ref[0]

    @pl.when(step == 0)
    def _():
        lane = lax.broadcasted_iota(jnp.int32, acc_ref.shape, 1)
        acc_ref[...] = jnp.where(lane == 0, vn_ref[0], 0.0)
        qk = q * kn_ref[0]
        for h in range(N_HEAD):
            m_ref[h] = jnp.sum(qk[heads[h]], axis=0, keepdims=True)
            l_ref[h] = jnp.ones((1, 1), f32)
        car_ref[...] = lfn_ref[0]

    ii = lax.broadcasted_iota(jnp.int32, (PAGE, PAGE), 0)
    jj = lax.broadcasted_iota(jnp.int32, (PAGE, PAGE), 1)
    later = (ii > jj).astype(bf16)
    carry = car_ref[...]
    biases = []
    for j in range(g):
        lfp = lf_refs[j][0]
        biases.append(_dot_onesr(lfp, later) + carry)
        carry = carry + jnp.sum(lfp, axis=-1, keepdims=True)
    car_ref[...] = carry

    ss = [jnp.concatenate(
        [jnp.sum(k_refs[j][0, heads[h], :] * q[heads[h]], axis=0, keepdims=True) + biases[j][h:h + 1, :]
         for j in range(g)], axis=0) for h in range(N_HEAD)]
    m_olds = [m_ref[h] for h in range(N_HEAD)]
    m_news = [jnp.maximum(m, jnp.max(jnp.max(s, axis=-1, keepdims=True), axis=0, keepdims=True))
              for m, s in zip(m_olds, ss)]
    alphas = [jnp.exp(mo - mn) for mo, mn in zip(m_olds, m_news)]
    ps = [jnp.exp(s - mn) for s, mn in zip(ss, m_news)]
    for h in range(N_HEAD):
        l_ref[h] = alphas[h] * l_ref[h] + jnp.sum(jnp.sum(ps[h], axis=-1, keepdims=True), axis=0, keepdims=True)
        m_ref[h] = m_news[h]
    for h in range(N_HEAD):
        acc = alphas[h] * acc_ref[heads[h], :]
        for j in range(g):
            acc = acc + v_refs[j][0, heads[h], :] * ps[h][j:j + 1, :]
        acc_ref[heads[h], :] = acc

    @pl.when(step == pl.num_programs(1) - 1)
    def _():
        for h in range(N_HEAD):
            o_ref[0, heads[h], :] = jnp.sum(acc_ref[heads[h], :], axis=-1, keepdims=True) / l_ref[h]


def fox_paged(page_table, q, knew, vnew, lfnew, kpool, vpool, lfpool, g):
    nb, npages = page_table.shape
    last = npages - 1
    cur = lambda b, i, pt: (b, 0, 0)
    pool = lambda j: (lambda b, i, pt: (pt[b, last - (i * g + j)], 0, 0))
    col = pl.BlockSpec((1, 512, 1), cur)
    return pl.pallas_call(
        functools.partial(_paged_kernel, g=g),
        grid_spec=pltpu.PrefetchScalarGridSpec(
            num_scalar_prefetch=1,
            grid=(nb, npages // g),
            in_specs=([col, col, col, pl.BlockSpec((1, N_HEAD, 1), cur)]
                      + [pl.BlockSpec((1, 512, PAGE), pool(j)) for j in range(g)]
                      + [pl.BlockSpec((1, 512, PAGE), pool(j)) for j in range(g)]
                      + [pl.BlockSpec((1, N_HEAD, PAGE), pool(j)) for j in range(g)]),
            out_specs=col,
            scratch_shapes=[pltpu.VMEM((N_HEAD, 1, 1), f32), pltpu.VMEM((N_HEAD, 1, 1), f32),
                            pltpu.VMEM((512, PAGE), f32), pltpu.VMEM((N_HEAD, 1), f32)]),
        out_shape=jax.ShapeDtypeStruct((nb, 512, 1), f32),
        compiler_params=_params(("parallel", "arbitrary"), VMEM_LIMIT),
        name="fox_paged",
    )(page_table, q, knew, vnew, lfnew, *([kpool] * g), *([vpool] * g), *([lfpool] * g))


def _merge_kernel(x_ref, or_ref, of_ref, ug_ref, wr_ref, wf_ref, wo_ref, o_ref):
    br = _dot(or_ref[...].astype(bf16), wr_ref[...])
    bf = _dot(of_ref[...].astype(bf16), wf_ref[...])
    ug = ug_ref[...]
    merged = jax.nn.sigmoid(ug[:, :D_MODEL]) * br + jax.nn.sigmoid(ug[:, D_MODEL:]) * bf
    o_ref[...] = x_ref[...] + _dot(merged.astype(bf16), wo_ref[...])


def merge_out(x, o_r, o_f, u, wr, wf, wo, tm):
    m = x.shape[0]
    row = lambda i: (i, 0)
    cst = lambda i: (0, 0)
    return pl.pallas_call(
        _merge_kernel,
        grid=(m // tm,),
        in_specs=[pl.BlockSpec((tm, D_MODEL), row), pl.BlockSpec((tm, 512), row), pl.BlockSpec((tm, 512), row),
                  pl.BlockSpec((tm, 2 * D_MODEL), lambda i: (i, COL_G // (2 * D_MODEL))),
                  pl.BlockSpec((512, D_MODEL), cst), pl.BlockSpec((512, D_MODEL), cst),
                  pl.BlockSpec((D_MODEL, D_MODEL), cst)],
        out_specs=pl.BlockSpec((tm, D_MODEL), row),
        out_shape=jax.ShapeDtypeStruct((m, D_MODEL), f32),
        compiler_params=_params(("parallel",), VMEM_LIMIT),
        name="merge_out",
    )(x, o_r, o_f, u, wr, wf, wo)


def _ffn_kernel(x_ref, g2_ref, gf_ref, wu_ref, wd_ref, o_ref, h_ref, acc_ref):
    j = pl.program_id(1)

    @pl.when(j == 0)
    def _():
        x = x_ref[...]
        ms = jnp.mean(x * x, axis=-1, keepdims=True)
        h_ref[...] = (x * lax.rsqrt(ms + RMS_EPS) * g2_ref[...]).astype(bf16)
        acc_ref[...] = jnp.zeros_like(acc_ref)

    hid = jnp.maximum(_dot(h_ref[...], wu_ref[...]), 0.0)
    acc_ref[...] += _dot((hid * hid).astype(bf16), wd_ref[...])

    @pl.when(j == pl.num_programs(1) - 1)
    def _():
        x2 = x_ref[...] + acc_ref[...]
        ms = jnp.mean(x2 * x2, axis=-1, keepdims=True)
        o_ref[...] = x2 * lax.rsqrt(ms + RMS_EPS) * gf_ref[...]


def ffn_final(x, g2, gf, wu, wd, tm, tf):
    m = x.shape[0]
    return pl.pallas_call(
        _ffn_kernel,
        grid=(m // tm, D_FF // tf),
        in_specs=[pl.BlockSpec((tm, D_MODEL), lambda i, j: (i, 0)),
                  pl.BlockSpec((1, D_MODEL), lambda i, j: (0, 0)), pl.BlockSpec((1, D_MODEL), lambda i, j: (0, 0)),
                  pl.BlockSpec((D_MODEL, tf), lambda i, j: (0, j)), pl.BlockSpec((tf, D_MODEL), lambda i, j: (j, 0))],
        out_specs=pl.BlockSpec((tm, D_MODEL), lambda i, j: (i, 0)),
        out_shape=jax.ShapeDtypeStruct((m, D_MODEL), f32),
        scratch_shapes=[pltpu.VMEM((tm, D_MODEL), bf16), pltpu.VMEM((tm, D_MODEL), f32)],
        compiler_params=_params(("parallel", "arbitrary"), VMEM_LIMIT),
        name="ffn_final",
    )(x, g2, gf, wu, wd)


def kernel(x_prompt, x_sample, state_shift, state_wkv, cache_k, cache_v, cache_logf, page_table, meta_tokens,
           norm_mix, w_in, mu_shift, w0, w_decay_up, a0, w_a_up, w_g_up, k_k, k_a, r_k, ln_x_w, ln_x_b, b_forget,
           w_br_rwkv, w_br_fox, w_out, norm_ffn, w_ffn_up, w_ffn_down, norm_final):
    assert norm_mix.shape[0] == 1, "single layer"
    nb, seq, _ = x_prompt.shape
    ns = x_sample.shape[0]
    tp = FRONT_PAD + N_META + seq
    tlen = N_META + seq
    tm = 640

    wi = w_in[0]
    wi = jnp.concatenate([wi[:, 0:1792], wi[:, 3328:3336], jnp.zeros((D_MODEL, 248), f32),
                          wi[:, 3336:5384], wi[:, 1792:3328]], axis=1).astype(bf16)
    row2 = lambda a: a.reshape(1, -1)
    lane_head = jnp.arange(512, dtype=jnp.int32) // HEAD
    pr = dict(mu=row2(mu_shift[0]), w0=row2(w0[0]), wd=w_decay_up[0], a0=row2(a0[0]), wa=w_a_up[0], wg=w_g_up[0],
              k_k=row2(k_k[0]), k_a=row2(k_a[0]), r_k=row2(r_k[0]),
              bd=(lane_head[:, None] == lane_head[None, :]).astype(bf16))
    lnw, lnb = row2(ln_x_w[0]), row2(ln_x_b[0])
    g_mix, g_ffn, g_fin = row2(norm_mix[0]), row2(norm_ffn[0]), row2(norm_final)
    wr, wf, wo = w_br_rwkv[0].astype(bf16), w_br_fox[0].astype(bf16), w_out[0].astype(bf16)
    wu, wdn = w_ffn_up[0].astype(bf16), w_ffn_down[0].astype(bf16)

    meta = jnp.broadcast_to(meta_tokens[None], (nb, N_META, D_MODEL))
    xp = jnp.concatenate([jnp.zeros((nb, FRONT_PAD, D_MODEL), f32), meta, x_prompt], axis=1)
    xp2 = xp.reshape(nb * tp, D_MODEL)
    u2 = norm_matmul(xp2, g_mix, wi, 1664, 1408)
    u3 = u2.reshape(nb, tp, D_INP)
    us2 = norm_matmul(x_sample.reshape(ns, D_MODEL), g_mix, wi, ns, 1408)

    r, lw, k2, v, kkn, b, g, bonus = rwkv_prep(u3, None, pr, 320)
    pw, qy = wkv_chunks(r, lw, k2, v, kkn, b, 5)
    per_head = lambda a: a.reshape(nb * N_HEAD, tp // CHUNK, 2 * CHUNK, HEAD)
    y, s_kv = wkv_serial(per_head(pw), per_head(qy), 5)
    o_r = rwkv_post(y.reshape(nb, N_HEAD, tp, HEAD), g, bonus, lnw, lnb, tm)
    new_wkv_p = jnp.swapaxes(s_kv, 1, 2).reshape(1, nb, N_HEAD, HEAD, HEAD)

    us3 = us2.reshape(1, ns, D_INP)
    sp = rwkv_prep(us3, state_shift[0].reshape(1, ns, D_SHIFT), pr, ns)
    rs, lws, ks, vs, kks, bs, gs, bonus_s = sp
    rowify = lambda a: a.reshape(ns, N_HEAD, 1, HEAD)
    s_new, ys = wkv_step(state_wkv[0], rowify(rs), rowify(lws), rowify(ks), rowify(kks), rowify(bs),
                         rowify(vs).reshape(ns, N_HEAD, HEAD, 1))
    ys_hm = jnp.transpose(ys.reshape(ns, N_HEAD, HEAD), (1, 0, 2))[None]
    o_r_s = rwkv_post(ys_hm, gs, bonus_s, lnw, lnb, ns)[0]

    fl_t = jnp.transpose(u3[:, :, COL_F:COL_F + N_HEAD], (0, 2, 1)).reshape(nb * N_HEAD, tp)
    bias_rows = jnp.tile(b_forget[0], nb).reshape(nb * N_HEAD, 1)
    lf_t, c_t = logf_cumsum(fl_t, bias_rows)
    pad_key = jnp.arange(tp, dtype=jnp.int32)[None, :] < FRONT_PAD
    ck = jnp.where(pad_key, -NEG, c_t * LOG2E).reshape(nb, N_HEAD // 2, 2, tp)
    o_f = fox_prompt(u3, jnp.pad(ck, ((0, 0), (0, 0), (0, 6), (0, 0))), 1664, 8)

    qs = us2[:, COL_Q:COL_Q + 512] * (HEAD ** -0.5)
    k_s = us2[:, COL_Q + 512:COL_Q + 1024]
    v_s = us2[:, COL_Q + 1024:COL_Q + 1536]
    lf_s = jax.nn.log_sigmoid(us2[:, COL_F:COL_F + N_HEAD] + b_forget[0][None, :])
    n_pool = cache_k.shape[1]
    kpool = jnp.transpose(cache_k[0], (0, 2, 3, 1)).reshape(n_pool, 512, PAGE)
    vpool = jnp.transpose(cache_v[0], (0, 2, 3, 1)).reshape(n_pool, 512, PAGE)
    o_f_s = fox_paged(page_table, qs.reshape(ns, 512, 1), k_s.reshape(ns, 512, 1), v_s.reshape(ns, 512, 1),
                      lf_s.reshape(ns, N_HEAD, 1), kpool, vpool, jnp.swapaxes(cache_logf[0], 1, 2),
                      16).reshape(ns, 512)

    x1 = merge_out(xp2, o_r.reshape(nb * tp, 512), o_f.reshape(nb * tp, 512), u2, wr, wf, wo, tm)
    yp = ffn_final(x1, g_ffn, g_fin, wu, wdn, 1664, 512).reshape(nb, tp, D_MODEL)
    x1s = merge_out(x_sample.reshape(ns, D_MODEL), o_r_s, o_f_s, us2, wr, wf, wo, ns)
    ysamp = ffn_final(x1s, g_ffn, g_fin, wu, wdn, ns, 512)

    y_prompt = yp[:, ROW0:]
    y_sample = ysamp.reshape(ns, 1, D_MODEL)
    new_shift_p = u3[:, tp - 1, :D_SHIFT][None]
    k_p = u3[:, FRONT_PAD:, COL_Q + 512:COL_Q + 1024].reshape(1, nb, tlen, N_HEAD, HEAD)
    v_p = u3[:, FRONT_PAD:, COL_Q + 1024:COL_Q + 1536].reshape(1, nb, tlen, N_HEAD, HEAD)
    lf_p = jnp.transpose(lf_t.reshape(nb, N_HEAD, tp), (0, 2, 1))[:, FRONT_PAD:][None]
    return (y_prompt, y_sample, new_shift_p, new_wkv_p, k_p, v_p, lf_p,
            us2[:, :D_SHIFT][None], s_new[None], k_s.reshape(1, ns, 1, N_HEAD, HEAD),
            v_s.reshape(1, ns, 1, N_HEAD, HEAD), lf_s.reshape(1, ns, 1, N_HEAD))
```

```python
import functools

import jax
import jax.numpy as jnp
from jax import lax
from jax.experimental import pallas as pl
from jax.experimental.pallas import tpu as pltpu

f32 = jnp.float32
bf16 = jnp.bfloat16

D_MODEL = 1024
N_META = 16
D_RWKV = 512
HEAD = 64
N_HEAD = 8
D_SHIFT = 1792
D_FF = 4096
RMS_EPS = 1e-6
GN_EPS = 64e-5
PAGE = 128

FRONT_PAD = 112
ROW0 = FRONT_PAD + N_META
CHUNK = 64
NEG = -1e30
LOG2E = 1.4426950408889634

COL_R = 0
COL_F = 1792
COL_G = 2048
COL_Q = 4096
D_INP = 5632

VMEM_LIMIT = 56 * 1024 * 1024

NN = (((1,), (0,)), ((), ()))
NT = (((1,), (1,)), ((), ()))
TN = (((0,), (0,)), ((), ()))


def _dot(a, b, dims=NN):
    return lax.dot_general(a, b, dims, preferred_element_type=f32)


def _split2(x):
    hi = x.astype(bf16)
    lo = (x - hi.astype(f32)).astype(bf16)
    return hi, lo


def _split3(x):
    hi = x.astype(bf16)
    r1 = x - hi.astype(f32)
    mid = r1.astype(bf16)
    lo = (r1 - mid.astype(f32)).astype(bf16)
    return hi, mid, lo


def _dot3(a, b, dims=NN):
    ah, al = _split2(a)
    bh, bl = _split2(b)
    return _dot(ah, bh, dims) + (_dot(ah, bl, dims) + _dot(al, bh, dims))


def _dot1(a, b, dims=NN):
    return _dot(a.astype(bf16), b.astype(bf16), dims)


def _dot_onesr(x, ones_bf16, dims=NN):
    hi, mid, lo = _split3(x)
    return _dot(hi, ones_bf16, dims) + (_dot(mid, ones_bf16, dims) + _dot(lo, ones_bf16, dims))


def _dot_onesl(ones_bf16, x, dims=NN):
    hi, mid, lo = _split3(x)
    return _dot(ones_bf16, hi, dims) + (_dot(ones_bf16, mid, dims) + _dot(ones_bf16, lo, dims))


def _params(sem, vmem=None):
    return pltpu.CompilerParams(dimension_semantics=sem, vmem_limit_bytes=vmem)


def _norm_matmul_kernel(x_ref, g_ref, w_ref, o_ref, h_ref):
    @pl.when(pl.program_id(1) == 0)
    def _():
        x = x_ref[...]
        ms = jnp.mean(x * x, axis=-1, keepdims=True)
        h_ref[...] = (x * lax.rsqrt(ms + RMS_EPS) * g_ref[...]).astype(bf16)

    o_ref[...] = _dot(h_ref[...], w_ref[...])


def norm_matmul(x, g, w, tm, tn):
    m, d = x.shape
    n = w.shape[1]
    return pl.pallas_call(
        _norm_matmul_kernel,
        grid=(m // tm, n // tn),
        in_specs=[pl.BlockSpec((tm, d), lambda i, j: (i, 0)),
                  pl.BlockSpec((1, d), lambda i, j: (0, 0)),
                  pl.BlockSpec((d, tn), lambda i, j: (0, j))],
        out_specs=pl.BlockSpec((tm, tn), lambda i, j: (i, j)),
        out_shape=jax.ShapeDtypeStruct((m, n), f32),
        scratch_shapes=[pltpu.VMEM((tm, d), bf16)],
        compiler_params=_params(("parallel", "arbitrary"), VMEM_LIMIT),
        name="norm_matmul",
    )(x, g, w)


def _prep_kernel(u_ref, up_ref, mu_ref, w0_ref, wd_ref, a0_ref, wa_ref, wg_ref, kk_ref, ka_ref, rk_ref,
                 bd_ref, r_o, lw_o, k_o, v_o, kkn_o, b_o, g_o, bonus_o, *, prev_is_tail):
    u = u_ref[0]
    if prev_is_tail:
        first = jnp.where(pl.program_id(1) == 0, 0.0, up_ref[0, 7:8, :])
        rowi = lax.broadcasted_iota(jnp.int32, u.shape, 0)
        up = jnp.where(rowi == 0, first, pltpu.roll(u, 1, 0))
    else:
        up = up_ref[0]
    us = u + mu_ref[...] * (up - u)
    r = us[:, 0:512]
    k = us[:, 512:1024]
    v = us[:, 1024:1536]
    zw = us[:, 1536:1600]
    za = us[:, 1600:1664]
    zg = us[:, 1664:1792]
    bd = bd_ref[...]

    z = -(w0_ref[...] + _dot3(jnp.tanh(zw), wd_ref[...]))
    w_raw = -(jnp.maximum(z, 0.0) + jnp.log1p(jnp.exp(-jnp.abs(z)))) - 0.5
    lw = -jnp.exp(w_raw)
    a = jax.nn.sigmoid(a0_ref[...] + _dot3(za, wa_ref[...]))
    g = _dot3(jax.nn.sigmoid(zg), wg_ref[...])
    kk = k * kk_ref[...]
    ss = _dot_onesr(kk * kk, bd)
    kkn = kk / jnp.maximum(jnp.sqrt(ss), 1e-12)
    k2 = k * (1.0 + (a - 1.0) * ka_ref[...])
    b = kkn * a
    bonus = _dot_onesr(r * k2 * rk_ref[...], bd) * v

    r_o[0] = r
    lw_o[0] = lw
    k_o[0] = k2
    v_o[0] = v
    kkn_o[0] = kkn
    b_o[0] = b
    g_o[0] = g
    bonus_o[0] = bonus


def rwkv_prep(u3, up3, pr, tm):
    nb, t = u3.shape[0], u3.shape[1]
    row = lambda i, j: (i, j, 0)
    if up3 is None:
        up_arr = u3
        up_spec = pl.BlockSpec((1, 8, D_SHIFT), lambda i, j: (i, jnp.maximum(j * (tm // 8) - 1, 0), 0))
    else:
        up_arr = up3
        up_spec = pl.BlockSpec((1, tm, D_SHIFT), row)
    cst = lambda i, j: (0, 0)
    out = jax.ShapeDtypeStruct((nb, t, D_RWKV), f32)
    out_spec = pl.BlockSpec((1, tm, D_RWKV), row)
    vec = lambda n: pl.BlockSpec((1, n), cst)
    return pl.pallas_call(
        functools.partial(_prep_kernel, prev_is_tail=up3 is None),
        grid=(nb, t // tm),
        in_specs=[pl.BlockSpec((1, tm, D_SHIFT), row), up_spec,
                  vec(D_SHIFT), vec(512), pl.BlockSpec((64, 512), cst), vec(512), pl.BlockSpec((64, 512), cst),
                  pl.BlockSpec((128, 512), cst), vec(512), vec(512), vec(512), pl.BlockSpec((512, 512), cst)],
        out_specs=[out_spec] * 8,
        out_shape=[out] * 8,
        compiler_params=_params(("parallel", "parallel"), VMEM_LIMIT),
        name="rwkv_prep",
    )(u3, up_arr, pr["mu"], pr["w0"], pr["wd"], pr["a0"], pr["wa"], pr["wg"], pr["k_k"], pr["k_a"], pr["r_k"],
      pr["bd"])


def _each(f, *lists):
    return [f(*xs) for xs in zip(*lists)]


def _wkv_chunk_kernel(r_ref, lw_ref, k_ref, v_ref, kk_ref, b_ref, pw_ref, qy_ref, *, nc):
    ii = lax.broadcasted_iota(jnp.int32, (CHUNK, CHUNK), 0)
    jj = lax.broadcasted_iota(jnp.int32, (CHUNK, CHUNK), 1)
    incl = jj <= ii
    strict = jj < ii
    ones_incl = incl.astype(bf16)
    same_blk = (ii >> 4) == (jj >> 4)
    eye = (ii == jj).astype(f32)
    units = [(c, h) for c in range(nc) for h in range(N_HEAD)]
    tile = lambda ref: [ref[0, c * CHUNK:(c + 1) * CHUNK, h * HEAD:(h + 1) * HEAD] for c, h in units]
    r, lw, k, v, kk, b = (tile(ref) for ref in (r_ref, lw_ref, k_ref, v_ref, kk_ref, b_ref))

    gcum = _each(lambda x: _dot_onesl(ones_incl, x), lw)
    gend = _each(lambda x: x[CHUNK - 1:CHUNK, :], gcum)
    kkt = _each(lambda x, gc, l: x * jnp.exp(gc - l), kk, gcum, lw)
    rt = _each(lambda x, gc: x * jnp.exp(gc), r, gcum)
    em = _each(lambda gc: jnp.exp(-gc), gcum)
    kh = _each(jnp.multiply, k, em)
    bh = _each(jnp.multiply, b, em)
    ec = _each(lambda ge, gc: jnp.exp(ge - gc), gend, gcum)
    kg = _each(jnp.multiply, k, ec)
    bg = _each(jnp.multiply, b, ec)

    lhs = _each(lambda x, y: jnp.concatenate([x, y], axis=0), kkt, rt)
    ab = _each(lambda x, y: _dot1(x, y, NT), lhs, bh)
    ak = _each(lambda x, y: _dot1(x, y, NT), lhs, kh)
    a_kb = _each(lambda x: jnp.where(strict, x[:CHUNK], 0.0), ab)
    a_rb = _each(lambda x: jnp.where(incl, x[CHUNK:], 0.0), ab)
    a_kk = _each(lambda x: jnp.where(strict, x[:CHUNK], 0.0), ak)
    a_rk = _each(lambda x: jnp.where(incl, x[CHUNK:], 0.0), ak)

    dg = _each(lambda x: jnp.where(same_blk, x, 0.0), a_kb)
    lo = _each(jnp.subtract, a_kb, dg)
    n1 = _each(jnp.negative, dg)
    n2 = _each(lambda x: _dot1(x, x), n1)
    n4 = _each(lambda x: _dot1(x, x), n2)
    n8 = _each(lambda x: _dot1(x, x), n4)
    t12 = _each(lambda x, y: _dot1(eye + x, eye + y), n1, n2)
    t48 = _each(lambda x, y: _dot1(eye + x, eye + y), n4, n8)
    td = _each(_dot1, t12, t48)
    x1 = _each(lambda x, y: -_dot1(x, y), td, lo)
    x2 = _each(lambda x: _dot1(x, x), x1)
    xx = _each(lambda x, y: _dot1(eye + x, eye + y), x1, x2)
    tinv = _each(_dot1, xx, td)

    akv = _each(_dot1, a_kk, v)
    w1u = _each(lambda t, x, y: _dot1(t, jnp.concatenate([x, y], axis=1)), tinv, kkt, akv)
    z = _each(lambda x, y, w: _dot1(jnp.concatenate([x.T, y], axis=0), w), bg, a_rb, w1u)
    kv = _each(lambda x, y, w: _dot1(jnp.concatenate([x.T, y], axis=0), w), kg, a_rk, v)
    for i, (c, h) in enumerate(units):
        base = jnp.concatenate([eye * jnp.exp(gend[i]), rt[i]], axis=0)
        pw_ref[0, h, c] = base - z[i][:, :CHUNK]
        qy_ref[0, h, c] = kv[i] - z[i][:, CHUNK:]


def wkv_chunks(r, lw, k, v, kk, b, nc):
    nb, t, _ = r.shape
    nchunk = t // CHUNK
    in_spec = pl.BlockSpec((1, nc * CHUNK, D_RWKV), lambda i, j: (i, j, 0))
    out_spec = pl.BlockSpec((1, N_HEAD, nc, 2 * CHUNK, HEAD), lambda i, j: (i, 0, j, 0, 0))
    out = jax.ShapeDtypeStruct((nb, N_HEAD, nchunk, 2 * CHUNK, HEAD), f32)
    return pl.pallas_call(
        functools.partial(_wkv_chunk_kernel, nc=nc),
        grid=(nb, nchunk // nc),
        in_specs=[in_spec] * 6,
        out_specs=[out_spec, out_spec],
        out_shape=[out, out],
        compiler_params=_params(("parallel", "parallel"), VMEM_LIMIT),
        name="wkv_chunks",
    )(r, lw, k, v, kk, b)


def _group_norm(y):
    mu = jnp.mean(y, axis=-1, keepdims=True)
    yc = y - mu
    var = jnp.mean(yc * yc, axis=-1, keepdims=True)
    return yc * lax.rsqrt(var + GN_EPS)


def _wkv_serial_kernel(pw_ref, qy_ref, g_ref, bonus_ref, lnw_ref, lnb_ref, o_ref, s_ref, st_ref, *, nb, nc):
    @pl.when(pl.program_id(0) == 0)
    def _():
        st_ref[...] = jnp.zeros_like(st_ref)

    nbh = nb * N_HEAD
    st = [st_ref[i] for i in range(nbh)]
    ys = []
    for c in range(nc):
        zs = [_dot1(pw_ref[i, c], st[i]) + qy_ref[i, c] for i in range(nbh)]
        st = [z[:CHUNK] for z in zs]
        ys += [z[CHUNK:] for z in zs]
    for i in range(nbh):
        st_ref[i] = st[i]

    mus = [jnp.mean(y, axis=-1, keepdims=True) for y in ys]
    ycs = _each(jnp.subtract, ys, mus)
    vrs = [jnp.mean(yc * yc, axis=-1, keepdims=True) for yc in ycs]
    yns = _each(lambda yc, vr: yc * lax.rsqrt(vr + GN_EPS), ycs, vrs)
    for c in range(nc):
        rows = slice(c * CHUNK, (c + 1) * CHUNK)
        for bi in range(nb):
            yn = jnp.concatenate(yns[c * nbh + bi * N_HEAD:c * nbh + (bi + 1) * N_HEAD], axis=-1)
            o_ref[bi, rows, :] = (yn * lnw_ref[...] + lnb_ref[...] + bonus_ref[bi, rows, :]) * g_ref[bi, rows, :]

    @pl.when(pl.program_id(0) == pl.num_programs(0) - 1)
    def _():
        s_ref[...] = st_ref[...]


def wkv_serial(pw, qy, g, bonus, lnw, lnb, nc):
    nbh, nchunk = pw.shape[0], pw.shape[1]
    nb = g.shape[0]
    blk = pl.BlockSpec((nbh, nc, 2 * CHUNK, HEAD), lambda j: (0, j, 0, 0))
    tok = pl.BlockSpec((nb, nc * CHUNK, D_RWKV), lambda j: (0, j, 0))
    par = pl.BlockSpec((1, D_RWKV), lambda j: (0, 0))
    return pl.pallas_call(
        functools.partial(_wkv_serial_kernel, nb=nb, nc=nc),
        grid=(nchunk // nc,),
        in_specs=[blk, blk, tok, tok, par, par],
        out_specs=[tok, pl.BlockSpec((nbh, HEAD, HEAD), lambda j: (0, 0, 0))],
        out_shape=[jax.ShapeDtypeStruct((nb, nchunk * CHUNK, D_RWKV), f32),
                   jax.ShapeDtypeStruct((nbh, HEAD, HEAD), f32)],
        scratch_shapes=[pltpu.VMEM((nbh, HEAD, HEAD), f32)],
        compiler_params=_params(("arbitrary",), VMEM_LIMIT),
        name="wkv_serial",
    )(pw, qy, g, bonus, lnw, lnb)


def _wkv_step_kernel(s_ref, r_ref, lw_ref, k_ref, kk_ref, b_ref, vc_ref, so_ref, y_ref):
    s = s_ref[0]
    skk = jnp.sum(s * kk_ref[0], axis=-1, keepdims=True)
    s1 = s * jnp.exp(lw_ref[0]) - skk * b_ref[0] + vc_ref[0] * k_ref[0]
    so_ref[0] = s1
    y_ref[0] = jnp.sum(s1 * r_ref[0], axis=-1, keepdims=True)


def wkv_step(s, r, lw, k, kk, b, vcol):
    n = s.shape[0]
    row = pl.BlockSpec((1, N_HEAD, 1, HEAD), lambda i: (i, 0, 0, 0))
    col = pl.BlockSpec((1, N_HEAD, HEAD, 1), lambda i: (i, 0, 0, 0))
    mat = pl.BlockSpec((1, N_HEAD, HEAD, HEAD), lambda i: (i, 0, 0, 0))
    return pl.pallas_call(
        _wkv_step_kernel,
        grid=(n,),
        in_specs=[mat, row, row, row, row, row, col],
        out_specs=[mat, col],
        out_shape=[jax.ShapeDtypeStruct(s.shape, f32), jax.ShapeDtypeStruct((n, N_HEAD, HEAD, 1), f32)],
        compiler_params=_params(("parallel",)),
        name="wkv_step",
    )(s, r, lw, k, kk, b, vcol)


def _post_kernel(y_ref, g_ref, bonus_ref, lnw_ref, lnb_ref, o_ref):
    yn = jnp.concatenate([_group_norm(y_ref[0, h]) for h in range(N_HEAD)], axis=-1)
    o_ref[0] = (yn * lnw_ref[...] + lnb_ref[...] + bonus_ref[0]) * g_ref[0]


def rwkv_post(y, g, bonus, lnw, lnb, tm):
    nb, _, t, _ = y.shape
    hm = pl.BlockSpec((1, N_HEAD, tm, HEAD), lambda i, j: (i, 0, j, 0))
    tok = pl.BlockSpec((1, tm, D_RWKV), lambda i, j: (i, j, 0))
    par = pl.BlockSpec((1, D_RWKV), lambda i, j: (0, 0))
    return pl.pallas_call(
        _post_kernel,
        grid=(nb, t // tm),
        in_specs=[hm, tok, tok, par, par],
        out_specs=tok,
        out_shape=jax.ShapeDtypeStruct((nb, t, D_RWKV), f32),
        compiler_params=_params(("parallel", "parallel"), VMEM_LIMIT),
        name="rwkv_post",
    )(y, g, bonus, lnw, lnb)


def _logf_kernel(fl_ref, b_ref, lf_ref, c_ref, *, nblk):
    ii = lax.broadcasted_iota(jnp.int32, (128, 128), 0)
    jj = lax.broadcasted_iota(jnp.int32, (128, 128), 1)
    upper = (ii <= jj).astype(bf16)
    carry = jnp.zeros((fl_ref.shape[0], 1), f32)
    for blk in range(nblk):
        sl = slice(128 * blk, 128 * (blk + 1))
        lf = jax.nn.log_sigmoid(fl_ref[:, sl] + b_ref[...])
        lf_ref[:, sl] = lf
        cs = _dot_onesr(lf, upper) + carry
        c_ref[:, sl] = cs
        carry = cs[:, 127:128]


def logf_cumsum(fl_t, bias):
    n, t = fl_t.shape
    out = jax.ShapeDtypeStruct((n, t), f32)
    return pl.pallas_call(
        functools.partial(_logf_kernel, nblk=t // 128),
        out_shape=[out, out],
        name="logf_cumsum",
    )(fl_t, bias)


SUM_ROWS = 16


def _fox_kernel(qi_ref, ki_ref, q_ref, k_ref, v_ref, ck_ref, o_ref, m_ref, acc_ref, *, tq, tk, nsub):
    step_id = pl.program_id(2)
    qi = qi_ref[step_id]
    ki = ki_ref[step_id]

    @pl.when(ki == 0)
    def _():
        m_ref[...] = jnp.full_like(m_ref, NEG)
        acc_ref[...] = jnp.zeros_like(acc_ref)

    def step(masked):
        lane_lo = lax.broadcasted_iota(jnp.int32, (tq, 2 * HEAD), 1) < HEAD
        q = q_ref[0] * (HEAD ** -0.5 * LOG2E)
        qes = [jnp.where(lane_lo == (e == 0), q, 0.0).astype(bf16) for e in range(2)]
        ts = tk // nsub
        ckt = ck_ref[0, 0].T

        def first_col(sub):
            return (sub * ts) // 128 * 128 if masked else 0

        def scores(sub):
            rs = slice(sub * ts, (sub + 1) * ts)
            lo = first_col(sub)
            kb = k_ref[0, rs, :].astype(bf16)
            return [_dot(kb, qes[e][lo:, :], NT) for e in range(2)]

        s_cur = scores(0)
        for sub in range(nsub):
            s_next = scores(sub + 1) if sub + 1 < nsub else None
            rs = slice(sub * ts, (sub + 1) * ts)
            lo = first_col(sub)
            vt = v_ref[0, rs, :].T.astype(bf16)
            ones = jnp.ones((SUM_ROWS, ts), bf16)
            ps, alphas = [], []
            for e in range(2):
                s = s_cur[e] - ckt[rs, e:e + 1]
                if masked:
                    row = lax.broadcasted_iota(jnp.int32, s.shape, 0) + sub * ts
                    col = lax.broadcasted_iota(jnp.int32, s.shape, 1) + lo
                    s = jnp.where(row <= col, s, NEG)
                m_old = m_ref[e, :, lo:]
                m_new = jnp.maximum(m_old, jnp.max(s, axis=0, keepdims=True))
                alphas.append(jnp.exp2(m_old - m_new))
                ps.append(jnp.exp2(s - m_new).astype(bf16))
                m_ref[e, :, lo:] = m_new
            for e in range(2):
                vte = jnp.concatenate([vt[HEAD * e:HEAD * (e + 1), :], ones], axis=0)
                acc_ref[e, :, lo:] = alphas[e] * acc_ref[e, :, lo:] + _dot(vte, ps[e])
            s_cur = s_next

    @pl.when(ki < qi)
    def _():
        step(False)

    @pl.when(ki == qi)
    def _():
        step(True)
        o = [acc_ref[e, 0:HEAD, :] / acc_ref[e, HEAD:HEAD + 1, :] for e in range(2)]
        o_ref[0] = jnp.concatenate(o, axis=0).T


def fox_prompt(u3, ck, tq, nsub):
    nb, t, _ = u3.shape
    nq = t // tq
    cq, ckk, cv = COL_Q // 128, (COL_Q + 512) // 128, (COL_Q + 1024) // 128
    pairs = [(i, j) for i in range(nq) for j in range(i + 1)]
    qi_tab = jnp.asarray([p[0] for p in pairs], jnp.int32)
    ki_tab = jnp.asarray([p[1] for p in pairs], jnp.int32)
    return pl.pallas_call(
        functools.partial(_fox_kernel, tq=tq, tk=tq, nsub=nsub),
        grid_spec=pltpu.PrefetchScalarGridSpec(
            num_scalar_prefetch=2,
            grid=(nb, N_HEAD // 2, len(pairs)),
            in_specs=[pl.BlockSpec((1, tq, 128), lambda b, h, s, qt, kt: (b, qt[s], cq + h)),
                      pl.BlockSpec((1, tq, 128), lambda b, h, s, qt, kt: (b, kt[s], ckk + h)),
                      pl.BlockSpec((1, tq, 128), lambda b, h, s, qt, kt: (b, kt[s], cv + h)),
                      pl.BlockSpec((1, 1, 8, tq), lambda b, h, s, qt, kt: (b, h, 0, kt[s]))],
            out_specs=pl.BlockSpec((1, tq, 128), lambda b, h, s, qt, kt: (b, qt[s], h)),
            scratch_shapes=[pltpu.VMEM((2, 1, tq), f32), pltpu.VMEM((2, HEAD + SUM_ROWS, tq), f32)]),
        out_shape=jax.ShapeDtypeStruct((nb, t, 512), f32),
        compiler_params=_params(("parallel", "parallel", "arbitrary"), VMEM_LIMIT),
        name="fox_prompt",
    )(qi_tab, ki_tab, u3, u3, u3, ck)


def _paged_kernel(pt_ref, q_ref, kn_ref, vn_ref, lfn_ref, *refs, g):
    k_refs, v_refs, lf_refs = refs[0:g], refs[g:2 * g], refs[2 * g:3 * g]
    o_ref = refs[3 * g]
    m_ref, l_ref, acc_ref, car_ref = refs[3 * g + 1:]
    step = pl.program_id(1)
    heads = [slice(HEAD * h, HEAD * (h + 1)) for h in range(N_HEAD)]
    q = q_ref[0]

    @pl.when(step == 0)
    def _():
        lane = lax.broadcasted_iota(jnp.int32, acc_ref.shape, 1)
        acc_ref[...] = jnp.where(lane == 0, vn_ref[0], 0.0)
        qk = q * kn_ref[0]
        for h in range(N_HEAD):
            m_ref[h] = jnp.sum(qk[heads[h]], axis=0, keepdims=True)
            l_ref[h] = jnp.ones((1, 1), f32)
        car_ref[...] = lfn_ref[0]

    ii = lax.broadcasted_iota(jnp.int32, (PAGE, PAGE), 0)
    jj = lax.broadcasted_iota(jnp.int32, (PAGE, PAGE), 1)
    later = (ii > jj).astype(bf16)
    carry = car_ref[...]
    biases = []
    for j in range(g):
        lfp = lf_refs[j][0]
        biases.append(_dot_onesr(lfp, later) + carry)
        carry = carry + jnp.sum(lfp, axis=-1, keepdims=True)
    car_ref[...] = carry

    ss = [jnp.concatenate(
        [jnp.sum(k_refs[j][0, heads[h], :] * q[heads[h]], axis=0, keepdims=True) + biases[j][h:h + 1, :]
         for j in range(g)], axis=0) for h in range(N_HEAD)]
    m_olds = [m_ref[h] for h in range(N_HEAD)]
    m_news = [jnp.maximum(m, jnp.max(jnp.max(s, axis=-1, keepdims=True), axis=0, keepdims=True))
              for m, s in zip(m_olds, ss)]
    alphas = [jnp.exp(mo - mn) for mo, mn in zip(m_olds, m_news)]
    ps = [jnp.exp(s - mn) for s, mn in zip(ss, m_news)]
    for h in range(N_HEAD):
        l_ref[h] = alphas[h] * l_ref[h] + jnp.sum(jnp.sum(ps[h], axis=-1, keepdims=True), axis=0, keepdims=True)
        m_ref[h] = m_news[h]
    for h in range(N_HEAD):
        acc = alphas[h] * acc_ref[heads[h], :]
        for j in range(g):
            acc = acc + v_refs[j][0, heads[h], :] * ps[h][j:j + 1, :]
        acc_ref[heads[h], :] = acc

    @pl.when(step == pl.num_programs(1) - 1)
    def _():
        for h in range(N_HEAD):
            o_ref[0, heads[h], :] = jnp.sum(acc_ref[heads[h], :], axis=-1, keepdims=True) / l_ref[h]


def fox_paged(page_table, q, knew, vnew, lfnew, kpool, vpool, lfpool, g):
    nb, npages = page_table.shape
    last = npages - 1
    cur = lambda b, i, pt: (b, 0, 0)
    pool = lambda j: (lambda b, i, pt: (pt[b, last - (i * g + j)], 0, 0))
    col = pl.BlockSpec((1, 512, 1), cur)
    return pl.pallas_call(
        functools.partial(_paged_kernel, g=g),
        grid_spec=pltpu.PrefetchScalarGridSpec(
            num_scalar_prefetch=1,
            grid=(nb, npages // g),
            in_specs=([col, col, col, pl.BlockSpec((1, N_HEAD, 1), cur)]
                      + [pl.BlockSpec((1, 512, PAGE), pool(j)) for j in range(g)]
                      + [pl.BlockSpec((1, 512, PAGE), pool(j)) for j in range(g)]
                      + [pl.BlockSpec((1, N_HEAD, PAGE), pool(j)) for j in range(g)]),
            out_specs=col,
            scratch_shapes=[pltpu.VMEM((N_HEAD, 1, 1), f32), pltpu.VMEM((N_HEAD, 1, 1), f32),
                            pltpu.VMEM((512, PAGE), f32), pltpu.VMEM((N_HEAD, 1), f32)]),
        out_shape=jax.ShapeDtypeStruct((nb, 512, 1), f32),
        compiler_params=_params(("parallel", "arbitrary"), VMEM_LIMIT),
        name="fox_paged",
    )(page_table, q, knew, vnew, lfnew, *([kpool] * g), *([vpool] * g), *([lfpool] * g))


def _merge_kernel(x_ref, or_ref, of_ref, ug_ref, wr_ref, wf_ref, wo_ref, o_ref):
    br = _dot(or_ref[...].astype(bf16), wr_ref[...])
    bf = _dot(of_ref[...].astype(bf16), wf_ref[...])
    ug = ug_ref[...]
    merged = jax.nn.sigmoid(ug[:, :D_MODEL]) * br + jax.nn.sigmoid(ug[:, D_MODEL:]) * bf
    o_ref[...] = x_ref[...] + _dot(merged.astype(bf16), wo_ref[...])


def merge_out(x, o_r, o_f, u, wr, wf, wo, tm):
    m = x.shape[0]
    row = lambda i: (i, 0)
    cst = lambda i: (0, 0)
    return pl.pallas_call(
        _merge_kernel,
        grid=(m // tm,),
        in_specs=[pl.BlockSpec((tm, D_MODEL), row), pl.BlockSpec((tm, 512), row), pl.BlockSpec((tm, 512), row),
                  pl.BlockSpec((tm, 2 * D_MODEL), lambda i: (i, COL_G // (2 * D_MODEL))),
                  pl.BlockSpec((512, D_MODEL), cst), pl.BlockSpec((512, D_MODEL), cst),
                  pl.BlockSpec((D_MODEL, D_MODEL), cst)],
        out_specs=pl.BlockSpec((tm, D_MODEL), row),
        out_shape=jax.ShapeDtypeStruct((m, D_MODEL), f32),
        compiler_params=_params(("parallel",), VMEM_LIMIT),
        name="merge_out",
    )(x, o_r, o_f, u, wr, wf, wo)


def _ffn_kernel(x_ref, g2_ref, gf_ref, wu_ref, wd_ref, o_ref, h_ref, acc_ref):
    j = pl.program_id(1)

    @pl.when(j == 0)
    def _():
        x = x_ref[...]
        ms = jnp.mean(x * x, axis=-1, keepdims=True)
        h_ref[...] = (x * lax.rsqrt(ms + RMS_EPS) * g2_ref[...]).astype(bf16)
        acc_ref[...] = jnp.zeros_like(acc_ref)

    hid = jnp.maximum(_dot(h_ref[...], wu_ref[...]), 0.0)
    acc_ref[...] += _dot((hid * hid).astype(bf16), wd_ref[...])

    @pl.when(j == pl.num_programs(1) - 1)
    def _():
        x2 = x_ref[...] + acc_ref[...]
        ms = jnp.mean(x2 * x2, axis=-1, keepdims=True)
        o_ref[...] = x2 * lax.rsqrt(ms + RMS_EPS) * gf_ref[...]


def ffn_final(x, g2, gf, wu, wd, tm, tf):
    m = x.shape[0]
    return pl.pallas_call(
        _ffn_kernel,
        grid=(m // tm, D_FF // tf),
        in_specs=[pl.BlockSpec((tm, D_MODEL), lambda i, j: (i, 0)),
                  pl.BlockSpec((1, D_MODEL), lambda i, j: (0, 0)), pl.BlockSpec((1, D_MODEL), lambda i, j: (0, 0)),
                  pl.BlockSpec((D_MODEL, tf), lambda i, j: (0, j)), pl.BlockSpec((tf, D_MODEL), lambda i, j: (j, 0))],
        out_specs=pl.BlockSpec((tm, D_MODEL), lambda i, j: (i, 0)),
        out_shape=jax.ShapeDtypeStruct((m, D_MODEL), f32),
        scratch_shapes=[pltpu.VMEM((tm, D_MODEL), bf16), pltpu.VMEM((tm, D_MODEL), f32)],
        compiler_params=_params(("parallel", "arbitrary"), VMEM_LIMIT),
        name="ffn_final",
    )(x, g2, gf, wu, wd)


def kernel(x_prompt, x_sample, state_shift, state_wkv, cache_k, cache_v, cache_logf, page_table, meta_tokens,
           norm_mix, w_in, mu_shift, w0, w_decay_up, a0, w_a_up, w_g_up, k_k, k_a, r_k, ln_x_w, ln_x_b, b_forget,
           w_br_rwkv, w_br_fox, w_out, norm_ffn, w_ffn_up, w_ffn_down, norm_final):
    assert norm_mix.shape[0] == 1, "single layer"
    nb, seq, _ = x_prompt.shape
    ns = x_sample.shape[0]
    tp = FRONT_PAD + N_META + seq
    tlen = N_META + seq
    tm = 640

    wi = w_in[0]
    wi = jnp.concatenate([wi[:, 0:1792], wi[:, 3328:3336], jnp.zeros((D_MODEL, 248), f32),
                          wi[:, 3336:5384], wi[:, 1792:3328]], axis=1).astype(bf16)
    row2 = lambda a: a.reshape(1, -1)
    lane_head = jnp.arange(512, dtype=jnp.int32) // HEAD
    pr = dict(mu=row2(mu_shift[0]), w0=row2(w0[0]), wd=w_decay_up[0], a0=row2(a0[0]), wa=w_a_up[0], wg=w_g_up[0],
              k_k=row2(k_k[0]), k_a=row2(k_a[0]), r_k=row2(r_k[0]),
              bd=(lane_head[:, None] == lane_head[None, :]).astype(bf16))
    lnw, lnb = row2(ln_x_w[0]), row2(ln_x_b[0])
    g_mix, g_ffn, g_fin = row2(norm_mix[0]), row2(norm_ffn[0]), row2(norm_final)
    wr, wf, wo = w_br_rwkv[0].astype(bf16), w_br_fox[0].astype(bf16), w_out[0].astype(bf16)
    wu, wdn = w_ffn_up[0].astype(bf16), w_ffn_down[0].astype(bf16)

    meta = jnp.broadcast_to(meta_tokens[None], (nb, N_META, D_MODEL))
    xp = jnp.concatenate([jnp.zeros((nb, FRONT_PAD, D_MODEL), f32), meta, x_prompt], axis=1)
    xp2 = xp.reshape(nb * tp, D_MODEL)
    u2 = norm_matmul(xp2, g_mix, wi, 1664, 1408)
    u3 = u2.reshape(nb, tp, D_INP)
    us2 = norm_matmul(x_sample.reshape(ns, D_MODEL), g_mix, wi, ns, 1408)

    r, lw, k2, v, kkn, b, g, bonus = rwkv_prep(u3, None, pr, 320)
    pw, qy = wkv_chunks(r, lw, k2, v, kkn, b, 5)
    per_head = lambda a: a.reshape(nb * N_HEAD, tp // CHUNK, 2 * CHUNK, HEAD)
    o_r, s_kv = wkv_serial(per_head(pw), per_head(qy), g, bonus, lnw, lnb, 5)
    new_wkv_p = jnp.swapaxes(s_kv, 1, 2).reshape(1, nb, N_HEAD, HEAD, HEAD)

    us3 = us2.reshape(1, ns, D_INP)
    sp = rwkv_prep(us3, state_shift[0].reshape(1, ns, D_SHIFT), pr, ns)
    rs, lws, ks, vs, kks, bs, gs, bonus_s = sp
    rowify = lambda a: a.reshape(ns, N_HEAD, 1, HEAD)
    s_new, ys = wkv_step(state_wkv[0], rowify(rs), rowify(lws), rowify(ks), rowify(kks), rowify(bs),
                         rowify(vs).reshape(ns, N_HEAD, HEAD, 1))
    ys_hm = jnp.transpose(ys.reshape(ns, N_HEAD, HEAD), (1, 0, 2))[None]
    o_r_s = rwkv_post(ys_hm, gs, bonus_s, lnw, lnb, ns)[0]

    fl_t = jnp.transpose(u3[:, :, COL_F:COL_F + N_HEAD], (0, 2, 1)).reshape(nb * N_HEAD, tp)
    bias_rows = jnp.tile(b_forget[0], nb).reshape(nb * N_HEAD, 1)
    lf_t, c_t = logf_cumsum(fl_t, bias_rows)
    pad_key = jnp.arange(tp, dtype=jnp.int32)[None, :] < FRONT_PAD
    ck = jnp.where(pad_key, -NEG, c_t * LOG2E).reshape(nb, N_HEAD // 2, 2, tp)
    o_f = fox_prompt(u3, jnp.pad(ck, ((0, 0), (0, 0), (0, 6), (0, 0))), 1664, 8)

    qs = us2[:, COL_Q:COL_Q + 512] * (HEAD ** -0.5)
    k_s = us2[:, COL_Q + 512:COL_Q + 1024]
    v_s = us2[:, COL_Q + 1024:COL_Q + 1536]
    lf_s = jax.nn.log_sigmoid(us2[:, COL_F:COL_F + N_HEAD] + b_forget[0][None, :])
    n_pool = cache_k.shape[1]
    kpool = jnp.transpose(cache_k[0], (0, 2, 3, 1)).reshape(n_pool, 512, PAGE)
    vpool = jnp.transpose(cache_v[0], (0, 2, 3, 1)).reshape(n_pool, 512, PAGE)
    o_f_s = fox_paged(page_table, qs.reshape(ns, 512, 1), k_s.reshape(ns, 512, 1), v_s.reshape(ns, 512, 1),
                      lf_s.reshape(ns, N_HEAD, 1), kpool, vpool, jnp.swapaxes(cache_logf[0], 1, 2),
                      32).reshape(ns, 512)

    x1 = merge_out(xp2, o_r.reshape(nb * tp, 512), o_f.reshape(nb * tp, 512), u2, wr, wf, wo, tm)
    yp = ffn_final(x1, g_ffn, g_fin, wu, wdn, 1664, 512).reshape(nb, tp, D_MODEL)
    x1s = merge_out(x_sample.reshape(ns, D_MODEL), o_r_s, o_f_s, us2, wr, wf, wo, ns)
    ysamp = ffn_final(x1s, g_ffn, g_fin, wu, wdn, ns, 512)

    y_prompt = yp[:, ROW0:]
    y_sample = ysamp.reshape(ns, 1, D_MODEL)
    new_shift_p = u3[:, tp - 1, :D_SHIFT][None]
    k_p = u3[:, FRONT_PAD:, COL_Q + 512:COL_Q + 1024].reshape(1, nb, tlen, N_HEAD, HEAD)
    v_p = u3[:, FRONT_PAD:, COL_Q + 1024:COL_Q + 1536].reshape(1, nb, tlen, N_HEAD, HEAD)
    lf_p = jnp.transpose(lf_t.reshape(nb, N_HEAD, tp), (0, 2, 1))[:, FRONT_PAD:][None]
    return (y_prompt, y_sample, new_shift_p, new_wkv_p, k_p, v_p, lf_p,
            us2[:, :D_SHIFT][None], s_new[None], k_s.reshape(1, ns, 1, N_HEAD, HEAD),
            v_s.reshape(1, ns, 1, N_HEAD, HEAD), lf_s.reshape(1, ns, 1, N_HEAD))
```

```python
import functools

import jax
import jax.numpy as jnp
from jax import lax
from jax.experimental import pallas as pl
from jax.experimental.pallas import tpu as pltpu

f32 = jnp.float32
bf16 = jnp.bfloat16

D_MODEL = 1024
N_META = 16
D_RWKV = 512
HEAD = 64
N_HEAD = 8
D_SHIFT = 1792
D_FF = 4096
RMS_EPS = 1e-6
GN_EPS = 64e-5
PAGE = 128

FRONT_PAD = 112
ROW0 = FRONT_PAD + N_META
CHUNK = 64
NEG = -1e30
LOG2E = 1.4426950408889634

COL_R = 0
COL_F = 1792
COL_G = 2048
COL_Q = 4096
D_INP = 5632

VMEM_LIMIT = 56 * 1024 * 1024

NN = (((1,), (0,)), ((), ()))
NT = (((1,), (1,)), ((), ()))
TN = (((0,), (0,)), ((), ()))


def _dot(a, b, dims=NN):
    return lax.dot_general(a, b, dims, preferred_element_type=f32)


def _split2(x):
    hi = x.astype(bf16)
    lo = (x - hi.astype(f32)).astype(bf16)
    return hi, lo


def _split3(x):
    hi = x.astype(bf16)
    r1 = x - hi.astype(f32)
    mid = r1.astype(bf16)
    lo = (r1 - mid.astype(f32)).astype(bf16)
    return hi, mid, lo


def _dot3(a, b, dims=NN):
    ah, al = _split2(a)
    bh, bl = _split2(b)
    return _dot(ah, bh, dims) + (_dot(ah, bl, dims) + _dot(al, bh, dims))


def _dot1(a, b, dims=NN):
    return _dot(a.astype(bf16), b.astype(bf16), dims)


def _dot_onesr(x, ones_bf16, dims=NN):
    hi, mid, lo = _split3(x)
    return _dot(hi, ones_bf16, dims) + (_dot(mid, ones_bf16, dims) + _dot(lo, ones_bf16, dims))


def _dot_onesl(ones_bf16, x, dims=NN):
    hi, mid, lo = _split3(x)
    return _dot(ones_bf16, hi, dims) + (_dot(ones_bf16, mid, dims) + _dot(ones_bf16, lo, dims))


def _params(sem, vmem=None):
    return pltpu.CompilerParams(dimension_semantics=sem, vmem_limit_bytes=vmem)


def _norm_matmul_kernel(x_ref, g_ref, w_ref, o_ref, h_ref):
    @pl.when(pl.program_id(1) == 0)
    def _():
        x = x_ref[...]
        ms = jnp.mean(x * x, axis=-1, keepdims=True)
        h_ref[...] = (x * lax.rsqrt(ms + RMS_EPS) * g_ref[...]).astype(bf16)

    o_ref[...] = _dot(h_ref[...], w_ref[...])


def norm_matmul(x, g, w, tm, tn):
    m, d = x.shape
    n = w.shape[1]
    return pl.pallas_call(
        _norm_matmul_kernel,
        grid=(m // tm, n // tn),
        in_specs=[pl.BlockSpec((tm, d), lambda i, j: (i, 0)),
                  pl.BlockSpec((1, d), lambda i, j: (0, 0)),
                  pl.BlockSpec((d, tn), lambda i, j: (0, j))],
        out_specs=pl.BlockSpec((tm, tn), lambda i, j: (i, j)),
        out_shape=jax.ShapeDtypeStruct((m, n), f32),
        scratch_shapes=[pltpu.VMEM((tm, d), bf16)],
        compiler_params=_params(("parallel", "arbitrary"), VMEM_LIMIT),
        name="norm_matmul",
    )(x, g, w)


def _prep_kernel(u_ref, up_ref, mu_ref, w0_ref, wd_ref, a0_ref, wa_ref, wg_ref, kk_ref, ka_ref, rk_ref,
                 bd_ref, r_o, lw_o, k_o, v_o, kkn_o, b_o, g_o, bonus_o, *, prev_is_tail):
    u = u_ref[0]
    if prev_is_tail:
        first = jnp.where(pl.program_id(1) == 0, 0.0, up_ref[0, 7:8, :])
        rowi = lax.broadcasted_iota(jnp.int32, u.shape, 0)
        up = jnp.where(rowi == 0, first, pltpu.roll(u, 1, 0))
    else:
        up = up_ref[0]
    us = u + mu_ref[...] * (up - u)
    r = us[:, 0:512]
    k = us[:, 512:1024]
    v = us[:, 1024:1536]
    zw = us[:, 1536:1600]
    za = us[:, 1600:1664]
    zg = us[:, 1664:1792]
    bd = bd_ref[...]

    z = -(w0_ref[...] + _dot3(jnp.tanh(zw), wd_ref[...]))
    w_raw = -(jnp.maximum(z, 0.0) + jnp.log1p(jnp.exp(-jnp.abs(z)))) - 0.5
    lw = -jnp.exp(w_raw)
    a = jax.nn.sigmoid(a0_ref[...] + _dot3(za, wa_ref[...]))
    g = _dot3(jax.nn.sigmoid(zg), wg_ref[...])
    kk = k * kk_ref[...]
    ss = _dot_onesr(kk * kk, bd)
    kkn = kk / jnp.maximum(jnp.sqrt(ss), 1e-12)
    k2 = k * (1.0 + (a - 1.0) * ka_ref[...])
    b = kkn * a
    bonus = _dot_onesr(r * k2 * rk_ref[...], bd) * v

    r_o[0] = r
    lw_o[0] = lw
    k_o[0] = k2
    v_o[0] = v
    kkn_o[0] = kkn
    b_o[0] = b
    g_o[0] = g
    bonus_o[0] = bonus


def rwkv_prep(u3, up3, pr, tm):
    nb, t = u3.shape[0], u3.shape[1]
    row = lambda i, j: (i, j, 0)
    if up3 is None:
        up_arr = u3
        up_spec = pl.BlockSpec((1, 8, D_SHIFT), lambda i, j: (i, jnp.maximum(j * (tm // 8) - 1, 0), 0))
    else:
        up_arr = up3
        up_spec = pl.BlockSpec((1, tm, D_SHIFT), row)
    cst = lambda i, j: (0, 0)
    out = jax.ShapeDtypeStruct((nb, t, D_RWKV), f32)
    out_spec = pl.BlockSpec((1, tm, D_RWKV), row)
    vec = lambda n: pl.BlockSpec((1, n), cst)
    return pl.pallas_call(
        functools.partial(_prep_kernel, prev_is_tail=up3 is None),
        grid=(nb, t // tm),
        in_specs=[pl.BlockSpec((1, tm, D_SHIFT), row), up_spec,
                  vec(D_SHIFT), vec(512), pl.BlockSpec((64, 512), cst), vec(512), pl.BlockSpec((64, 512), cst),
                  pl.BlockSpec((128, 512), cst), vec(512), vec(512), vec(512), pl.BlockSpec((512, 512), cst)],
        out_specs=[out_spec] * 8,
        out_shape=[out] * 8,
        compiler_params=_params(("parallel", "parallel"), VMEM_LIMIT),
        name="rwkv_prep",
    )(u3, up_arr, pr["mu"], pr["w0"], pr["wd"], pr["a0"], pr["wa"], pr["wg"], pr["k_k"], pr["k_a"], pr["r_k"],
      pr["bd"])


def _each(f, *lists):
    return [f(*xs) for xs in zip(*lists)]


def _wkv_chunk_kernel(r_ref, lw_ref, k_ref, v_ref, kk_ref, b_ref, m_ref, *, nc):
    ii = lax.broadcasted_iota(jnp.int32, (CHUNK, CHUNK), 0)
    jj = lax.broadcasted_iota(jnp.int32, (CHUNK, CHUNK), 1)
    incl = jj <= ii
    strict = jj < ii
    ones_incl = incl.astype(bf16)
    same_blk = (ii >> 4) == (jj >> 4)
    eye = (ii == jj).astype(f32)
    units = [(c, h) for c in range(nc) for h in range(N_HEAD)]
    tile = lambda ref: [ref[0, c * CHUNK:(c + 1) * CHUNK, h * HEAD:(h + 1) * HEAD] for c, h in units]
    r, lw, k, v, kk, b = (tile(ref) for ref in (r_ref, lw_ref, k_ref, v_ref, kk_ref, b_ref))

    gcum = _each(lambda x: _dot_onesl(ones_incl, x), lw)
    gend = _each(lambda x: x[CHUNK - 1:CHUNK, :], gcum)
    kkt = _each(lambda x, gc, l: x * jnp.exp(gc - l), kk, gcum, lw)
    rt = _each(lambda x, gc: x * jnp.exp(gc), r, gcum)
    em = _each(lambda gc: jnp.exp(-gc), gcum)
    kh = _each(jnp.multiply, k, em)
    bh = _each(jnp.multiply, b, em)
    ec = _each(lambda ge, gc: jnp.exp(ge - gc), gend, gcum)
    kg = _each(jnp.multiply, k, ec)
    bg = _each(jnp.multiply, b, ec)

    lhs = _each(lambda x, y: jnp.concatenate([x, y], axis=0), kkt, rt)
    ab = _each(lambda x, y: _dot1(x, y, NT), lhs, bh)
    ak = _each(lambda x, y: _dot1(x, y, NT), lhs, kh)
    a_kb = _each(lambda x: jnp.where(strict, x[:CHUNK], 0.0), ab)
    a_rb = _each(lambda x: jnp.where(incl, x[CHUNK:], 0.0), ab)
    a_kk = _each(lambda x: jnp.where(strict, x[:CHUNK], 0.0), ak)
    a_rk = _each(lambda x: jnp.where(incl, x[CHUNK:], 0.0), ak)

    dg = _each(lambda x: jnp.where(same_blk, x, 0.0), a_kb)
    lo = _each(jnp.subtract, a_kb, dg)
    n1 = _each(jnp.negative, dg)
    n2 = _each(lambda x: _dot1(x, x), n1)
    n4 = _each(lambda x: _dot1(x, x), n2)
    n8 = _each(lambda x: _dot1(x, x), n4)
    t12 = _each(lambda x, y: _dot1(eye + x, eye + y), n1, n2)
    t48 = _each(lambda x, y: _dot1(eye + x, eye + y), n4, n8)
    td = _each(_dot1, t12, t48)
    x1 = _each(lambda x, y: -_dot1(x, y), td, lo)
    x2 = _each(lambda x: _dot1(x, x), x1)
    xx = _each(lambda x, y: _dot1(eye + x, eye + y), x1, x2)
    tinv = _each(_dot1, xx, td)

    akv = _each(_dot1, a_kk, v)
    w1u = _each(lambda t, x, y: _dot1(t, jnp.concatenate([x, y], axis=1)), tinv, kkt, akv)
    z = _each(lambda x, y, w: _dot1(jnp.concatenate([x.T, y], axis=0), w), bg, a_rb, w1u)
    kv = _each(lambda x, y, w: _dot1(jnp.concatenate([x.T, y], axis=0), w), kg, a_rk, v)
    for i, (c, h) in enumerate(units):
        base = jnp.concatenate([eye * jnp.exp(gend[i]), rt[i]], axis=0)
        m_ref[0, h, c] = jnp.concatenate([base, kv[i]], axis=1) - z[i]


def wkv_chunks(r, lw, k, v, kk, b, nc):
    nb, t, _ = r.shape
    nchunk = t // CHUNK
    in_spec = pl.BlockSpec((1, nc * CHUNK, D_RWKV), lambda i, j: (i, j, 0))
    out_spec = pl.BlockSpec((1, N_HEAD, nc, 2 * CHUNK, 2 * HEAD), lambda i, j: (i, 0, j, 0, 0))
    out = jax.ShapeDtypeStruct((nb, N_HEAD, nchunk, 2 * CHUNK, 2 * HEAD), f32)
    return pl.pallas_call(
        functools.partial(_wkv_chunk_kernel, nc=nc),
        grid=(nb, nchunk // nc),
        in_specs=[in_spec] * 6,
        out_specs=out_spec,
        out_shape=out,
        compiler_params=_params(("parallel", "parallel"), VMEM_LIMIT),
        name="wkv_chunks",
    )(r, lw, k, v, kk, b)


def _group_norm(y):
    mu = jnp.mean(y, axis=-1, keepdims=True)
    yc = y - mu
    var = jnp.mean(yc * yc, axis=-1, keepdims=True)
    return yc * lax.rsqrt(var + GN_EPS)


def _wkv_serial_kernel(m_ref, g_ref, bonus_ref, lnw_ref, lnb_ref, o_ref, s_ref, st_ref, *, nb, nc):
    @pl.when(pl.program_id(0) == 0)
    def _():
        st_ref[...] = jnp.zeros_like(st_ref)

    nbh = nb * N_HEAD
    st = [st_ref[i] for i in range(nbh)]
    ys = []
    for c in range(nc):
        zs = [_dot1(m_ref[i, c, :, :HEAD], st[i]) + m_ref[i, c, :, HEAD:] for i in range(nbh)]
        st = [z[:CHUNK] for z in zs]
        ys += [z[CHUNK:] for z in zs]
    for i in range(nbh):
        st_ref[i] = st[i]

    mus = [jnp.mean(y, axis=-1, keepdims=True) for y in ys]
    ycs = _each(jnp.subtract, ys, mus)
    vrs = [jnp.mean(yc * yc, axis=-1, keepdims=True) for yc in ycs]
    yns = _each(lambda yc, vr: yc * lax.rsqrt(vr + GN_EPS), ycs, vrs)
    for c in range(nc):
        rows = slice(c * CHUNK, (c + 1) * CHUNK)
        for bi in range(nb):
            yn = jnp.concatenate(yns[c * nbh + bi * N_HEAD:c * nbh + (bi + 1) * N_HEAD], axis=-1)
            o_ref[bi, rows, :] = (yn * lnw_ref[...] + lnb_ref[...] + bonus_ref[bi, rows, :]) * g_ref[bi, rows, :]

    @pl.when(pl.program_id(0) == pl.num_programs(0) - 1)
    def _():
        s_ref[...] = st_ref[...]


def wkv_serial(maps, g, bonus, lnw, lnb, nc):
    nbh, nchunk = maps.shape[0], maps.shape[1]
    nb = g.shape[0]
    blk = pl.BlockSpec((nbh, nc, 2 * CHUNK, 2 * HEAD), lambda j: (0, j, 0, 0))
    tok = pl.BlockSpec((nb, nc * CHUNK, D_RWKV), lambda j: (0, j, 0))
    par = pl.BlockSpec((1, D_RWKV), lambda j: (0, 0))
    return pl.pallas_call(
        functools.partial(_wkv_serial_kernel, nb=nb, nc=nc),
        grid=(nchunk // nc,),
        in_specs=[blk, tok, tok, par, par],
        out_specs=[tok, pl.BlockSpec((nbh, HEAD, HEAD), lambda j: (0, 0, 0))],
        out_shape=[jax.ShapeDtypeStruct((nb, nchunk * CHUNK, D_RWKV), f32),
                   jax.ShapeDtypeStruct((nbh, HEAD, HEAD), f32)],
        scratch_shapes=[pltpu.VMEM((nbh, HEAD, HEAD), f32)],
        compiler_params=_params(("arbitrary",), VMEM_LIMIT),
        name="wkv_serial",
    )(maps, g, bonus, lnw, lnb)


def _wkv_step_kernel(s_ref, r_ref, lw_ref, k_ref, kk_ref, b_ref, vc_ref, so_ref, y_ref):
    s = s_ref[0]
    skk = jnp.sum(s * kk_ref[0], axis=-1, keepdims=True)
    s1 = s * jnp.exp(lw_ref[0]) - skk * b_ref[0] + vc_ref[0] * k_ref[0]
    so_ref[0] = s1
    y_ref[0] = jnp.sum(s1 * r_ref[0], axis=-1, keepdims=True)


def wkv_step(s, r, lw, k, kk, b, vcol):
    n = s.shape[0]
    row = pl.BlockSpec((1, N_HEAD, 1, HEAD), lambda i: (i, 0, 0, 0))
    col = pl.BlockSpec((1, N_HEAD, HEAD, 1), lambda i: (i, 0, 0, 0))
    mat = pl.BlockSpec((1, N_HEAD, HEAD, HEAD), lambda i: (i, 0, 0, 0))
    return pl.pallas_call(
        _wkv_step_kernel,
        grid=(n,),
        in_specs=[mat, row, row, row, row, row, col],
        out_specs=[mat, col],
        out_shape=[jax.ShapeDtypeStruct(s.shape, f32), jax.ShapeDtypeStruct((n, N_HEAD, HEAD, 1), f32)],
        compiler_params=_params(("parallel",)),
        name="wkv_step",
    )(s, r, lw, k, kk, b, vcol)


def _post_kernel(y_ref, g_ref, bonus_ref, lnw_ref, lnb_ref, o_ref):
    yn = jnp.concatenate([_group_norm(y_ref[0, h]) for h in range(N_HEAD)], axis=-1)
    o_ref[0] = (yn * lnw_ref[...] + lnb_ref[...] + bonus_ref[0]) * g_ref[0]


def rwkv_post(y, g, bonus, lnw, lnb, tm):
    nb, _, t, _ = y.shape
    hm = pl.BlockSpec((1, N_HEAD, tm, HEAD), lambda i, j: (i, 0, j, 0))
    tok = pl.BlockSpec((1, tm, D_RWKV), lambda i, j: (i, j, 0))
    par = pl.BlockSpec((1, D_RWKV), lambda i, j: (0, 0))
    return pl.pallas_call(
        _post_kernel,
        grid=(nb, t // tm),
        in_specs=[hm, tok, tok, par, par],
        out_specs=tok,
        out_shape=jax.ShapeDtypeStruct((nb, t, D_RWKV), f32),
        compiler_params=_params(("parallel", "parallel"), VMEM_LIMIT),
        name="rwkv_post",
    )(y, g, bonus, lnw, lnb)


def _logf_kernel(u_ref, b_ref, lf_ref, c_ref, *, nblk):
    ii = lax.broadcasted_iota(jnp.int32, (128, 128), 0)
    jj = lax.broadcasted_iota(jnp.int32, (128, 128), 1)
    upper = (ii <= jj).astype(bf16)
    nb = u_ref.shape[0]
    carry = jnp.zeros((nb * N_HEAD, 1), f32)
    for blk in range(nblk):
        sl = slice(128 * blk, 128 * (blk + 1))
        fl = jnp.concatenate([u_ref[bi, sl, :].T[:N_HEAD] for bi in range(nb)], axis=0)
        lf = jax.nn.log_sigmoid(fl + b_ref[...])
        lf_ref[:, sl] = lf
        cs = _dot_onesr(lf, upper) + carry
        c_ref[:, sl] = cs
        carry = cs[:, 127:128]


def logf_cumsum(u3, bias):
    nb, t, _ = u3.shape
    out = jax.ShapeDtypeStruct((nb * N_HEAD, t), f32)
    return pl.pallas_call(
        functools.partial(_logf_kernel, nblk=t // 128),
        grid=(1,),
        in_specs=[pl.BlockSpec((nb, t, 128), lambda i: (0, 0, COL_F // 128)),
                  pl.BlockSpec((nb * N_HEAD, 1), lambda i: (0, 0))],
        out_specs=[pl.BlockSpec((nb * N_HEAD, t), lambda i: (0, 0))] * 2,
        out_shape=[out, out],
        compiler_params=_params(("arbitrary",), VMEM_LIMIT),
        name="logf_cumsum",
    )(u3, bias)


SUM_ROWS = 16


def _fox_kernel(qi_ref, ki_ref, q_ref, k_ref, v_ref, ck_ref, o_ref, m_ref, acc_ref, *, tq, tk, nsub):
    step_id = pl.program_id(2)
    qi = qi_ref[step_id]
    ki = ki_ref[step_id]

    @pl.when(ki == 0)
    def _():
        m_ref[...] = jnp.full_like(m_ref, NEG)
        acc_ref[...] = jnp.zeros_like(acc_ref)

    def step(masked):
        lane_lo = lax.broadcasted_iota(jnp.int32, (tq, 2 * HEAD), 1) < HEAD
        q = q_ref[0] * (HEAD ** -0.5 * LOG2E)
        qes = [jnp.where(lane_lo == (e == 0), q, 0.0).astype(bf16) for e in range(2)]
        ts = tk // nsub
        ckt = ck_ref[0, 0].T

        def first_col(sub):
            return (sub * ts) // 128 * 128 if masked else 0

        def scores(sub):
            rs = slice(sub * ts, (sub + 1) * ts)
            lo = first_col(sub)
            kb = k_ref[0, rs, :].astype(bf16)
            return [_dot(kb, qes[e][lo:, :], NT) for e in range(2)]

        s_cur = scores(0)
        for sub in range(nsub):
            s_next = scores(sub + 1) if sub + 1 < nsub else None
            rs = slice(sub * ts, (sub + 1) * ts)
            lo = first_col(sub)
            vt = v_ref[0, rs, :].T.astype(bf16)
            ones = jnp.ones((SUM_ROWS, ts), bf16)
            ps, alphas = [], []
            for e in range(2):
                s = s_cur[e] - ckt[rs, e:e + 1]
                if masked:
                    row = lax.broadcasted_iota(jnp.int32, s.shape, 0) + sub * ts
                    col = lax.broadcasted_iota(jnp.int32, s.shape, 1) + lo
                    s = jnp.where(row <= col, s, NEG)
                m_old = m_ref[e, :, lo:]
                m_new = jnp.maximum(m_old, jnp.max(s, axis=0, keepdims=True))
                alphas.append(jnp.exp2(m_old - m_new))
                ps.append(jnp.exp2(s - m_new).astype(bf16))
                m_ref[e, :, lo:] = m_new
            for e in range(2):
                vte = jnp.concatenate([vt[HEAD * e:HEAD * (e + 1), :], ones], axis=0)
                acc_ref[e, :, lo:] = alphas[e] * acc_ref[e, :, lo:] + _dot(vte, ps[e])
            s_cur = s_next

    @pl.when(ki < qi)
    def _():
        step(False)

    @pl.when(ki == qi)
    def _():
        step(True)
        o = [acc_ref[e, 0:HEAD, :] / acc_ref[e, HEAD:HEAD + 1, :] for e in range(2)]
        o_ref[0] = jnp.concatenate(o, axis=0).T


def fox_prompt(u3, ck, tq, nsub):
    nb, t, _ = u3.shape
    nq = t // tq
    cq, ckk, cv = COL_Q // 128, (COL_Q + 512) // 128, (COL_Q + 1024) // 128
    pairs = [(i, j) for i in range(nq) for j in range(i + 1)]
    qi_tab = jnp.asarray([p[0] for p in pairs], jnp.int32)
    ki_tab = jnp.asarray([p[1] for p in pairs], jnp.int32)
    return pl.pallas_call(
        functools.partial(_fox_kernel, tq=tq, tk=tq, nsub=nsub),
        grid_spec=pltpu.PrefetchScalarGridSpec(
            num_scalar_prefetch=2,
            grid=(nb, N_HEAD // 2, len(pairs)),
            in_specs=[pl.BlockSpec((1, tq, 128), lambda b, h, s, qt, kt: (b, qt[s], cq + h)),
                      pl.BlockSpec((1, tq, 128), lambda b, h, s, qt, kt: (b, kt[s], ckk + h)),
                      pl.BlockSpec((1, tq, 128), lambda b, h, s, qt, kt: (b, kt[s], cv + h)),
                      pl.BlockSpec((1, 1, 8, tq), lambda b, h, s, qt, kt: (b, h, 0, kt[s]))],
            out_specs=pl.BlockSpec((1, tq, 128), lambda b, h, s, qt, kt: (b, qt[s], h)),
            scratch_shapes=[pltpu.VMEM((2, 1, tq), f32), pltpu.VMEM((2, HEAD + SUM_ROWS, tq), f32)]),
        out_shape=jax.ShapeDtypeStruct((nb, t, 512), f32),
        compiler_params=_params(("parallel", "parallel", "arbitrary"), VMEM_LIMIT),
        name="fox_prompt",
    )(qi_tab, ki_tab, u3, u3, u3, ck)


def _paged_kernel(pt_ref, q_ref, kn_ref, vn_ref, lfn_ref, *refs, g):
    k_refs, v_refs, lf_refs = refs[0:g], refs[g:2 * g], refs[2 * g:3 * g]
    o_ref = refs[3 * g]
    m_ref, l_ref, acc_ref, car_ref = refs[3 * g + 1:]
    step = pl.program_id(1)
    heads = [slice(HEAD * h, HEAD * (h + 1)) for h in range(N_HEAD)]
    q = q_ref[0]

    @pl.when(step == 0)
    def _():
        lane = lax.broadcasted_iota(jnp.int32, acc_ref.shape, 1)
        acc_ref[...] = jnp.where(lane == 0, vn_ref[0], 0.0)
        qk = q * kn_ref[0]
        for h in range(N_HEAD):
            m_ref[h] = jnp.sum(qk[heads[h]], axis=0, keepdims=True)
            l_ref[h] = jnp.ones((1, 1), f32)
        car_ref[...] = lfn_ref[0]

    ii = lax.broadcasted_iota(jnp.int32, (PAGE, PAGE), 0)
    jj = lax.broadcasted_iota(jnp.int32, (PAGE, PAGE), 1)
    later = (ii > jj).astype(bf16)
    carry = car_ref[...]
    biases = []
    for j in range(g):
        lfp = lf_refs[j][0]
        biases.append(_dot_onesr(lfp, later) + carry)
        carry = carry + jnp.sum(lfp, axis=-1, keepdims=True)
    car_ref[...] = carry

    ss = [jnp.concatenate(
        [jnp.sum(k_refs[j][0, heads[h], :] * q[heads[h]], axis=0, keepdims=True) + biases[j][h:h + 1, :]
         for j in range(g)], axis=0) for h in range(N_HEAD)]
    m_olds = [m_ref[h] for h in range(N_HEAD)]
    m_news = [jnp.maximum(m, jnp.max(jnp.max(s, axis=-1, keepdims=True), axis=0, keepdims=True))
              for m, s in zip(m_olds, ss)]
    alphas = [jnp.exp(mo - mn) for mo, mn in zip(m_olds, m_news)]
    ps = [jnp.exp(s - mn) for s, mn in zip(ss, m_news)]
    for h in range(N_HEAD):
        l_ref[h] = alphas[h] * l_ref[h] + jnp.sum(jnp.sum(ps[h], axis=-1, keepdims=True), axis=0, keepdims=True)
        m_ref[h] = m_news[h]
    for h in range(N_HEAD):
        acc = alphas[h] * acc_ref[heads[h], :]
        for j in range(g):
            acc = acc + v_refs[j][0, heads[h], :] * ps[h][j:j + 1, :]
        acc_ref[heads[h], :] = acc

    @pl.when(step == pl.num_programs(1) - 1)
    def _():
        for h in range(N_HEAD):
            o_ref[0, heads[h], :] = jnp.sum(acc_ref[heads[h], :], axis=-1, keepdims=True) / l_ref[h]


def fox_paged(page_table, q, knew, vnew, lfnew, kpool, vpool, lfpool, g):
    nb, npages = page_table.shape
    last = npages - 1
    cur = lambda b, i, pt: (b, 0, 0)
    pool = lambda j: (lambda b, i, pt: (pt[b, last - (i * g + j)], 0, 0))
    col = pl.BlockSpec((1, 512, 1), cur)
    return pl.pallas_call(
        functools.partial(_paged_kernel, g=g),
        grid_spec=pltpu.PrefetchScalarGridSpec(
            num_scalar_prefetch=1,
            grid=(nb, npages // g),
            in_specs=([col, col, col, pl.BlockSpec((1, N_HEAD, 1), cur)]
                      + [pl.BlockSpec((1, 512, PAGE), pool(j)) for j in range(g)]
                      + [pl.BlockSpec((1, 512, PAGE), pool(j)) for j in range(g)]
                      + [pl.BlockSpec((1, N_HEAD, PAGE), pool(j)) for j in range(g)]),
            out_specs=col,
            scratch_shapes=[pltpu.VMEM((N_HEAD, 1, 1), f32), pltpu.VMEM((N_HEAD, 1, 1), f32),
                            pltpu.VMEM((512, PAGE), f32), pltpu.VMEM((N_HEAD, 1), f32)]),
        out_shape=jax.ShapeDtypeStruct((nb, 512, 1), f32),
        compiler_params=_params(("parallel", "arbitrary"), VMEM_LIMIT),
        name="fox_paged",
    )(page_table, q, knew, vnew, lfnew, *([kpool] * g), *([vpool] * g), *([lfpool] * g))


def _merge_kernel(x_ref, or_ref, of_ref, ug_ref, wr_ref, wf_ref, wo_ref, o_ref):
    br = _dot(or_ref[...].astype(bf16), wr_ref[...])
    bf = _dot(of_ref[...].astype(bf16), wf_ref[...])
    ug = ug_ref[...]
    merged = jax.nn.sigmoid(ug[:, :D_MODEL]) * br + jax.nn.sigmoid(ug[:, D_MODEL:]) * bf
    o_ref[...] = x_ref[...] + _dot(merged.astype(bf16), wo_ref[...])


def merge_out(x, o_r, o_f, u, wr, wf, wo, tm):
    m = x.shape[0]
    row = lambda i: (i, 0)
    cst = lambda i: (0, 0)
    return pl.pallas_call(
        _merge_kernel,
        grid=(m // tm,),
        in_specs=[pl.BlockSpec((tm, D_MODEL), row), pl.BlockSpec((tm, 512), row), pl.BlockSpec((tm, 512), row),
                  pl.BlockSpec((tm, 2 * D_MODEL), lambda i: (i, COL_G // (2 * D_MODEL))),
                  pl.BlockSpec((512, D_MODEL), cst), pl.BlockSpec((512, D_MODEL), cst),
                  pl.BlockSpec((D_MODEL, D_MODEL), cst)],
        out_specs=pl.BlockSpec((tm, D_MODEL), row),
        out_shape=jax.ShapeDtypeStruct((m, D_MODEL), f32),
        compiler_params=_params(("parallel",), VMEM_LIMIT),
        name="merge_out",
    )(x, o_r, o_f, u, wr, wf, wo)


def _ffn_kernel(x_ref, g2_ref, gf_ref, wu_ref, wd_ref, o_ref, h_ref, acc_ref):
    j = pl.program_id(1)

    @pl.when(j == 0)
    def _():
        x = x_ref[...]
        ms = jnp.mean(x * x, axis=-1, keepdims=True)
        h_ref[...] = (x * lax.rsqrt(ms + RMS_EPS) * g2_ref[...]).astype(bf16)
        acc_ref[...] = jnp.zeros_like(acc_ref)

    hid = jnp.maximum(_dot(h_ref[...], wu_ref[...]), 0.0)
    acc_ref[...] += _dot((hid * hid).astype(bf16), wd_ref[...])

    @pl.when(j == pl.num_programs(1) - 1)
    def _():
        x2 = x_ref[...] + acc_ref[...]
        ms = jnp.mean(x2 * x2, axis=-1, keepdims=True)
        o_ref[...] = x2 * lax.rsqrt(ms + RMS_EPS) * gf_ref[...]


def ffn_final(x, g2, gf, wu, wd, tm, tf):
    m = x.shape[0]
    return pl.pallas_call(
        _ffn_kernel,
        grid=(m // tm, D_FF // tf),
        in_specs=[pl.BlockSpec((tm, D_MODEL), lambda i, j: (i, 0)),
                  pl.BlockSpec((1, D_MODEL), lambda i, j: (0, 0)), pl.BlockSpec((1, D_MODEL), lambda i, j: (0, 0)),
                  pl.BlockSpec((D_MODEL, tf), lambda i, j: (0, j)), pl.BlockSpec((tf, D_MODEL), lambda i, j: (j, 0))],
        out_specs=pl.BlockSpec((tm, D_MODEL), lambda i, j: (i, 0)),
        out_shape=jax.ShapeDtypeStruct((m, D_MODEL), f32),
        scratch_shapes=[pltpu.VMEM((tm, D_MODEL), bf16), pltpu.VMEM((tm, D_MODEL), f32)],
        compiler_params=_params(("parallel", "arbitrary"), VMEM_LIMIT),
        name="ffn_final",
    )(x, g2, gf, wu, wd)


def kernel(x_prompt, x_sample, state_shift, state_wkv, cache_k, cache_v, cache_logf, page_table, meta_tokens,
           norm_mix, w_in, mu_shift, w0, w_decay_up, a0, w_a_up, w_g_up, k_k, k_a, r_k, ln_x_w, ln_x_b, b_forget,
           w_br_rwkv, w_br_fox, w_out, norm_ffn, w_ffn_up, w_ffn_down, norm_final):
    assert norm_mix.shape[0] == 1, "single layer"
    nb, seq, _ = x_prompt.shape
    ns = x_sample.shape[0]
    tp = FRONT_PAD + N_META + seq
    tlen = N_META + seq
    tm = 640

    wi = w_in[0]
    wi = jnp.concatenate([wi[:, 0:1792], wi[:, 3328:3336], jnp.zeros((D_MODEL, 248), f32),
                          wi[:, 3336:5384], wi[:, 1792:3328]], axis=1).astype(bf16)
    row2 = lambda a: a.reshape(1, -1)
    lane_head = jnp.arange(512, dtype=jnp.int32) // HEAD
    pr = dict(mu=row2(mu_shift[0]), w0=row2(w0[0]), wd=w_decay_up[0], a0=row2(a0[0]), wa=w_a_up[0], wg=w_g_up[0],
              k_k=row2(k_k[0]), k_a=row2(k_a[0]), r_k=row2(r_k[0]),
              bd=(lane_head[:, None] == lane_head[None, :]).astype(bf16))
    lnw, lnb = row2(ln_x_w[0]), row2(ln_x_b[0])
    g_mix, g_ffn, g_fin = row2(norm_mix[0]), row2(norm_ffn[0]), row2(norm_final)
    wr, wf, wo = w_br_rwkv[0].astype(bf16), w_br_fox[0].astype(bf16), w_out[0].astype(bf16)
    wu, wdn = w_ffn_up[0].astype(bf16), w_ffn_down[0].astype(bf16)

    meta = jnp.broadcast_to(meta_tokens[None], (nb, N_META, D_MODEL))
    xp = jnp.concatenate([jnp.zeros((nb, FRONT_PAD, D_MODEL), f32), meta, x_prompt], axis=1)
    xp2 = xp.reshape(nb * tp, D_MODEL)
    u2 = norm_matmul(xp2, g_mix, wi, 1664, 1408)
    u3 = u2.reshape(nb, tp, D_INP)
    us2 = norm_matmul(x_sample.reshape(ns, D_MODEL), g_mix, wi, ns, 1408)

    r, lw, k2, v, kkn, b, g, bonus = rwkv_prep(u3, None, pr, 320)
    maps = wkv_chunks(r, lw, k2, v, kkn, b, 5)
    maps = maps.reshape(nb * N_HEAD, tp // CHUNK, 2 * CHUNK, 2 * HEAD)
    o_r, s_kv = wkv_serial(maps, g, bonus, lnw, lnb, 5)
    new_wkv_p = jnp.swapaxes(s_kv, 1, 2).reshape(1, nb, N_HEAD, HEAD, HEAD)

    us3 = us2.reshape(1, ns, D_INP)
    sp = rwkv_prep(us3, state_shift[0].reshape(1, ns, D_SHIFT), pr, ns)
    rs, lws, ks, vs, kks, bs, gs, bonus_s = sp
    rowify = lambda a: a.reshape(ns, N_HEAD, 1, HEAD)
    s_new, ys = wkv_step(state_wkv[0], rowify(rs), rowify(lws), rowify(ks), rowify(kks), rowify(bs),
                         rowify(vs).reshape(ns, N_HEAD, HEAD, 1))
    ys_hm = jnp.transpose(ys.reshape(ns, N_HEAD, HEAD), (1, 0, 2))[None]
    o_r_s = rwkv_post(ys_hm, gs, bonus_s, lnw, lnb, ns)[0]

    bias_rows = jnp.tile(b_forget[0], nb).reshape(nb * N_HEAD, 1)
    lf_t, c_t = logf_cumsum(u3, bias_rows)
    pad_key = jnp.arange(tp, dtype=jnp.int32)[None, :] < FRONT_PAD
    ck = jnp.where(pad_key, -NEG, c_t * LOG2E).reshape(nb, N_HEAD // 2, 2, tp)
    o_f = fox_prompt(u3, jnp.pad(ck, ((0, 0), (0, 0), (0, 6), (0, 0))), 1664, 8)

    qs = us2[:, COL_Q:COL_Q + 512] * (HEAD ** -0.5)
    k_s = us2[:, COL_Q + 512:COL_Q + 1024]
    v_s = us2[:, COL_Q + 1024:COL_Q + 1536]
    lf_s = jax.nn.log_sigmoid(us2[:, COL_F:COL_F + N_HEAD] + b_forget[0][None, :])
    n_pool = cache_k.shape[1]
    kpool = jnp.transpose(cache_k[0], (0, 2, 3, 1)).reshape(n_pool, 512, PAGE)
    vpool = jnp.transpose(cache_v[0], (0, 2, 3, 1)).reshape(n_pool, 512, PAGE)
    o_f_s = fox_paged(page_table, qs.reshape(ns, 512, 1), k_s.reshape(ns, 512, 1), v_s.reshape(ns, 512, 1),
                      lf_s.reshape(ns, N_HEAD, 1), kpool, vpool, jnp.swapaxes(cache_logf[0], 1, 2),
                      32).reshape(ns, 512)

    x1 = merge_out(xp2, o_r.reshape(nb * tp, 512), o_f.reshape(nb * tp, 512), u2, wr, wf, wo, tm)
    yp = ffn_final(x1, g_ffn, g_fin, wu, wdn, 1664, 512).reshape(nb, tp, D_MODEL)
    x1s = merge_out(x_sample.reshape(ns, D_MODEL), o_r_s, o_f_s, us2, wr, wf, wo, ns)
    ysamp = ffn_final(x1s, g_ffn, g_fin, wu, wdn, ns, 512)

    y_prompt = yp[:, ROW0:]
    y_sample = ysamp.reshape(ns, 1, D_MODEL)
    new_shift_p = u3[:, tp - 1, :D_SHIFT][None]
    k_p = u3[:, FRONT_PAD:, COL_Q + 512:COL_Q + 1024].reshape(1, nb, tlen, N_HEAD, HEAD)
    v_p = u3[:, FRONT_PAD:, COL_Q + 1024:COL_Q + 1536].reshape(1, nb, tlen, N_HEAD, HEAD)
    lf_p = jnp.transpose(lf_t.reshape(nb, N_HEAD, tp), (0, 2, 1))[:, FRONT_PAD:][None]
    return (y_prompt, y_sample, new_shift_p, new_wkv_p, k_p, v_p, lf_p,
            us2[:, :D_SHIFT][None], s_new[None], k_s.reshape(1, ns, 1, N_HEAD, HEAD),
            v_s.reshape(1, ns, 1, N_HEAD, HEAD), lf_s.reshape(1, ns, 1, N_HEAD))
```

```python
import functools

import jax
import jax.numpy as jnp
from jax import lax
from jax.experimental import pallas as pl
from jax.experimental.pallas import tpu as pltpu

f32 = jnp.float32
bf16 = jnp.bfloat16

D_MODEL = 1024
N_META = 16
D_RWKV = 512
HEAD = 64
N_HEAD = 8
D_SHIFT = 1792
D_FF = 4096
RMS_EPS = 1e-6
GN_EPS = 64e-5
PAGE = 128

FRONT_PAD = 112
ROW0 = FRONT_PAD + N_META
CHUNK = 64
NEG = -1e30
LOG2E = 1.4426950408889634

COL_R = 0
COL_F = 1792
COL_G = 2048
COL_Q = 4096
D_INP = 5632

VMEM_LIMIT = 56 * 1024 * 1024

TM_PROJ, TN_PROJ = 1664, 1408
TM_PREP = 320
CHUNKS_PER_STEP = 5
TQ_FOX, NSUB_FOX = 1664, 8
PAGES_PER_STEP = 32
TM_MERGE = 640
TM_FFN, TF_FFN = 1664, 512
STEP_ROWS = 4

NN = (((1,), (0,)), ((), ()))
NT = (((1,), (1,)), ((), ()))
TN = (((0,), (0,)), ((), ()))


def _dot(a, b, dims=NN):
    return lax.dot_general(a, b, dims, preferred_element_type=f32)


def _split2(x):
    hi = x.astype(bf16)
    lo = (x - hi.astype(f32)).astype(bf16)
    return hi, lo


def _split3(x):
    hi = x.astype(bf16)
    r1 = x - hi.astype(f32)
    mid = r1.astype(bf16)
    lo = (r1 - mid.astype(f32)).astype(bf16)
    return hi, mid, lo


def _dot3(a, b, dims=NN):
    ah, al = _split2(a)
    bh, bl = _split2(b)
    return _dot(ah, bh, dims) + (_dot(ah, bl, dims) + _dot(al, bh, dims))


def _dot1(a, b, dims=NN):
    return _dot(a.astype(bf16), b.astype(bf16), dims)


def _dot_onesr(x, ones_bf16, dims=NN):
    hi, mid, lo = _split3(x)
    return _dot(hi, ones_bf16, dims) + (_dot(mid, ones_bf16, dims) + _dot(lo, ones_bf16, dims))


def _segsum(x, ones_bf16):
    hi, lo = _split2(x)
    return _dot(hi, ones_bf16) + _dot(lo, ones_bf16)


def _dot_onesl(ones_bf16, x, dims=NN):
    hi, mid, lo = _split3(x)
    return _dot(ones_bf16, hi, dims) + (_dot(ones_bf16, mid, dims) + _dot(ones_bf16, lo, dims))


def _params(sem, vmem=None):
    return pltpu.CompilerParams(dimension_semantics=sem, vmem_limit_bytes=vmem)


def _norm_matmul_kernel(x_ref, g_ref, w_ref, o_ref, h_ref):
    @pl.when(pl.program_id(1) == 0)
    def _():
        x = x_ref[...]
        ms = jnp.mean(x * x, axis=-1, keepdims=True)
        h_ref[...] = (x * lax.rsqrt(ms + RMS_EPS) * g_ref[...]).astype(bf16)

    o_ref[...] = _dot(h_ref[...], w_ref[...])


def norm_matmul(x, g, w, tm, tn):
    m, d = x.shape
    n = w.shape[1]
    return pl.pallas_call(
        _norm_matmul_kernel,
        grid=(m // tm, n // tn),
        in_specs=[pl.BlockSpec((tm, d), lambda i, j: (i, 0)),
                  pl.BlockSpec((1, d), lambda i, j: (0, 0)),
                  pl.BlockSpec((d, tn), lambda i, j: (0, j))],
        out_specs=pl.BlockSpec((tm, tn), lambda i, j: (i, j)),
        out_shape=jax.ShapeDtypeStruct((m, n), f32),
        scratch_shapes=[pltpu.VMEM((tm, d), bf16)],
        compiler_params=_params(("parallel", "arbitrary"), VMEM_LIMIT),
        name="norm_matmul",
    )(x, g, w)


def _prep_kernel(u_ref, up_ref, mu_ref, w0_ref, wd_ref, a0_ref, wa_ref, wg_ref, kk_ref, ka_ref, rk_ref,
                 bd_ref, r_o, lw_o, k_o, v_o, kkn_o, b_o, g_o, bonus_o, *, prev_is_tail):
    u = u_ref[0]
    if prev_is_tail:
        first = jnp.where(pl.program_id(1) == 0, 0.0, up_ref[0, 7:8, :])
        rowi = lax.broadcasted_iota(jnp.int32, u.shape, 0)
        up = jnp.where(rowi == 0, first, pltpu.roll(u, 1, 0))
    else:
        up = up_ref[0]
    us = u + mu_ref[...] * (up - u)
    r = us[:, 0:512]
    k = us[:, 512:1024]
    v = us[:, 1024:1536]
    zw = us[:, 1536:1600]
    za = us[:, 1600:1664]
    zg = us[:, 1664:1792]
    bd = bd_ref[...]

    z = -(w0_ref[...] + _dot3(jnp.tanh(zw), wd_ref[...]))
    w_raw = -(jnp.maximum(z, 0.0) + jnp.log1p(jnp.exp(-jnp.abs(z)))) - 0.5
    lw = -jnp.exp(w_raw)
    a = jax.nn.sigmoid(a0_ref[...] + _dot3(za, wa_ref[...]))
    g = _dot3(jax.nn.sigmoid(zg), wg_ref[...])
    kk = k * kk_ref[...]
    ss = _segsum(kk * kk, bd)
    kkn = kk / jnp.maximum(jnp.sqrt(ss), 1e-12)
    k2 = k * (1.0 + (a - 1.0) * ka_ref[...])
    b = kkn * a
    bonus = _segsum(r * k2 * rk_ref[...], bd) * v

    r_o[0] = r
    lw_o[0] = lw
    k_o[0] = k2
    v_o[0] = v
    kkn_o[0] = kkn
    b_o[0] = b
    g_o[0] = g
    bonus_o[0] = bonus


def rwkv_prep(u3, up3, pr, tm):
    nb, t = u3.shape[0], u3.shape[1]
    row = lambda i, j: (i, j, 0)
    if up3 is None:
        up_arr = u3
        up_spec = pl.BlockSpec((1, 8, D_SHIFT), lambda i, j: (i, jnp.maximum(j * (tm // 8) - 1, 0), 0))
    else:
        up_arr = up3
        up_spec = pl.BlockSpec((1, tm, D_SHIFT), row)
    cst = lambda i, j: (0, 0)
    out = jax.ShapeDtypeStruct((nb, t, D_RWKV), f32)
    out_spec = pl.BlockSpec((1, tm, D_RWKV), row)
    vec = lambda n: pl.BlockSpec((1, n), cst)
    return pl.pallas_call(
        functools.partial(_prep_kernel, prev_is_tail=up3 is None),
        grid=(nb, t // tm),
        in_specs=[pl.BlockSpec((1, tm, D_SHIFT), row), up_spec,
                  vec(D_SHIFT), vec(512), pl.BlockSpec((64, 512), cst), vec(512), pl.BlockSpec((64, 512), cst),
                  pl.BlockSpec((128, 512), cst), vec(512), vec(512), vec(512), pl.BlockSpec((512, 512), cst)],
        out_specs=[out_spec] * 8,
        out_shape=[out] * 8,
        compiler_params=_params(("parallel", "parallel"), VMEM_LIMIT),
        name="rwkv_prep",
    )(u3, up_arr, pr["mu"], pr["w0"], pr["wd"], pr["a0"], pr["wa"], pr["wg"], pr["k_k"], pr["k_a"], pr["r_k"],
      pr["bd"])


def _each(f, *lists):
    return [f(*xs) for xs in zip(*lists)]


def _wkv_chunk_kernel(r_ref, lw_ref, k_ref, v_ref, kk_ref, b_ref, m_ref, *, nc):
    ii = lax.broadcasted_iota(jnp.int32, (CHUNK, CHUNK), 0)
    jj = lax.broadcasted_iota(jnp.int32, (CHUNK, CHUNK), 1)
    incl = jj <= ii
    strict = jj < ii
    ones_incl = incl.astype(bf16)
    same_blk = (ii >> 4) == (jj >> 4)
    eye = (ii == jj).astype(f32)
    units = [(c, h) for c in range(nc) for h in range(N_HEAD)]
    tile = lambda ref: [ref[0, c * CHUNK:(c + 1) * CHUNK, h * HEAD:(h + 1) * HEAD] for c, h in units]
    r, lw, k, v, kk, b = (tile(ref) for ref in (r_ref, lw_ref, k_ref, v_ref, kk_ref, b_ref))

    gcum = _each(lambda x: _dot_onesl(ones_incl, x), lw)
    gend = _each(lambda x: x[CHUNK - 1:CHUNK, :], gcum)
    kkt = _each(lambda x, gc, l: x * jnp.exp(gc - l), kk, gcum, lw)
    rt = _each(lambda x, gc: x * jnp.exp(gc), r, gcum)
    em = _each(lambda gc: jnp.exp(-gc), gcum)
    kh = _each(jnp.multiply, k, em)
    bh = _each(jnp.multiply, b, em)
    ec = _each(lambda ge, gc: jnp.exp(ge - gc), gend, gcum)
    kg = _each(jnp.multiply, k, ec)
    bg = _each(jnp.multiply, b, ec)

    lhs = _each(lambda x, y: jnp.concatenate([x, y], axis=0), kkt, rt)
    ab = _each(lambda x, y: _dot1(x, y, NT), lhs, bh)
    ak = _each(lambda x, y: _dot1(x, y, NT), lhs, kh)
    a_kb = _each(lambda x: jnp.where(strict, x[:CHUNK], 0.0), ab)
    a_rb = _each(lambda x: jnp.where(incl, x[CHUNK:], 0.0), ab)
    a_kk = _each(lambda x: jnp.where(strict, x[:CHUNK], 0.0), ak)
    a_rk = _each(lambda x: jnp.where(incl, x[CHUNK:], 0.0), ak)

    dg = _each(lambda x: jnp.where(same_blk, x, 0.0), a_kb)
    lo = _each(jnp.subtract, a_kb, dg)
    n1 = _each(jnp.negative, dg)
    n2 = _each(lambda x: _dot1(x, x), n1)
    n4 = _each(lambda x: _dot1(x, x), n2)
    n8 = _each(lambda x: _dot1(x, x), n4)
    t12 = _each(lambda x, y: _dot1(eye + x, eye + y), n1, n2)
    t48 = _each(lambda x, y: _dot1(eye + x, eye + y), n4, n8)
    td = _each(_dot1, t12, t48)
    x1 = _each(lambda x, y: -_dot1(x, y), td, lo)
    x2 = _each(lambda x: _dot1(x, x), x1)
    xx = _each(lambda x, y: _dot1(eye + x, eye + y), x1, x2)
    tinv = _each(_dot1, xx, td)

    akv = _each(_dot1, a_kk, v)
    w1u = _each(lambda t, x, y: _dot1(t, jnp.concatenate([x, y], axis=1)), tinv, kkt, akv)
    z = _each(lambda x, y, w: _dot1(jnp.concatenate([x.T, y], axis=0), w), bg, a_rb, w1u)
    kv = _each(lambda x, y, w: _dot1(jnp.concatenate([x.T, y], axis=0), w), kg, a_rk, v)
    for i, (c, h) in enumerate(units):
        base = jnp.concatenate([eye * jnp.exp(gend[i]), rt[i]], axis=0)
        m_ref[0, h, c] = jnp.concatenate([base, kv[i]], axis=1) - z[i]


def wkv_chunks(r, lw, k, v, kk, b, nc):
    nb, t, _ = r.shape
    nchunk = t // CHUNK
    in_spec = pl.BlockSpec((1, nc * CHUNK, D_RWKV), lambda i, j: (i, j, 0))
    out_spec = pl.BlockSpec((1, N_HEAD, nc, 2 * CHUNK, 2 * HEAD), lambda i, j: (i, 0, j, 0, 0))
    out = jax.ShapeDtypeStruct((nb, N_HEAD, nchunk, 2 * CHUNK, 2 * HEAD), f32)
    return pl.pallas_call(
        functools.partial(_wkv_chunk_kernel, nc=nc),
        grid=(nb, nchunk // nc),
        in_specs=[in_spec] * 6,
        out_specs=out_spec,
        out_shape=out,
        compiler_params=_params(("parallel", "parallel"), VMEM_LIMIT),
        name="wkv_chunks",
    )(r, lw, k, v, kk, b)


def _group_norm(y):
    mu = jnp.mean(y, axis=-1, keepdims=True)
    yc = y - mu
    var = jnp.mean(yc * yc, axis=-1, keepdims=True)
    return yc * lax.rsqrt(var + GN_EPS)


def _wkv_serial_kernel(m_ref, g_ref, bonus_ref, lnw_ref, lnb_ref, o_ref, s_ref, st_ref, *, nb, nc):
    @pl.when(pl.program_id(0) == 0)
    def _():
        st_ref[...] = jnp.zeros_like(st_ref)

    nbh = nb * N_HEAD
    st = [st_ref[i] for i in range(nbh)]
    ys = []
    for c in range(nc):
        zs = [_dot1(m_ref[i, c, :, :HEAD], st[i]) + m_ref[i, c, :, HEAD:] for i in range(nbh)]
        st = [z[:CHUNK] for z in zs]
        ys += [z[CHUNK:] for z in zs]
    for i in range(nbh):
        st_ref[i] = st[i]

    mus = [jnp.mean(y, axis=-1, keepdims=True) for y in ys]
    ycs = _each(jnp.subtract, ys, mus)
    vrs = [jnp.mean(yc * yc, axis=-1, keepdims=True) for yc in ycs]
    yns = _each(lambda yc, vr: yc * lax.rsqrt(vr + GN_EPS), ycs, vrs)
    for c in range(nc):
        rows = slice(c * CHUNK, (c + 1) * CHUNK)
        for bi in range(nb):
            yn = jnp.concatenate(yns[c * nbh + bi * N_HEAD:c * nbh + (bi + 1) * N_HEAD], axis=-1)
            o_ref[bi, rows, :] = (yn * lnw_ref[...] + lnb_ref[...] + bonus_ref[bi, rows, :]) * g_ref[bi, rows, :]

    @pl.when(pl.program_id(0) == pl.num_programs(0) - 1)
    def _():
        s_ref[...] = st_ref[...]


def wkv_serial(maps, g, bonus, lnw, lnb, nc):
    nbh, nchunk = maps.shape[0], maps.shape[1]
    nb = g.shape[0]
    blk = pl.BlockSpec((nbh, nc, 2 * CHUNK, 2 * HEAD), lambda j: (0, j, 0, 0))
    tok = pl.BlockSpec((nb, nc * CHUNK, D_RWKV), lambda j: (0, j, 0))
    par = pl.BlockSpec((1, D_RWKV), lambda j: (0, 0))
    return pl.pallas_call(
        functools.partial(_wkv_serial_kernel, nb=nb, nc=nc),
        grid=(nchunk // nc,),
        in_specs=[blk, tok, tok, par, par],
        out_specs=[tok, pl.BlockSpec((nbh, HEAD, HEAD), lambda j: (0, 0, 0))],
        out_shape=[jax.ShapeDtypeStruct((nb, nchunk * CHUNK, D_RWKV), f32),
                   jax.ShapeDtypeStruct((nbh, HEAD, HEAD), f32)],
        scratch_shapes=[pltpu.VMEM((nbh, HEAD, HEAD), f32)],
        compiler_params=_params(("arbitrary",), VMEM_LIMIT),
        name="wkv_serial",
    )(maps, g, bonus, lnw, lnb)


def _wkv_step_kernel(s_ref, r_ref, lw_ref, k_ref, kk_ref, b_ref, vc_ref, so_ref, y_ref):
    s = s_ref[...]
    skk = jnp.sum(s * kk_ref[...], axis=-1, keepdims=True)
    s1 = s * jnp.exp(lw_ref[...]) - skk * b_ref[...] + vc_ref[...] * k_ref[...]
    so_ref[...] = s1
    y_ref[...] = jnp.sum(s1 * r_ref[...], axis=-1, keepdims=True)


def wkv_step(s, r, lw, k, kk, b, vcol):
    n = s.shape[0]
    row = pl.BlockSpec((STEP_ROWS, N_HEAD, 1, HEAD), lambda i: (i, 0, 0, 0))
    col = pl.BlockSpec((STEP_ROWS, N_HEAD, HEAD, 1), lambda i: (i, 0, 0, 0))
    mat = pl.BlockSpec((STEP_ROWS, N_HEAD, HEAD, HEAD), lambda i: (i, 0, 0, 0))
    return pl.pallas_call(
        _wkv_step_kernel,
        grid=(n // STEP_ROWS,),
        in_specs=[mat, row, row, row, row, row, col],
        out_specs=[mat, col],
        out_shape=[jax.ShapeDtypeStruct(s.shape, f32), jax.ShapeDtypeStruct((n, N_HEAD, HEAD, 1), f32)],
        compiler_params=_params(("parallel",)),
        name="wkv_step",
    )(s, r, lw, k, kk, b, vcol)


def _post_kernel(y_ref, g_ref, bonus_ref, lnw_ref, lnb_ref, o_ref):
    yn = jnp.concatenate([_group_norm(y_ref[0, h]) for h in range(N_HEAD)], axis=-1)
    o_ref[0] = (yn * lnw_ref[...] + lnb_ref[...] + bonus_ref[0]) * g_ref[0]


def rwkv_post(y, g, bonus, lnw, lnb, tm):
    nb, _, t, _ = y.shape
    hm = pl.BlockSpec((1, N_HEAD, tm, HEAD), lambda i, j: (i, 0, j, 0))
    tok = pl.BlockSpec((1, tm, D_RWKV), lambda i, j: (i, j, 0))
    par = pl.BlockSpec((1, D_RWKV), lambda i, j: (0, 0))
    return pl.pallas_call(
        _post_kernel,
        grid=(nb, t // tm),
        in_specs=[hm, tok, tok, par, par],
        out_specs=tok,
        out_shape=jax.ShapeDtypeStruct((nb, t, D_RWKV), f32),
        compiler_params=_params(("parallel", "parallel"), VMEM_LIMIT),
        name="rwkv_post",
    )(y, g, bonus, lnw, lnb)


def _logf_kernel(u_ref, b_ref, lf_ref, c_ref, *, nblk):
    ii = lax.broadcasted_iota(jnp.int32, (128, 128), 0)
    jj = lax.broadcasted_iota(jnp.int32, (128, 128), 1)
    upper = (ii <= jj).astype(bf16)
    nb = u_ref.shape[0]
    carry = jnp.zeros((nb * N_HEAD, 1), f32)
    for blk in range(nblk):
        sl = slice(128 * blk, 128 * (blk + 1))
        fl = jnp.concatenate([u_ref[bi, sl, :].T[:N_HEAD] for bi in range(nb)], axis=0)
        lf = jax.nn.log_sigmoid(fl + b_ref[...])
        lf_ref[:, sl] = lf
        cs = _dot_onesr(lf, upper) + carry
        c_ref[:, sl] = cs
        carry = cs[:, 127:128]


def logf_cumsum(u3, bias):
    nb, t, _ = u3.shape
    out = jax.ShapeDtypeStruct((nb * N_HEAD, t), f32)
    return pl.pallas_call(
        functools.partial(_logf_kernel, nblk=t // 128),
        grid=(1,),
        in_specs=[pl.BlockSpec((nb, t, 128), lambda i: (0, 0, COL_F // 128)),
                  pl.BlockSpec((nb * N_HEAD, 1), lambda i: (0, 0))],
        out_specs=[pl.BlockSpec((nb * N_HEAD, t), lambda i: (0, 0))] * 2,
        out_shape=[out, out],
        compiler_params=_params(("arbitrary",), VMEM_LIMIT),
        name="logf_cumsum",
    )(u3, bias)


SUM_ROWS = 16


def _fox_kernel(qi_ref, ki_ref, q_ref, k_ref, v_ref, ck_ref, o_ref, m_ref, acc_ref, *, tq, tk, nsub):
    step_id = pl.program_id(2)
    qi = qi_ref[step_id]
    ki = ki_ref[step_id]

    @pl.when(ki == 0)
    def _():
        m_ref[...] = jnp.full_like(m_ref, NEG)
        acc_ref[...] = jnp.zeros_like(acc_ref)

    def step(masked):
        lane_lo = lax.broadcasted_iota(jnp.int32, (tq, 2 * HEAD), 1) < HEAD
        q = q_ref[0] * (HEAD ** -0.5 * LOG2E)
        qes = [jnp.where(lane_lo == (e == 0), q, 0.0).astype(bf16) for e in range(2)]
        ts = tk // nsub
        ckt = ck_ref[0, 0].T

        def first_col(sub):
            return (sub * ts) // 128 * 128 if masked else 0

        def scores(sub):
            rs = slice(sub * ts, (sub + 1) * ts)
            lo = first_col(sub)
            kb = k_ref[0, rs, :].astype(bf16)
            return [_dot(kb, qes[e][lo:, :], NT) for e in range(2)]

        s_cur = scores(0)
        for sub in range(nsub):
            s_next = scores(sub + 1) if sub + 1 < nsub else None
            rs = slice(sub * ts, (sub + 1) * ts)
            lo = first_col(sub)
            vt = v_ref[0, rs, :].T.astype(bf16)
            ones = jnp.ones((SUM_ROWS, ts), bf16)
            ps, alphas = [], []
            for e in range(2):
                s = s_cur[e] - ckt[rs, e:e + 1]
                if masked:
                    row = lax.broadcasted_iota(jnp.int32, s.shape, 0) + sub * ts
                    col = lax.broadcasted_iota(jnp.int32, s.shape, 1) + lo
                    s = jnp.where(row <= col, s, NEG)
                m_old = m_ref[e, :, lo:]
                m_new = jnp.maximum(m_old, jnp.max(s, axis=0, keepdims=True))
                alphas.append(jnp.exp2(m_old - m_new))
                ps.append(jnp.exp2(s - m_new).astype(bf16))
                m_ref[e, :, lo:] = m_new
            for e in range(2):
                vte = jnp.concatenate([vt[HEAD * e:HEAD * (e + 1), :], ones], axis=0)
                acc_ref[e, :, lo:] = alphas[e] * acc_ref[e, :, lo:] + _dot(vte, ps[e])
            s_cur = s_next

    @pl.when(ki < qi)
    def _():
        step(False)

    @pl.when(ki == qi)
    def _():
        step(True)
        o = [acc_ref[e, 0:HEAD, :] / acc_ref[e, HEAD:HEAD + 1, :] for e in range(2)]
        o_ref[0] = jnp.concatenate(o, axis=0).T


def fox_prompt(u3, ck, tq, nsub):
    nb, t, _ = u3.shape
    nq = t // tq
    cq, ckk, cv = COL_Q // 128, (COL_Q + 512) // 128, (COL_Q + 1024) // 128
    pairs = [(i, j) for i in range(nq) for j in range(i + 1)]
    qi_tab = jnp.asarray([p[0] for p in pairs], jnp.int32)
    ki_tab = jnp.asarray([p[1] for p in pairs], jnp.int32)
    return pl.pallas_call(
        functools.partial(_fox_kernel, tq=tq, tk=tq, nsub=nsub),
        grid_spec=pltpu.PrefetchScalarGridSpec(
            num_scalar_prefetch=2,
            grid=(nb, N_HEAD // 2, len(pairs)),
            in_specs=[pl.BlockSpec((1, tq, 128), lambda b, h, s, qt, kt: (b, qt[s], cq + h)),
                      pl.BlockSpec((1, tq, 128), lambda b, h, s, qt, kt: (b, kt[s], ckk + h)),
                      pl.BlockSpec((1, tq, 128), lambda b, h, s, qt, kt: (b, kt[s], cv + h)),
                      pl.BlockSpec((1, 1, 8, tq), lambda b, h, s, qt, kt: (b, h, 0, kt[s]))],
            out_specs=pl.BlockSpec((1, tq, 128), lambda b, h, s, qt, kt: (b, qt[s], h)),
            scratch_shapes=[pltpu.VMEM((2, 1, tq), f32), pltpu.VMEM((2, HEAD + SUM_ROWS, tq), f32)]),
        out_shape=jax.ShapeDtypeStruct((nb, t, 512), f32),
        compiler_params=_params(("parallel", "parallel", "arbitrary"), VMEM_LIMIT),
        name="fox_prompt",
    )(qi_tab, ki_tab, u3, u3, u3, ck)


def _paged_kernel(pt_ref, q_ref, kn_ref, vn_ref, lfn_ref, *refs, g):
    k_refs, v_refs, lf_refs = refs[0:g], refs[g:2 * g], refs[2 * g:3 * g]
    o_ref = refs[3 * g]
    m_ref, l_ref, acc_ref, car_ref = refs[3 * g + 1:]
    step = pl.program_id(1)
    heads = [slice(HEAD * h, HEAD * (h + 1)) for h in range(N_HEAD)]
    q = q_ref[0]

    @pl.when(step == 0)
    def _():
        lane = lax.broadcasted_iota(jnp.int32, acc_ref.shape, 1)
        acc_ref[...] = jnp.where(lane == 0, vn_ref[0], 0.0)
        qk = q * kn_ref[0]
        for h in range(N_HEAD):
            m_ref[h] = jnp.sum(qk[heads[h]], axis=0, keepdims=True)
            l_ref[h] = jnp.ones((1, 1), f32)
        car_ref[...] = lfn_ref[0]

    ii = lax.broadcasted_iota(jnp.int32, (PAGE, PAGE), 0)
    jj = lax.broadcasted_iota(jnp.int32, (PAGE, PAGE), 1)
    later = (ii > jj).astype(bf16)
    carry = car_ref[...]
    biases = []
    for j in range(g):
        lfp = lf_refs[j][0]
        biases.append(_dot_onesr(lfp, later) + carry)
        carry = carry + jnp.sum(lfp, axis=-1, keepdims=True)
    car_ref[...] = carry

    ss = [jnp.concatenate(
        [jnp.sum(k_refs[j][0, heads[h], :] * q[heads[h]], axis=0, keepdims=True) + biases[j][h:h + 1, :]
         for j in range(g)], axis=0) for h in range(N_HEAD)]
    m_olds = [m_ref[h] for h in range(N_HEAD)]
    m_news = [jnp.maximum(m, jnp.max(jnp.max(s, axis=-1, keepdims=True), axis=0, keepdims=True))
              for m, s in zip(m_olds, ss)]
    alphas = [jnp.exp(mo - mn) for mo, mn in zip(m_olds, m_news)]
    ps = [jnp.exp(s - mn) for s, mn in zip(ss, m_news)]
    for h in range(N_HEAD):
        l_ref[h] = alphas[h] * l_ref[h] + jnp.sum(jnp.sum(ps[h], axis=-1, keepdims=True), axis=0, keepdims=True)
        m_ref[h] = m_news[h]
    for h in range(N_HEAD):
        acc = alphas[h] * acc_ref[heads[h], :]
        for j in range(g):
            acc = acc + v_refs[j][0, heads[h], :] * ps[h][j:j + 1, :]
        acc_ref[heads[h], :] = acc

    @pl.when(step == pl.num_programs(1) - 1)
    def _():
        for h in range(N_HEAD):
            o_ref[0, heads[h], :] = jnp.sum(acc_ref[heads[h], :], axis=-1, keepdims=True) / l_ref[h]


def fox_paged(page_table, q, knew, vnew, lfnew, kpool, vpool, lfpool, g):
    nb, npages = page_table.shape
    last = npages - 1
    cur = lambda b, i, pt: (b, 0, 0)
    pool = lambda j: (lambda b, i, pt: (pt[b, last - (i * g + j)], 0, 0))
    col = pl.BlockSpec((1, 512, 1), cur)
    return pl.pallas_call(
        functools.partial(_paged_kernel, g=g),
        grid_spec=pltpu.PrefetchScalarGridSpec(
            num_scalar_prefetch=1,
            grid=(nb, npages // g),
            in_specs=([col, col, col, pl.BlockSpec((1, N_HEAD, 1), cur)]
                      + [pl.BlockSpec((1, 512, PAGE), pool(j)) for j in range(g)]
                      + [pl.BlockSpec((1, 512, PAGE), pool(j)) for j in range(g)]
                      + [pl.BlockSpec((1, N_HEAD, PAGE), pool(j)) for j in range(g)]),
            out_specs=col,
            scratch_shapes=[pltpu.VMEM((N_HEAD, 1, 1), f32), pltpu.VMEM((N_HEAD, 1, 1), f32),
                            pltpu.VMEM((512, PAGE), f32), pltpu.VMEM((N_HEAD, 1), f32)]),
        out_shape=jax.ShapeDtypeStruct((nb, 512, 1), f32),
        compiler_params=_params(("parallel", "arbitrary"), VMEM_LIMIT),
        name="fox_paged",
    )(page_table, q, knew, vnew, lfnew, *([kpool] * g), *([vpool] * g), *([lfpool] * g))


def _merge_kernel(x_ref, or_ref, of_ref, ug_ref, wr_ref, wf_ref, wo_ref, o_ref):
    br = _dot(or_ref[...].astype(bf16), wr_ref[...])
    bf = _dot(of_ref[...].astype(bf16), wf_ref[...])
    ug = ug_ref[...]
    merged = jax.nn.sigmoid(ug[:, :D_MODEL]) * br + jax.nn.sigmoid(ug[:, D_MODEL:]) * bf
    o_ref[...] = x_ref[...] + _dot(merged.astype(bf16), wo_ref[...])


def merge_out(x, o_r, o_f, u, wr, wf, wo, tm):
    m = x.shape[0]
    row = lambda i: (i, 0)
    cst = lambda i: (0, 0)
    return pl.pallas_call(
        _merge_kernel,
        grid=(m // tm,),
        in_specs=[pl.BlockSpec((tm, D_MODEL), row), pl.BlockSpec((tm, 512), row), pl.BlockSpec((tm, 512), row),
                  pl.BlockSpec((tm, 2 * D_MODEL), lambda i: (i, COL_G // (2 * D_MODEL))),
                  pl.BlockSpec((512, D_MODEL), cst), pl.BlockSpec((512, D_MODEL), cst),
                  pl.BlockSpec((D_MODEL, D_MODEL), cst)],
        out_specs=pl.BlockSpec((tm, D_MODEL), row),
        out_shape=jax.ShapeDtypeStruct((m, D_MODEL), f32),
        compiler_params=_params(("parallel",), VMEM_LIMIT),
        name="merge_out",
    )(x, o_r, o_f, u, wr, wf, wo)


def _ffn_kernel(x_ref, g2_ref, gf_ref, wu_ref, wd_ref, o_ref, h_ref, acc_ref):
    j = pl.program_id(1)

    @pl.when(j == 0)
    def _():
        x = x_ref[...]
        ms = jnp.mean(x * x, axis=-1, keepdims=True)
        h_ref[...] = (x * lax.rsqrt(ms + RMS_EPS) * g2_ref[...]).astype(bf16)
        acc_ref[...] = jnp.zeros_like(acc_ref)

    hid = jnp.maximum(_dot(h_ref[...], wu_ref[...]), 0.0)
    acc_ref[...] += _dot((hid * hid).astype(bf16), wd_ref[...])

    @pl.when(j == pl.num_programs(1) - 1)
    def _():
        x2 = x_ref[...] + acc_ref[...]
        ms = jnp.mean(x2 * x2, axis=-1, keepdims=True)
        o_ref[...] = x2 * lax.rsqrt(ms + RMS_EPS) * gf_ref[...]


def ffn_final(x, g2, gf, wu, wd, tm, tf):
    m = x.shape[0]
    return pl.pallas_call(
        _ffn_kernel,
        grid=(m // tm, D_FF // tf),
        in_specs=[pl.BlockSpec((tm, D_MODEL), lambda i, j: (i, 0)),
                  pl.BlockSpec((1, D_MODEL), lambda i, j: (0, 0)), pl.BlockSpec((1, D_MODEL), lambda i, j: (0, 0)),
                  pl.BlockSpec((D_MODEL, tf), lambda i, j: (0, j)), pl.BlockSpec((tf, D_MODEL), lambda i, j: (j, 0))],
        out_specs=pl.BlockSpec((tm, D_MODEL), lambda i, j: (i, 0)),
        out_shape=jax.ShapeDtypeStruct((m, D_MODEL), f32),
        scratch_shapes=[pltpu.VMEM((tm, D_MODEL), bf16), pltpu.VMEM((tm, D_MODEL), f32)],
        compiler_params=_params(("parallel", "arbitrary"), VMEM_LIMIT),
        name="ffn_final",
    )(x, g2, gf, wu, wd)


def kernel(x_prompt, x_sample, state_shift, state_wkv, cache_k, cache_v, cache_logf, page_table, meta_tokens,
           norm_mix, w_in, mu_shift, w0, w_decay_up, a0, w_a_up, w_g_up, k_k, k_a, r_k, ln_x_w, ln_x_b, b_forget,
           w_br_rwkv, w_br_fox, w_out, norm_ffn, w_ffn_up, w_ffn_down, norm_final):
    assert norm_mix.shape[0] == 1, "single layer"
    nb, seq, _ = x_prompt.shape
    ns = x_sample.shape[0]
    tp = FRONT_PAD + N_META + seq
    tlen = N_META + seq

    wi = w_in[0]
    wi = jnp.concatenate([wi[:, 0:1792], wi[:, 3328:3336], jnp.zeros((D_MODEL, 248), f32),
                          wi[:, 3336:5384], wi[:, 1792:3328]], axis=1).astype(bf16)
    row2 = lambda a: a.reshape(1, -1)
    lane_head = jnp.arange(512, dtype=jnp.int32) // HEAD
    pr = dict(mu=row2(mu_shift[0]), w0=row2(w0[0]), wd=w_decay_up[0], a0=row2(a0[0]), wa=w_a_up[0], wg=w_g_up[0],
              k_k=row2(k_k[0]), k_a=row2(k_a[0]), r_k=row2(r_k[0]),
              bd=(lane_head[:, None] == lane_head[None, :]).astype(bf16))
    lnw, lnb = row2(ln_x_w[0]), row2(ln_x_b[0])
    g_mix, g_ffn, g_fin = row2(norm_mix[0]), row2(norm_ffn[0]), row2(norm_final)
    wr, wf, wo = w_br_rwkv[0].astype(bf16), w_br_fox[0].astype(bf16), w_out[0].astype(bf16)
    wu, wdn = w_ffn_up[0].astype(bf16), w_ffn_down[0].astype(bf16)

    meta = jnp.broadcast_to(meta_tokens[None], (nb, N_META, D_MODEL))
    xp = jnp.concatenate([jnp.zeros((nb, FRONT_PAD, D_MODEL), f32), meta, x_prompt], axis=1)
    xp2 = xp.reshape(nb * tp, D_MODEL)
    u2 = norm_matmul(xp2, g_mix, wi, TM_PROJ, TN_PROJ)
    u3 = u2.reshape(nb, tp, D_INP)
    us2 = norm_matmul(x_sample.reshape(ns, D_MODEL), g_mix, wi, ns, TN_PROJ)

    r, lw, k2, v, kkn, b, g, bonus = rwkv_prep(u3, None, pr, TM_PREP)
    maps = wkv_chunks(r, lw, k2, v, kkn, b, CHUNKS_PER_STEP)
    maps = maps.reshape(nb * N_HEAD, tp // CHUNK, 2 * CHUNK, 2 * HEAD)
    o_r, s_kv = wkv_serial(maps, g, bonus, lnw, lnb, CHUNKS_PER_STEP)
    new_wkv_p = jnp.swapaxes(s_kv, 1, 2).reshape(1, nb, N_HEAD, HEAD, HEAD)

    us3 = us2.reshape(1, ns, D_INP)
    sp = rwkv_prep(us3, state_shift[0].reshape(1, ns, D_SHIFT), pr, ns)
    rs, lws, ks, vs, kks, bs, gs, bonus_s = sp
    rowify = lambda a: a.reshape(ns, N_HEAD, 1, HEAD)
    s_new, ys = wkv_step(state_wkv[0], rowify(rs), rowify(lws), rowify(ks), rowify(kks), rowify(bs),
                         rowify(vs).reshape(ns, N_HEAD, HEAD, 1))
    ys_hm = jnp.transpose(ys.reshape(ns, N_HEAD, HEAD), (1, 0, 2))[None]
    o_r_s = rwkv_post(ys_hm, gs, bonus_s, lnw, lnb, ns)[0]

    bias_rows = jnp.tile(b_forget[0], nb).reshape(nb * N_HEAD, 1)
    lf_t, c_t = logf_cumsum(u3, bias_rows)
    pad_key = jnp.arange(tp, dtype=jnp.int32)[None, :] < FRONT_PAD
    ck = jnp.where(pad_key, -NEG, c_t * LOG2E).reshape(nb, N_HEAD // 2, 2, tp)
    o_f = fox_prompt(u3, jnp.pad(ck, ((0, 0), (0, 0), (0, 6), (0, 0))), TQ_FOX, NSUB_FOX)

    qs = us2[:, COL_Q:COL_Q + 512] * (HEAD ** -0.5)
    k_s = us2[:, COL_Q + 512:COL_Q + 1024]
    v_s = us2[:, COL_Q + 1024:COL_Q + 1536]
    lf_s = jax.nn.log_sigmoid(us2[:, COL_F:COL_F + N_HEAD] + b_forget[0][None, :])
    n_pool = cache_k.shape[1]
    kpool = jnp.transpose(cache_k[0], (0, 2, 3, 1)).reshape(n_pool, 512, PAGE)
    vpool = jnp.transpose(cache_v[0], (0, 2, 3, 1)).reshape(n_pool, 512, PAGE)
    o_f_s = fox_paged(page_table, qs.reshape(ns, 512, 1), k_s.reshape(ns, 512, 1), v_s.reshape(ns, 512, 1),
                      lf_s.reshape(ns, N_HEAD, 1), kpool, vpool, jnp.swapaxes(cache_logf[0], 1, 2),
                      PAGES_PER_STEP).reshape(ns, 512)

    x1 = merge_out(xp2, o_r.reshape(nb * tp, 512), o_f.reshape(nb * tp, 512), u2, wr, wf, wo, TM_MERGE)
    yp = ffn_final(x1, g_ffn, g_fin, wu, wdn, TM_FFN, TF_FFN).reshape(nb, tp, D_MODEL)
    x1s = merge_out(x_sample.reshape(ns, D_MODEL), o_r_s, o_f_s, us2, wr, wf, wo, ns)
    ysamp = ffn_final(x1s, g_ffn, g_fin, wu, wdn, ns, TF_FFN)

    y_prompt = yp[:, ROW0:]
    y_sample = ysamp.reshape(ns, 1, D_MODEL)
    new_shift_p = u3[:, tp - 1, :D_SHIFT][None]
    k_p = u3[:, FRONT_PAD:, COL_Q + 512:COL_Q + 1024].reshape(1, nb, tlen, N_HEAD, HEAD)
    v_p = u3[:, FRONT_PAD:, COL_Q + 1024:COL_Q + 1536].reshape(1, nb, tlen, N_HEAD, HEAD)
    lf_p = jnp.transpose(lf_t.reshape(nb, N_HEAD, tp), (0, 2, 1))[:, FRONT_PAD:][None]
    return (y_prompt, y_sample, new_shift_p, new_wkv_p, k_p, v_p, lf_p,
            us2[:, :D_SHIFT][None], s_new[None], k_s.reshape(1, ns, 1, N_HEAD, HEAD),
            v_s.reshape(1, ns, 1, N_HEAD, HEAD), lf_s.reshape(1, ns, 1, N_HEAD))
```

```python
import functools

import jax
import jax.numpy as jnp
from jax import lax
from jax.experimental import pallas as pl
from jax.experimental.pallas import tpu as pltpu

f32 = jnp.float32
bf16 = jnp.bfloat16

D_MODEL = 1024
N_META = 16
D_RWKV = 512
HEAD = 64
N_HEAD = 8
D_SHIFT = 1792
D_FF = 4096
RMS_EPS = 1e-6
GN_EPS = 64e-5
PAGE = 128

FRONT_PAD = 112
ROW0 = FRONT_PAD + N_META
CHUNK = 64
NEG = -1e30
LOG2E = 1.4426950408889634

COL_R = 0
COL_F = 1792
COL_G = 2048
COL_Q = 4096
D_INP = 5632

VMEM_LIMIT = 56 * 1024 * 1024

TM_PROJ, TN_PROJ = 1664, 1408
TM_PREP = 320
CHUNKS_PER_STEP = 5
TQ_FOX, NSUB_FOX = 1664, 8
PAGES_PER_STEP = 32
TM_MERGE = 640
TM_FFN, TF_FFN = 1664, 512
STEP_ROWS = 4

NN = (((1,), (0,)), ((), ()))
NT = (((1,), (1,)), ((), ()))
TN = (((0,), (0,)), ((), ()))


def _dot(a, b, dims=NN):
    return lax.dot_general(a, b, dims, preferred_element_type=f32)


def _split2(x):
    hi = x.astype(bf16)
    lo = (x - hi.astype(f32)).astype(bf16)
    return hi, lo


def _split3(x):
    hi = x.astype(bf16)
    r1 = x - hi.astype(f32)
    mid = r1.astype(bf16)
    lo = (r1 - mid.astype(f32)).astype(bf16)
    return hi, mid, lo


def _dot3(a, b, dims=NN):
    ah, al = _split2(a)
    bh, bl = _split2(b)
    return _dot(ah, bh, dims) + (_dot(ah, bl, dims) + _dot(al, bh, dims))


def _dot1(a, b, dims=NN):
    return _dot(a.astype(bf16), b.astype(bf16), dims)


def _dot_onesr(x, ones_bf16, dims=NN):
    hi, mid, lo = _split3(x)
    return _dot(hi, ones_bf16, dims) + (_dot(mid, ones_bf16, dims) + _dot(lo, ones_bf16, dims))


def _segsum(x, ones_bf16):
    hi, lo = _split2(x)
    return _dot(hi, ones_bf16) + _dot(lo, ones_bf16)


def _dot_onesl(ones_bf16, x, dims=NN):
    hi, mid, lo = _split3(x)
    return _dot(ones_bf16, hi, dims) + (_dot(ones_bf16, mid, dims) + _dot(ones_bf16, lo, dims))


def _params(sem, vmem=None):
    return pltpu.CompilerParams(dimension_semantics=sem, vmem_limit_bytes=vmem)


def _rms(x, g):
    ms = jnp.mean(x * x, axis=-1, keepdims=True)
    return x * lax.rsqrt(ms + RMS_EPS) * g


def _norm_matmul_kernel(x_ref, g_ref, w_ref, o_ref, h_ref):
    @pl.when(pl.program_id(1) == 0)
    def _():
        h_ref[...] = _rms(x_ref[...], g_ref[...]).astype(bf16)

    o_ref[...] = _dot(h_ref[...], w_ref[...])


def norm_matmul(x, g, w, tm, tn):
    m, d = x.shape
    n = w.shape[1]
    return pl.pallas_call(
        _norm_matmul_kernel,
        grid=(m // tm, n // tn),
        in_specs=[pl.BlockSpec((tm, d), lambda i, j: (i, 0)),
                  pl.BlockSpec((1, d), lambda i, j: (0, 0)),
                  pl.BlockSpec((d, tn), lambda i, j: (0, j))],
        out_specs=pl.BlockSpec((tm, tn), lambda i, j: (i, j)),
        out_shape=jax.ShapeDtypeStruct((m, n), f32),
        scratch_shapes=[pltpu.VMEM((tm, d), bf16)],
        compiler_params=_params(("parallel", "arbitrary"), VMEM_LIMIT),
        name="norm_matmul",
    )(x, g, w)


def _seq_tile_spec(tm, d, tiles_per_seq):
    def index(i, *_):
        start = jnp.maximum((i % tiles_per_seq) * tm - ROW0, 0)
        return (i // tiles_per_seq, pl.multiple_of(start, 128), 0)
    return pl.BlockSpec((pl.Element(1), pl.Element(tm), pl.Element(d)), index)


def _padded_rows(x_ref, head_ref, first):
    x = x_ref[0]
    rowi = lax.broadcasted_iota(jnp.int32, x.shape, 0)
    head = jnp.concatenate([head_ref[...], jnp.zeros((x.shape[0] - ROW0, x.shape[1]), f32)], axis=0)
    return jnp.where(first, jnp.where(rowi < ROW0, head, pltpu.roll(x, ROW0, 0)), x)


def _norm_matmul_seq_kernel(x_ref, head_ref, g_ref, w_ref, o_ref, h_ref, *, tiles_per_seq):
    @pl.when(pl.program_id(1) == 0)
    def _():
        x = _padded_rows(x_ref, head_ref, pl.program_id(0) % tiles_per_seq == 0)
        h_ref[...] = _rms(x, g_ref[...]).astype(bf16)

    o_ref[...] = _dot(h_ref[...], w_ref[...])


def norm_matmul_seq(x3, head, g, w, tm, tn):
    nb, seq, d = x3.shape
    tps = (seq + ROW0) // tm
    n = w.shape[1]
    return pl.pallas_call(
        functools.partial(_norm_matmul_seq_kernel, tiles_per_seq=tps),
        grid=(nb * tps, n // tn),
        in_specs=[_seq_tile_spec(tm, d, tps),
                  pl.BlockSpec((ROW0, d), lambda i, j: (0, 0)),
                  pl.BlockSpec((1, d), lambda i, j: (0, 0)),
                  pl.BlockSpec((d, tn), lambda i, j: (0, j))],
        out_specs=pl.BlockSpec((tm, tn), lambda i, j: (i, j)),
        out_shape=jax.ShapeDtypeStruct((nb * tps * tm, n), f32),
        scratch_shapes=[pltpu.VMEM((tm, d), bf16)],
        compiler_params=_params(("parallel", "arbitrary"), VMEM_LIMIT),
        name="norm_matmul_seq",
    )(x3, head, g, w)


def _prep_kernel(u_ref, up_ref, mu_ref, w0_ref, wd_ref, a0_ref, wa_ref, wg_ref, kk_ref, ka_ref, rk_ref,
                 bd_ref, r_o, lw_o, k_o, v_o, kkn_o, b_o, g_o, bonus_o, *, prev_is_tail):
    u = u_ref[0]
    if prev_is_tail:
        first = jnp.where(pl.program_id(1) == 0, 0.0, up_ref[0, 7:8, :])
        rowi = lax.broadcasted_iota(jnp.int32, u.shape, 0)
        up = jnp.where(rowi == 0, first, pltpu.roll(u, 1, 0))
    else:
        up = up_ref[0]
    us = u + mu_ref[...] * (up - u)
    r = us[:, 0:512]
    k = us[:, 512:1024]
    v = us[:, 1024:1536]
    zw = us[:, 1536:1600]
    za = us[:, 1600:1664]
    zg = us[:, 1664:1792]
    bd = bd_ref[...]

    z = -(w0_ref[...] + _dot3(jnp.tanh(zw), wd_ref[...]))
    w_raw = -(jnp.maximum(z, 0.0) + jnp.log1p(jnp.exp(-jnp.abs(z)))) - 0.5
    lw = -jnp.exp(w_raw)
    a = jax.nn.sigmoid(a0_ref[...] + _dot3(za, wa_ref[...]))
    g = _dot3(jax.nn.sigmoid(zg), wg_ref[...])
    kk = k * kk_ref[...]
    ss = _segsum(kk * kk, bd)
    kkn = kk / jnp.maximum(jnp.sqrt(ss), 1e-12)
    k2 = k * (1.0 + (a - 1.0) * ka_ref[...])
    b = kkn * a
    bonus = _segsum(r * k2 * rk_ref[...], bd) * v

    r_o[0] = r
    lw_o[0] = lw
    k_o[0] = k2
    v_o[0] = v
    kkn_o[0] = kkn
    b_o[0] = b
    g_o[0] = g
    bonus_o[0] = bonus


def rwkv_prep(u3, up3, pr, tm):
    nb, t = u3.shape[0], u3.shape[1]
    row = lambda i, j: (i, j, 0)
    if up3 is None:
        up_arr = u3
        up_spec = pl.BlockSpec((1, 8, D_SHIFT), lambda i, j: (i, jnp.maximum(j * (tm // 8) - 1, 0), 0))
    else:
        up_arr = up3
        up_spec = pl.BlockSpec((1, tm, D_SHIFT), row)
    cst = lambda i, j: (0, 0)
    out = jax.ShapeDtypeStruct((nb, t, D_RWKV), f32)
    out_spec = pl.BlockSpec((1, tm, D_RWKV), row)
    vec = lambda n: pl.BlockSpec((1, n), cst)
    return pl.pallas_call(
        functools.partial(_prep_kernel, prev_is_tail=up3 is None),
        grid=(nb, t // tm),
        in_specs=[pl.BlockSpec((1, tm, D_SHIFT), row), up_spec,
                  vec(D_SHIFT), vec(512), pl.BlockSpec((64, 512), cst), vec(512), pl.BlockSpec((64, 512), cst),
                  pl.BlockSpec((128, 512), cst), vec(512), vec(512), vec(512), pl.BlockSpec((512, 512), cst)],
        out_specs=[out_spec] * 8,
        out_shape=[out] * 8,
        compiler_params=_params(("parallel", "parallel"), VMEM_LIMIT),
        name="rwkv_prep",
    )(u3, up_arr, pr["mu"], pr["w0"], pr["wd"], pr["a0"], pr["wa"], pr["wg"], pr["k_k"], pr["k_a"], pr["r_k"],
      pr["bd"])


def _each(f, *lists):
    return [f(*xs) for xs in zip(*lists)]


def _wkv_chunk_kernel(r_ref, lw_ref, k_ref, v_ref, kk_ref, b_ref, m_ref, *, nc):
    ii = lax.broadcasted_iota(jnp.int32, (CHUNK, CHUNK), 0)
    jj = lax.broadcasted_iota(jnp.int32, (CHUNK, CHUNK), 1)
    incl = jj <= ii
    strict = jj < ii
    ones_incl = incl.astype(bf16)
    same_blk = (ii >> 4) == (jj >> 4)
    eye = (ii == jj).astype(f32)
    units = [(c, h) for c in range(nc) for h in range(N_HEAD)]
    tile = lambda ref: [ref[0, c * CHUNK:(c + 1) * CHUNK, h * HEAD:(h + 1) * HEAD] for c, h in units]
    r, lw, k, v, kk, b = (tile(ref) for ref in (r_ref, lw_ref, k_ref, v_ref, kk_ref, b_ref))

    gcum = _each(lambda x: _dot_onesl(ones_incl, x), lw)
    gend = _each(lambda x: x[CHUNK - 1:CHUNK, :], gcum)
    kkt = _each(lambda x, gc, l: x * jnp.exp(gc - l), kk, gcum, lw)
    rt = _each(lambda x, gc: x * jnp.exp(gc), r, gcum)
    em = _each(lambda gc: jnp.exp(-gc), gcum)
    kh = _each(jnp.multiply, k, em)
    bh = _each(jnp.multiply, b, em)
    ec = _each(lambda ge, gc: jnp.exp(ge - gc), gend, gcum)
    kg = _each(jnp.multiply, k, ec)
    bg = _each(jnp.multiply, b, ec)

    lhs = _each(lambda x, y: jnp.concatenate([x, y], axis=0), kkt, rt)
    ab = _each(lambda x, y: _dot1(x, y, NT), lhs, bh)
    ak = _each(lambda x, y: _dot1(x, y, NT), lhs, kh)
    a_kb = _each(lambda x: jnp.where(strict, x[:CHUNK], 0.0), ab)
    a_rb = _each(lambda x: jnp.where(incl, x[CHUNK:], 0.0), ab)
    a_kk = _each(lambda x: jnp.where(strict, x[:CHUNK], 0.0), ak)
    a_rk = _each(lambda x: jnp.where(incl, x[CHUNK:], 0.0), ak)

    dg = _each(lambda x: jnp.where(same_blk, x, 0.0), a_kb)
    lo = _each(jnp.subtract, a_kb, dg)
    n1 = _each(jnp.negative, dg)
    n2 = _each(lambda x: _dot1(x, x), n1)
    n4 = _each(lambda x: _dot1(x, x), n2)
    n8 = _each(lambda x: _dot1(x, x), n4)
    t12 = _each(lambda x, y: _dot1(eye + x, eye + y), n1, n2)
    t48 = _each(lambda x, y: _dot1(eye + x, eye + y), n4, n8)
    td = _each(_dot1, t12, t48)
    x1 = _each(lambda x, y: -_dot1(x, y), td, lo)
    x2 = _each(lambda x: _dot1(x, x), x1)
    xx = _each(lambda x, y: _dot1(eye + x, eye + y), x1, x2)
    tinv = _each(_dot1, xx, td)

    akv = _each(_dot1, a_kk, v)
    w1u = _each(lambda t, x, y: _dot1(t, jnp.concatenate([x, y], axis=1)), tinv, kkt, akv)
    z = _each(lambda x, y, w: _dot1(jnp.concatenate([x.T, y], axis=0), w), bg, a_rb, w1u)
    kv = _each(lambda x, y, w: _dot1(jnp.concatenate([x.T, y], axis=0), w), kg, a_rk, v)
    for i, (c, h) in enumerate(units):
        base = jnp.concatenate([eye * jnp.exp(gend[i]), rt[i]], axis=0)
        m_ref[0, h, c] = jnp.concatenate([base, kv[i]], axis=1) - z[i]


def wkv_chunks(r, lw, k, v, kk, b, nc):
    nb, t, _ = r.shape
    nchunk = t // CHUNK
    in_spec = pl.BlockSpec((1, nc * CHUNK, D_RWKV), lambda i, j: (i, j, 0))
    out_spec = pl.BlockSpec((1, N_HEAD, nc, 2 * CHUNK, 2 * HEAD), lambda i, j: (i, 0, j, 0, 0))
    out = jax.ShapeDtypeStruct((nb, N_HEAD, nchunk, 2 * CHUNK, 2 * HEAD), f32)
    return pl.pallas_call(
        functools.partial(_wkv_chunk_kernel, nc=nc),
        grid=(nb, nchunk // nc),
        in_specs=[in_spec] * 6,
        out_specs=out_spec,
        out_shape=out,
        compiler_params=_params(("parallel", "parallel"), VMEM_LIMIT),
        name="wkv_chunks",
    )(r, lw, k, v, kk, b)


def _group_norm(y):
    mu = jnp.mean(y, axis=-1, keepdims=True)
    yc = y - mu
    var = jnp.mean(yc * yc, axis=-1, keepdims=True)
    return yc * lax.rsqrt(var + GN_EPS)


def _wkv_serial_kernel(m_ref, g_ref, bonus_ref, lnw_ref, lnb_ref, o_ref, s_ref, st_ref, *, nb, nc):
    @pl.when(pl.program_id(0) == 0)
    def _():
        st_ref[...] = jnp.zeros_like(st_ref)

    nbh = nb * N_HEAD
    st = [st_ref[i] for i in range(nbh)]
    ys = []
    for c in range(nc):
        zs = [_dot1(m_ref[i, c, :, :HEAD], st[i]) + m_ref[i, c, :, HEAD:] for i in range(nbh)]
        st = [z[:CHUNK] for z in zs]
        ys += [z[CHUNK:] for z in zs]
    for i in range(nbh):
        st_ref[i] = st[i]

    mus = [jnp.mean(y, axis=-1, keepdims=True) for y in ys]
    ycs = _each(jnp.subtract, ys, mus)
    vrs = [jnp.mean(yc * yc, axis=-1, keepdims=True) for yc in ycs]
    yns = _each(lambda yc, vr: yc * lax.rsqrt(vr + GN_EPS), ycs, vrs)
    for c in range(nc):
        rows = slice(c * CHUNK, (c + 1) * CHUNK)
        for bi in range(nb):
            yn = jnp.concatenate(yns[c * nbh + bi * N_HEAD:c * nbh + (bi + 1) * N_HEAD], axis=-1)
            o_ref[bi, rows, :] = (yn * lnw_ref[...] + lnb_ref[...] + bonus_ref[bi, rows, :]) * g_ref[bi, rows, :]

    @pl.when(pl.program_id(0) == pl.num_programs(0) - 1)
    def _():
        s_ref[...] = st_ref[...]


def wkv_serial(maps, g, bonus, lnw, lnb, nc):
    nbh, nchunk = maps.shape[0], maps.shape[1]
    nb = g.shape[0]
    blk = pl.BlockSpec((nbh, nc, 2 * CHUNK, 2 * HEAD), lambda j: (0, j, 0, 0))
    tok = pl.BlockSpec((nb, nc * CHUNK, D_RWKV), lambda j: (0, j, 0))
    par = pl.BlockSpec((1, D_RWKV), lambda j: (0, 0))
    return pl.pallas_call(
        functools.partial(_wkv_serial_kernel, nb=nb, nc=nc),
        grid=(nchunk // nc,),
        in_specs=[blk, tok, tok, par, par],
        out_specs=[tok, pl.BlockSpec((nbh, HEAD, HEAD), lambda j: (0, 0, 0))],
        out_shape=[jax.ShapeDtypeStruct((nb, nchunk * CHUNK, D_RWKV), f32),
                   jax.ShapeDtypeStruct((nbh, HEAD, HEAD), f32)],
        scratch_shapes=[pltpu.VMEM((nbh, HEAD, HEAD), f32)],
        compiler_params=_params(("arbitrary",), VMEM_LIMIT),
        name="wkv_serial",
    )(maps, g, bonus, lnw, lnb)


def _wkv_step_kernel(s_ref, r_ref, lw_ref, k_ref, kk_ref, b_ref, vc_ref, so_ref, y_ref):
    s = s_ref[...]
    skk = jnp.sum(s * kk_ref[...], axis=-1, keepdims=True)
    s1 = s * jnp.exp(lw_ref[...]) - skk * b_ref[...] + vc_ref[...] * k_ref[...]
    so_ref[...] = s1
    y_ref[...] = jnp.sum(s1 * r_ref[...], axis=-1, keepdims=True)


def wkv_step(s, r, lw, k, kk, b, vcol):
    n = s.shape[0]
    row = pl.BlockSpec((STEP_ROWS, N_HEAD, 1, HEAD), lambda i: (i, 0, 0, 0))
    col = pl.BlockSpec((STEP_ROWS, N_HEAD, HEAD, 1), lambda i: (i, 0, 0, 0))
    mat = pl.BlockSpec((STEP_ROWS, N_HEAD, HEAD, HEAD), lambda i: (i, 0, 0, 0))
    return pl.pallas_call(
        _wkv_step_kernel,
        grid=(n // STEP_ROWS,),
        in_specs=[mat, row, row, row, row, row, col],
        out_specs=[mat, col],
        out_shape=[jax.ShapeDtypeStruct(s.shape, f32), jax.ShapeDtypeStruct((n, N_HEAD, HEAD, 1), f32)],
        compiler_params=_params(("parallel",)),
        name="wkv_step",
    )(s, r, lw, k, kk, b, vcol)


def _post_kernel(y_ref, g_ref, bonus_ref, lnw_ref, lnb_ref, o_ref):
    yn = jnp.concatenate([_group_norm(y_ref[0, h]) for h in range(N_HEAD)], axis=-1)
    o_ref[0] = (yn * lnw_ref[...] + lnb_ref[...] + bonus_ref[0]) * g_ref[0]


def rwkv_post(y, g, bonus, lnw, lnb, tm):
    nb, _, t, _ = y.shape
    hm = pl.BlockSpec((1, N_HEAD, tm, HEAD), lambda i, j: (i, 0, j, 0))
    tok = pl.BlockSpec((1, tm, D_RWKV), lambda i, j: (i, j, 0))
    par = pl.BlockSpec((1, D_RWKV), lambda i, j: (0, 0))
    return pl.pallas_call(
        _post_kernel,
        grid=(nb, t // tm),
        in_specs=[hm, tok, tok, par, par],
        out_specs=tok,
        out_shape=jax.ShapeDtypeStruct((nb, t, D_RWKV), f32),
        compiler_params=_params(("parallel", "parallel"), VMEM_LIMIT),
        name="rwkv_post",
    )(y, g, bonus, lnw, lnb)


def _logf_kernel(u_ref, b_ref, lf_ref, c_ref, *, nblk):
    ii = lax.broadcasted_iota(jnp.int32, (128, 128), 0)
    jj = lax.broadcasted_iota(jnp.int32, (128, 128), 1)
    upper = (ii <= jj).astype(bf16)
    nb = u_ref.shape[0]
    carry = jnp.zeros((nb * N_HEAD, 1), f32)
    for blk in range(nblk):
        sl = slice(128 * blk, 128 * (blk + 1))
        fl = jnp.concatenate([u_ref[bi, sl, :].T[:N_HEAD] for bi in range(nb)], axis=0)
        lf = jax.nn.log_sigmoid(fl + b_ref[...])
        lf_ref[:, sl] = lf
        cs = _dot_onesr(lf, upper) + carry
        c_ref[:, sl] = cs
        carry = cs[:, 127:128]


def logf_cumsum(u3, bias):
    nb, t, _ = u3.shape
    out = jax.ShapeDtypeStruct((nb * N_HEAD, t), f32)
    return pl.pallas_call(
        functools.partial(_logf_kernel, nblk=t // 128),
        grid=(1,),
        in_specs=[pl.BlockSpec((nb, t, 128), lambda i: (0, 0, COL_F // 128)),
                  pl.BlockSpec((nb * N_HEAD, 1), lambda i: (0, 0))],
        out_specs=[pl.BlockSpec((nb * N_HEAD, t), lambda i: (0, 0))] * 2,
        out_shape=[out, out],
        compiler_params=_params(("arbitrary",), VMEM_LIMIT),
        name="logf_cumsum",
    )(u3, bias)


SUM_ROWS = 16


def _fox_kernel(qi_ref, ki_ref, q_ref, k_ref, v_ref, ck_ref, o_ref, m_ref, acc_ref, *, tq, tk, nsub):
    step_id = pl.program_id(2)
    qi = qi_ref[step_id]
    ki = ki_ref[step_id]

    @pl.when(ki == 0)
    def _():
        m_ref[...] = jnp.full_like(m_ref, NEG)
        acc_ref[...] = jnp.zeros_like(acc_ref)

    def step(masked):
        lane_lo = lax.broadcasted_iota(jnp.int32, (tq, 2 * HEAD), 1) < HEAD
        q = q_ref[0] * (HEAD ** -0.5 * LOG2E)
        qes = [jnp.where(lane_lo == (e == 0), q, 0.0).astype(bf16) for e in range(2)]
        ts = tk // nsub
        ckt = ck_ref[0, 0].T

        def first_col(sub):
            return (sub * ts) // 128 * 128 if masked else 0

        def scores(sub):
            rs = slice(sub * ts, (sub + 1) * ts)
            lo = first_col(sub)
            kb = k_ref[0, rs, :].astype(bf16)
            return [_dot(kb, qes[e][lo:, :], NT) for e in range(2)]

        s_cur = scores(0)
        for sub in range(nsub):
            s_next = scores(sub + 1) if sub + 1 < nsub else None
            rs = slice(sub * ts, (sub + 1) * ts)
            lo = first_col(sub)
            vt = v_ref[0, rs, :].T.astype(bf16)
            ones = jnp.ones((SUM_ROWS, ts), bf16)
            ps, alphas = [], []
            for e in range(2):
                s = s_cur[e] - ckt[rs, e:e + 1]
                if masked:
                    row = lax.broadcasted_iota(jnp.int32, s.shape, 0) + sub * ts
                    col = lax.broadcasted_iota(jnp.int32, s.shape, 1) + lo
                    s = jnp.where(row <= col, s, NEG)
                m_old = m_ref[e, :, lo:]
                m_new = jnp.maximum(m_old, jnp.max(s, axis=0, keepdims=True))
                alphas.append(jnp.exp2(m_old - m_new))
                ps.append(jnp.exp2(s - m_new).astype(bf16))
                m_ref[e, :, lo:] = m_new
            for e in range(2):
                vte = jnp.concatenate([vt[HEAD * e:HEAD * (e + 1), :], ones], axis=0)
                acc_ref[e, :, lo:] = alphas[e] * acc_ref[e, :, lo:] + _dot(vte, ps[e])
            s_cur = s_next

    @pl.when(ki < qi)
    def _():
        step(False)

    @pl.when(ki == qi)
    def _():
        step(True)
        o = [acc_ref[e, 0:HEAD, :] / acc_ref[e, HEAD:HEAD + 1, :] for e in range(2)]
        o_ref[0] = jnp.concatenate(o, axis=0).T


def fox_prompt(u3, ck, tq, nsub):
    nb, t, _ = u3.shape
    nq = t // tq
    cq, ckk, cv = COL_Q // 128, (COL_Q + 512) // 128, (COL_Q + 1024) // 128
    pairs = [(i, j) for i in range(nq) for j in range(i + 1)]
    qi_tab = jnp.asarray([p[0] for p in pairs], jnp.int32)
    ki_tab = jnp.asarray([p[1] for p in pairs], jnp.int32)
    return pl.pallas_call(
        functools.partial(_fox_kernel, tq=tq, tk=tq, nsub=nsub),
        grid_spec=pltpu.PrefetchScalarGridSpec(
            num_scalar_prefetch=2,
            grid=(nb, N_HEAD // 2, len(pairs)),
            in_specs=[pl.BlockSpec((1, tq, 128), lambda b, h, s, qt, kt: (b, qt[s], cq + h)),
                      pl.BlockSpec((1, tq, 128), lambda b, h, s, qt, kt: (b, kt[s], ckk + h)),
                      pl.BlockSpec((1, tq, 128), lambda b, h, s, qt, kt: (b, kt[s], cv + h)),
                      pl.BlockSpec((1, 1, 8, tq), lambda b, h, s, qt, kt: (b, h, 0, kt[s]))],
            out_specs=pl.BlockSpec((1, tq, 128), lambda b, h, s, qt, kt: (b, qt[s], h)),
            scratch_shapes=[pltpu.VMEM((2, 1, tq), f32), pltpu.VMEM((2, HEAD + SUM_ROWS, tq), f32)]),
        out_shape=jax.ShapeDtypeStruct((nb, t, 512), f32),
        compiler_params=_params(("parallel", "parallel", "arbitrary"), VMEM_LIMIT),
        name="fox_prompt",
    )(qi_tab, ki_tab, u3, u3, u3, ck)


def _paged_kernel(pt_ref, q_ref, kn_ref, vn_ref, lfn_ref, *refs, g):
    k_refs, v_refs, lf_refs = refs[0:g], refs[g:2 * g], refs[2 * g:3 * g]
    o_ref = refs[3 * g]
    m_ref, l_ref, acc_ref, car_ref = refs[3 * g + 1:]
    step = pl.program_id(1)
    heads = [slice(HEAD * h, HEAD * (h + 1)) for h in range(N_HEAD)]
    q = q_ref[0]

    @pl.when(step == 0)
    def _():
        lane = lax.broadcasted_iota(jnp.int32, acc_ref.shape, 1)
        acc_ref[...] = jnp.where(lane == 0, vn_ref[0], 0.0)
        qk = q * kn_ref[0]
        for h in range(N_HEAD):
            m_ref[h] = jnp.sum(qk[heads[h]], axis=0, keepdims=True)
            l_ref[h] = jnp.ones((1, 1), f32)
        car_ref[...] = lfn_ref[0]

    ii = lax.broadcasted_iota(jnp.int32, (PAGE, PAGE), 0)
    jj = lax.broadcasted_iota(jnp.int32, (PAGE, PAGE), 1)
    later = (ii > jj).astype(bf16)
    carry = car_ref[...]
    biases = []
    for j in range(g):
        lfp = lf_refs[j][0]
        biases.append(_dot_onesr(lfp, later) + carry)
        carry = carry + jnp.sum(lfp, axis=-1, keepdims=True)
    car_ref[...] = carry

    ss = [jnp.concatenate(
        [jnp.sum(k_refs[j][0, heads[h], :] * q[heads[h]], axis=0, keepdims=True) + biases[j][h:h + 1, :]
         for j in range(g)], axis=0) for h in range(N_HEAD)]
    m_olds = [m_ref[h] for h in range(N_HEAD)]
    m_news = [jnp.maximum(m, jnp.max(jnp.max(s, axis=-1, keepdims=True), axis=0, keepdims=True))
              for m, s in zip(m_olds, ss)]
    alphas = [jnp.exp(mo - mn) for mo, mn in zip(m_olds, m_news)]
    ps = [jnp.exp(s - mn) for s, mn in zip(ss, m_news)]
    for h in range(N_HEAD):
        l_ref[h] = alphas[h] * l_ref[h] + jnp.sum(jnp.sum(ps[h], axis=-1, keepdims=True), axis=0, keepdims=True)
        m_ref[h] = m_news[h]
    for h in range(N_HEAD):
        acc = alphas[h] * acc_ref[heads[h], :]
        for j in range(g):
            acc = acc + v_refs[j][0, heads[h], :] * ps[h][j:j + 1, :]
        acc_ref[heads[h], :] = acc

    @pl.when(step == pl.num_programs(1) - 1)
    def _():
        for h in range(N_HEAD):
            o_ref[0, heads[h], :] = jnp.sum(acc_ref[heads[h], :], axis=-1, keepdims=True) / l_ref[h]


def fox_paged(page_table, q, knew, vnew, lfnew, kpool, vpool, lfpool, g):
    nb, npages = page_table.shape
    last = npages - 1
    cur = lambda b, i, pt: (b, 0, 0)
    pool = lambda j: (lambda b, i, pt: (pt[b, last - (i * g + j)], 0, 0))
    col = pl.BlockSpec((1, 512, 1), cur)
    return pl.pallas_call(
        functools.partial(_paged_kernel, g=g),
        grid_spec=pltpu.PrefetchScalarGridSpec(
            num_scalar_prefetch=1,
            grid=(nb, npages // g),
            in_specs=([col, col, col, pl.BlockSpec((1, N_HEAD, 1), cur)]
                      + [pl.BlockSpec((1, 512, PAGE), pool(j)) for j in range(g)]
                      + [pl.BlockSpec((1, 512, PAGE), pool(j)) for j in range(g)]
                      + [pl.BlockSpec((1, N_HEAD, PAGE), pool(j)) for j in range(g)]),
            out_specs=col,
            scratch_shapes=[pltpu.VMEM((N_HEAD, 1, 1), f32), pltpu.VMEM((N_HEAD, 1, 1), f32),
                            pltpu.VMEM((512, PAGE), f32), pltpu.VMEM((N_HEAD, 1), f32)]),
        out_shape=jax.ShapeDtypeStruct((nb, 512, 1), f32),
        compiler_params=_params(("parallel", "arbitrary"), VMEM_LIMIT),
        name="fox_paged",
    )(page_table, q, knew, vnew, lfnew, *([kpool] * g), *([vpool] * g), *([lfpool] * g))


def _merge_update(or_ref, of_ref, ug_ref, wr_ref, wf_ref, wo_ref):
    br = _dot(or_ref[...].astype(bf16), wr_ref[...])
    bf = _dot(of_ref[...].astype(bf16), wf_ref[...])
    ug = ug_ref[...]
    merged = jax.nn.sigmoid(ug[:, :D_MODEL]) * br + jax.nn.sigmoid(ug[:, D_MODEL:]) * bf
    return _dot(merged.astype(bf16), wo_ref[...])


def _merge_kernel(x_ref, or_ref, of_ref, ug_ref, wr_ref, wf_ref, wo_ref, o_ref):
    o_ref[...] = x_ref[...] + _merge_update(or_ref, of_ref, ug_ref, wr_ref, wf_ref, wo_ref)


def _merge_seq_kernel(x_ref, head_ref, or_ref, of_ref, ug_ref, wr_ref, wf_ref, wo_ref, o_ref, *, tiles_per_seq):
    x = _padded_rows(x_ref, head_ref, pl.program_id(0) % tiles_per_seq == 0)
    o_ref[...] = x + _merge_update(or_ref, of_ref, ug_ref, wr_ref, wf_ref, wo_ref)


def merge_out(x, o_r, o_f, u, wr, wf, wo, tm, head=None):
    m = o_r.shape[0]
    row = lambda i: (i, 0)
    cst = lambda i: (0, 0)
    rest = [pl.BlockSpec((tm, 512), row), pl.BlockSpec((tm, 512), row),
            pl.BlockSpec((tm, 2 * D_MODEL), lambda i: (i, COL_G // (2 * D_MODEL))),
            pl.BlockSpec((512, D_MODEL), cst), pl.BlockSpec((512, D_MODEL), cst),
            pl.BlockSpec((D_MODEL, D_MODEL), cst)]
    if head is None:
        body, first, args = _merge_kernel, [pl.BlockSpec((tm, D_MODEL), row)], (x,)
    else:
        tps = (x.shape[1] + ROW0) // tm
        body = functools.partial(_merge_seq_kernel, tiles_per_seq=tps)
        first = [_seq_tile_spec(tm, D_MODEL, tps), pl.BlockSpec((ROW0, D_MODEL), cst)]
        args = (x, head)
    return pl.pallas_call(
        body,
        grid=(m // tm,),
        in_specs=first + rest,
        out_specs=pl.BlockSpec((tm, D_MODEL), row),
        out_shape=jax.ShapeDtypeStruct((m, D_MODEL), f32),
        compiler_params=_params(("parallel",), VMEM_LIMIT),
        name="merge_out",
    )(*args, o_r, o_f, u, wr, wf, wo)


def _ffn_kernel(x_ref, g2_ref, gf_ref, wu_ref, wd_ref, o_ref, h_ref, acc_ref):
    j = pl.program_id(1)

    @pl.when(j == 0)
    def _():
        h_ref[...] = _rms(x_ref[...], g2_ref[...]).astype(bf16)
        acc_ref[...] = jnp.zeros_like(acc_ref)

    hid = jnp.maximum(_dot(h_ref[...], wu_ref[...]), 0.0)
    acc_ref[...] += _dot((hid * hid).astype(bf16), wd_ref[...])

    @pl.when(j == pl.num_programs(1) - 1)
    def _():
        o_ref[...] = _rms(x_ref[...] + acc_ref[...], gf_ref[...])


def ffn_final(x, g2, gf, wu, wd, tm, tf):
    m = x.shape[0]
    return pl.pallas_call(
        _ffn_kernel,
        grid=(m // tm, D_FF // tf),
        in_specs=[pl.BlockSpec((tm, D_MODEL), lambda i, j: (i, 0)),
                  pl.BlockSpec((1, D_MODEL), lambda i, j: (0, 0)), pl.BlockSpec((1, D_MODEL), lambda i, j: (0, 0)),
                  pl.BlockSpec((D_MODEL, tf), lambda i, j: (0, j)), pl.BlockSpec((tf, D_MODEL), lambda i, j: (j, 0))],
        out_specs=pl.BlockSpec((tm, D_MODEL), lambda i, j: (i, 0)),
        out_shape=jax.ShapeDtypeStruct((m, D_MODEL), f32),
        scratch_shapes=[pltpu.VMEM((tm, D_MODEL), bf16), pltpu.VMEM((tm, D_MODEL), f32)],
        compiler_params=_params(("parallel", "arbitrary"), VMEM_LIMIT),
        name="ffn_final",
    )(x, g2, gf, wu, wd)


def kernel(x_prompt, x_sample, state_shift, state_wkv, cache_k, cache_v, cache_logf, page_table, meta_tokens,
           norm_mix, w_in, mu_shift, w0, w_decay_up, a0, w_a_up, w_g_up, k_k, k_a, r_k, ln_x_w, ln_x_b, b_forget,
           w_br_rwkv, w_br_fox, w_out, norm_ffn, w_ffn_up, w_ffn_down, norm_final):
    assert norm_mix.shape[0] == 1, "single layer"
    nb, seq, _ = x_prompt.shape
    ns = x_sample.shape[0]
    tp = FRONT_PAD + N_META + seq
    tlen = N_META + seq

    wi = w_in[0]
    wi = jnp.concatenate([wi[:, 0:1792], wi[:, 3328:3336], jnp.zeros((D_MODEL, 248), f32),
                          wi[:, 3336:5384], wi[:, 1792:3328]], axis=1).astype(bf16)
    row2 = lambda a: a.reshape(1, -1)
    lane_head = jnp.arange(512, dtype=jnp.int32) // HEAD
    pr = dict(mu=row2(mu_shift[0]), w0=row2(w0[0]), wd=w_decay_up[0], a0=row2(a0[0]), wa=w_a_up[0], wg=w_g_up[0],
              k_k=row2(k_k[0]), k_a=row2(k_a[0]), r_k=row2(r_k[0]),
              bd=(lane_head[:, None] == lane_head[None, :]).astype(bf16))
    lnw, lnb = row2(ln_x_w[0]), row2(ln_x_b[0])
    g_mix, g_ffn, g_fin = row2(norm_mix[0]), row2(norm_ffn[0]), row2(norm_final)
    wr, wf, wo = w_br_rwkv[0].astype(bf16), w_br_fox[0].astype(bf16), w_out[0].astype(bf16)
    wu, wdn = w_ffn_up[0].astype(bf16), w_ffn_down[0].astype(bf16)

    head = jnp.concatenate([jnp.zeros((FRONT_PAD, D_MODEL), f32), meta_tokens], axis=0)
    u2 = norm_matmul_seq(x_prompt, head, g_mix, wi, TM_PROJ, TN_PROJ)
    u3 = u2.reshape(nb, tp, D_INP)
    us2 = norm_matmul(x_sample.reshape(ns, D_MODEL), g_mix, wi, ns, TN_PROJ)

    r, lw, k2, v, kkn, b, g, bonus = rwkv_prep(u3, None, pr, TM_PREP)
    maps = wkv_chunks(r, lw, k2, v, kkn, b, CHUNKS_PER_STEP)
    maps = maps.reshape(nb * N_HEAD, tp // CHUNK, 2 * CHUNK, 2 * HEAD)
    o_r, s_kv = wkv_serial(maps, g, bonus, lnw, lnb, CHUNKS_PER_STEP)
    new_wkv_p = jnp.swapaxes(s_kv, 1, 2).reshape(1, nb, N_HEAD, HEAD, HEAD)

    us3 = us2.reshape(1, ns, D_INP)
    sp = rwkv_prep(us3, state_shift[0].reshape(1, ns, D_SHIFT), pr, ns)
    rs, lws, ks, vs, kks, bs, gs, bonus_s = sp
    rowify = lambda a: a.reshape(ns, N_HEAD, 1, HEAD)
    s_new, ys = wkv_step(state_wkv[0], rowify(rs), rowify(lws), rowify(ks), rowify(kks), rowify(bs),
                         rowify(vs).reshape(ns, N_HEAD, HEAD, 1))
    ys_hm = jnp.transpose(ys.reshape(ns, N_HEAD, HEAD), (1, 0, 2))[None]
    o_r_s = rwkv_post(ys_hm, gs, bonus_s, lnw, lnb, ns)[0]

    bias_rows = jnp.tile(b_forget[0], nb).reshape(nb * N_HEAD, 1)
    lf_t, c_t = logf_cumsum(u3, bias_rows)
    pad_key = jnp.arange(tp, dtype=jnp.int32)[None, :] < FRONT_PAD
    ck = jnp.where(pad_key, -NEG, c_t * LOG2E).reshape(nb, N_HEAD // 2, 2, tp)
    o_f = fox_prompt(u3, jnp.pad(ck, ((0, 0), (0, 0), (0, 6), (0, 0))), TQ_FOX, NSUB_FOX)

    qs = us2[:, COL_Q:COL_Q + 512] * (HEAD ** -0.5)
    k_s = us2[:, COL_Q + 512:COL_Q + 1024]
    v_s = us2[:, COL_Q + 1024:COL_Q + 1536]
    lf_s = jax.nn.log_sigmoid(us2[:, COL_F:COL_F + N_HEAD] + b_forget[0][None, :])
    n_pool = cache_k.shape[1]
    kpool = jnp.transpose(cache_k[0], (0, 2, 3, 1)).reshape(n_pool, 512, PAGE)
    vpool = jnp.transpose(cache_v[0], (0, 2, 3, 1)).reshape(n_pool, 512, PAGE)
    o_f_s = fox_paged(page_table, qs.reshape(ns, 512, 1), k_s.reshape(ns, 512, 1), v_s.reshape(ns, 512, 1),
                      lf_s.reshape(ns, N_HEAD, 1), kpool, vpool, jnp.swapaxes(cache_logf[0], 1, 2),
                      PAGES_PER_STEP).reshape(ns, 512)

    x1 = merge_out(x_prompt, o_r.reshape(nb * tp, 512), o_f.reshape(nb * tp, 512), u2, wr, wf, wo, TM_MERGE, head)
    yp = ffn_final(x1, g_ffn, g_fin, wu, wdn, TM_FFN, TF_FFN).reshape(nb, tp, D_MODEL)
    x1s = merge_out(x_sample.reshape(ns, D_MODEL), o_r_s, o_f_s, us2, wr, wf, wo, ns)
    ysamp = ffn_final(x1s, g_ffn, g_fin, wu, wdn, ns, TF_FFN)

    y_prompt = yp[:, ROW0:]
    y_sample = ysamp.reshape(ns, 1, D_MODEL)
    new_shift_p = u3[:, tp - 1, :D_SHIFT][None]
    k_p = u3[:, FRONT_PAD:, COL_Q + 512:COL_Q + 1024].reshape(1, nb, tlen, N_HEAD, HEAD)
    v_p = u3[:, FRONT_PAD:, COL_Q + 1024:COL_Q + 1536].reshape(1, nb, tlen, N_HEAD, HEAD)
    lf_p = jnp.transpose(lf_t.reshape(nb, N_HEAD, tp), (0, 2, 1))[:, FRONT_PAD:][None]
    return (y_prompt, y_sample, new_shift_p, new_wkv_p, k_p, v_p, lf_p,
            us2[:, :D_SHIFT][None], s_new[None], k_s.reshape(1, ns, 1, N_HEAD, HEAD),
            v_s.reshape(1, ns, 1, N_HEAD, HEAD), lf_s.reshape(1, ns, 1, N_HEAD))
```

```python
import functools

import jax
import jax.numpy as jnp
from jax import lax
from jax.experimental import pallas as pl
from jax.experimental.pallas import tpu as pltpu

f32 = jnp.float32
bf16 = jnp.bfloat16

D_MODEL = 1024
N_META = 16
D_RWKV = 512
HEAD = 64
N_HEAD = 8
D_SHIFT = 1792
D_FF = 4096
RMS_EPS = 1e-6
GN_EPS = 64e-5
PAGE = 128

FRONT_PAD = 112
ROW0 = FRONT_PAD + N_META
CHUNK = 64
NEG = -1e30
LOG2E = 1.4426950408889634

COL_R = 0
COL_F = 1792
COL_G = 2048
COL_Q = 4096
D_INP = 5632

VMEM_LIMIT = 56 * 1024 * 1024

TM_PROJ, TN_PROJ = 1664, 1408
TM_PREP = 320
CHUNKS_PER_STEP = 5
TQ_FOX, NSUB_FOX = 1664, 8
PAGES_PER_STEP = 32
TM_MERGE = 640
TM_FFN, TF_FFN = 1664, 512
STEP_ROWS = 4

NN = (((1,), (0,)), ((), ()))
NT = (((1,), (1,)), ((), ()))
TN = (((0,), (0,)), ((), ()))


def _dot(a, b, dims=NN):
    return lax.dot_general(a, b, dims, preferred_element_type=f32)


def _split2(x):
    hi = x.astype(bf16)
    lo = (x - hi.astype(f32)).astype(bf16)
    return hi, lo


def _split3(x):
    hi = x.astype(bf16)
    r1 = x - hi.astype(f32)
    mid = r1.astype(bf16)
    lo = (r1 - mid.astype(f32)).astype(bf16)
    return hi, mid, lo


def _dot3(a, b, dims=NN):
    ah, al = _split2(a)
    bh, bl = _split2(b)
    return _dot(ah, bh, dims) + (_dot(ah, bl, dims) + _dot(al, bh, dims))


def _dot1(a, b, dims=NN):
    return _dot(a.astype(bf16), b.astype(bf16), dims)


def _dot_onesr(x, ones_bf16, dims=NN):
    hi, mid, lo = _split3(x)
    return _dot(hi, ones_bf16, dims) + (_dot(mid, ones_bf16, dims) + _dot(lo, ones_bf16, dims))


def _segsum(x, ones_bf16):
    hi, lo = _split2(x)
    return _dot(hi, ones_bf16) + _dot(lo, ones_bf16)


def _dot_onesl(ones_bf16, x, dims=NN):
    hi, mid, lo = _split3(x)
    return _dot(ones_bf16, hi, dims) + (_dot(ones_bf16, mid, dims) + _dot(ones_bf16, lo, dims))


def _params(sem, vmem=None):
    return pltpu.CompilerParams(dimension_semantics=sem, vmem_limit_bytes=vmem)


def _rms(x, g):
    ms = jnp.mean(x * x, axis=-1, keepdims=True)
    return x * lax.rsqrt(ms + RMS_EPS) * g


def _norm_matmul_kernel(x_ref, g_ref, w_ref, o_ref, h_ref):
    @pl.when(pl.program_id(1) == 0)
    def _():
        h_ref[...] = _rms(x_ref[...], g_ref[...]).astype(bf16)

    o_ref[...] = _dot(h_ref[...], w_ref[...])


def norm_matmul(x, g, w, tm, tn):
    m, d = x.shape
    n = w.shape[1]
    return pl.pallas_call(
        _norm_matmul_kernel,
        grid=(m // tm, n // tn),
        in_specs=[pl.BlockSpec((tm, d), lambda i, j: (i, 0)),
                  pl.BlockSpec((1, d), lambda i, j: (0, 0)),
                  pl.BlockSpec((d, tn), lambda i, j: (0, j))],
        out_specs=pl.BlockSpec((tm, tn), lambda i, j: (i, j)),
        out_shape=jax.ShapeDtypeStruct((m, n), f32),
        scratch_shapes=[pltpu.VMEM((tm, d), bf16)],
        compiler_params=_params(("parallel", "arbitrary"), VMEM_LIMIT),
        name="norm_matmul",
    )(x, g, w)


def _seq_tile_spec(tm, d, tiles_per_seq):
    def index(i, *_):
        start = jnp.maximum((i % tiles_per_seq) * tm - ROW0, 0)
        return (i // tiles_per_seq, pl.multiple_of(start, 128), 0)
    return pl.BlockSpec((pl.Element(1), pl.Element(tm), pl.Element(d)), index)


def _padded_rows(x_ref, head_ref, first):
    x = x_ref[0]
    rowi = lax.broadcasted_iota(jnp.int32, x.shape, 0)
    head = jnp.concatenate([head_ref[...], jnp.zeros((x.shape[0] - ROW0, x.shape[1]), f32)], axis=0)
    return jnp.where(first, jnp.where(rowi < ROW0, head, pltpu.roll(x, ROW0, 0)), x)


def _norm_matmul_seq_kernel(x_ref, head_ref, g_ref, w_ref, o_ref, h_ref, *, tiles_per_seq):
    @pl.when(pl.program_id(1) == 0)
    def _():
        x = _padded_rows(x_ref, head_ref, pl.program_id(0) % tiles_per_seq == 0)
        h_ref[...] = _rms(x, g_ref[...]).astype(bf16)

    o_ref[...] = _dot(h_ref[...], w_ref[...])


def norm_matmul_seq(x3, head, g, w, tm, tn):
    nb, seq, d = x3.shape
    tps = (seq + ROW0) // tm
    n = w.shape[1]
    return pl.pallas_call(
        functools.partial(_norm_matmul_seq_kernel, tiles_per_seq=tps),
        grid=(nb * tps, n // tn),
        in_specs=[_seq_tile_spec(tm, d, tps),
                  pl.BlockSpec((ROW0, d), lambda i, j: (0, 0)),
                  pl.BlockSpec((1, d), lambda i, j: (0, 0)),
                  pl.BlockSpec((d, tn), lambda i, j: (0, j))],
        out_specs=pl.BlockSpec((tm, tn), lambda i, j: (i, j)),
        out_shape=jax.ShapeDtypeStruct((nb * tps * tm, n), f32),
        scratch_shapes=[pltpu.VMEM((tm, d), bf16)],
        compiler_params=_params(("parallel", "arbitrary"), VMEM_LIMIT),
        name="norm_matmul_seq",
    )(x3, head, g, w)


def _prep_kernel(u_ref, up_ref, mu_ref, w0_ref, wd_ref, a0_ref, wa_ref, wg_ref, kk_ref, ka_ref, rk_ref,
                 bd_ref, r_o, lw_o, k_o, v_o, kkn_o, b_o, g_o, bonus_o, *, prev_is_tail):
    u = u_ref[0]
    if prev_is_tail:
        first = jnp.where(pl.program_id(1) == 0, 0.0, up_ref[0, 7:8, :])
        rowi = lax.broadcasted_iota(jnp.int32, u.shape, 0)
        up = jnp.where(rowi == 0, first, pltpu.roll(u, 1, 0))
    else:
        up = up_ref[0]
    us = u + mu_ref[...] * (up - u)
    r = us[:, 0:512]
    k = us[:, 512:1024]
    v = us[:, 1024:1536]
    zw = us[:, 1536:1600]
    za = us[:, 1600:1664]
    zg = us[:, 1664:1792]
    bd = bd_ref[...]

    z = -(w0_ref[...] + _dot3(jnp.tanh(zw), wd_ref[...]))
    w_raw = -(jnp.maximum(z, 0.0) + jnp.log1p(jnp.exp(-jnp.abs(z)))) - 0.5
    lw = -jnp.exp(w_raw)
    a = jax.nn.sigmoid(a0_ref[...] + _dot3(za, wa_ref[...]))
    g = _dot3(jax.nn.sigmoid(zg), wg_ref[...])
    kk = k * kk_ref[...]
    ss = _segsum(kk * kk, bd)
    kkn = kk / jnp.maximum(jnp.sqrt(ss), 1e-12)
    k2 = k * (1.0 + (a - 1.0) * ka_ref[...])
    b = kkn * a
    bonus = _segsum(r * k2 * rk_ref[...], bd) * v

    r_o[0] = r
    lw_o[0] = lw
    k_o[0] = k2
    v_o[0] = v
    kkn_o[0] = kkn
    b_o[0] = b
    g_o[0] = g
    bonus_o[0] = bonus


def rwkv_prep(u3, up3, pr, tm):
    nb, t = u3.shape[0], u3.shape[1]
    row = lambda i, j: (i, j, 0)
    if up3 is None:
        up_arr = u3
        up_spec = pl.BlockSpec((1, 8, D_SHIFT), lambda i, j: (i, jnp.maximum(j * (tm // 8) - 1, 0), 0))
    else:
        up_arr = up3
        up_spec = pl.BlockSpec((1, tm, D_SHIFT), row)
    cst = lambda i, j: (0, 0)
    out = jax.ShapeDtypeStruct((nb, t, D_RWKV), f32)
    out_spec = pl.BlockSpec((1, tm, D_RWKV), row)
    vec = lambda n: pl.BlockSpec((1, n), cst)
    return pl.pallas_call(
        functools.partial(_prep_kernel, prev_is_tail=up3 is None),
        grid=(nb, t // tm),
        in_specs=[pl.BlockSpec((1, tm, D_SHIFT), row), up_spec,
                  vec(D_SHIFT), vec(512), pl.BlockSpec((64, 512), cst), vec(512), pl.BlockSpec((64, 512), cst),
                  pl.BlockSpec((128, 512), cst), vec(512), vec(512), vec(512), pl.BlockSpec((512, 512), cst)],
        out_specs=[out_spec] * 8,
        out_shape=[out] * 8,
        compiler_params=_params(("parallel", "parallel"), VMEM_LIMIT),
        name="rwkv_prep",
    )(u3, up_arr, pr["mu"], pr["w0"], pr["wd"], pr["a0"], pr["wa"], pr["wg"], pr["k_k"], pr["k_a"], pr["r_k"],
      pr["bd"])


def _each(f, *lists):
    return [f(*xs) for xs in zip(*lists)]


def _wkv_chunk_kernel(r_ref, lw_ref, k_ref, v_ref, kk_ref, b_ref, m_ref, *, nc):
    ii = lax.broadcasted_iota(jnp.int32, (CHUNK, CHUNK), 0)
    jj = lax.broadcasted_iota(jnp.int32, (CHUNK, CHUNK), 1)
    incl = jj <= ii
    strict = jj < ii
    ones_incl = incl.astype(bf16)
    same_blk = (ii >> 4) == (jj >> 4)
    eye = (ii == jj).astype(f32)
    units = [(c, h) for c in range(nc) for h in range(N_HEAD)]
    tile = lambda ref: [ref[0, c * CHUNK:(c + 1) * CHUNK, h * HEAD:(h + 1) * HEAD] for c, h in units]
    r, lw, k, v, kk, b = (tile(ref) for ref in (r_ref, lw_ref, k_ref, v_ref, kk_ref, b_ref))

    gcum = _each(lambda x: _dot_onesl(ones_incl, x), lw)
    gend = _each(lambda x: x[CHUNK - 1:CHUNK, :], gcum)
    kkt = _each(lambda x, gc, l: x * jnp.exp(gc - l), kk, gcum, lw)
    rt = _each(lambda x, gc: x * jnp.exp(gc), r, gcum)
    em = _each(lambda gc: jnp.exp(-gc), gcum)
    kh = _each(jnp.multiply, k, em)
    bh = _each(jnp.multiply, b, em)
    ec = _each(lambda ge, gc: jnp.exp(ge - gc), gend, gcum)
    kg = _each(jnp.multiply, k, ec)
    bg = _each(jnp.multiply, b, ec)

    lhs = _each(lambda x, y: jnp.concatenate([x, y], axis=0), kkt, rt)
    ab = _each(lambda x, y: _dot1(x, y, NT), lhs, bh)
    ak = _each(lambda x, y: _dot1(x, y, NT), lhs, kh)
    a_kb = _each(lambda x: jnp.where(strict, x[:CHUNK], 0.0), ab)
    a_rb = _each(lambda x: jnp.where(incl, x[CHUNK:], 0.0), ab)
    a_kk = _each(lambda x: jnp.where(strict, x[:CHUNK], 0.0), ak)
    a_rk = _each(lambda x: jnp.where(incl, x[CHUNK:], 0.0), ak)

    dg = _each(lambda x: jnp.where(same_blk, x, 0.0), a_kb)
    lo = _each(jnp.subtract, a_kb, dg)
    n1 = _each(jnp.negative, dg)
    n2 = _each(lambda x: _dot1(x, x), n1)
    n4 = _each(lambda x: _dot1(x, x), n2)
    n8 = _each(lambda x: _dot1(x, x), n4)
    t12 = _each(lambda x, y: _dot1(eye + x, eye + y), n1, n2)
    t48 = _each(lambda x, y: _dot1(eye + x, eye + y), n4, n8)
    td = _each(_dot1, t12, t48)
    x1 = _each(lambda x, y: -_dot1(x, y), td, lo)
    x2 = _each(lambda x: _dot1(x, x), x1)
    xx = _each(lambda x, y: _dot1(eye + x, eye + y), x1, x2)
    tinv = _each(_dot1, xx, td)

    akv = _each(_dot1, a_kk, v)
    w1u = _each(lambda t, x, y: _dot1(t, jnp.concatenate([x, y], axis=1)), tinv, kkt, akv)
    z = _each(lambda x, y, w: _dot1(jnp.concatenate([x.T, y], axis=0), w), bg, a_rb, w1u)
    kv = _each(lambda x, y, w: _dot1(jnp.concatenate([x.T, y], axis=0), w), kg, a_rk, v)
    for i, (c, h) in enumerate(units):
        base = jnp.concatenate([eye * jnp.exp(gend[i]), rt[i]], axis=0)
        m_ref[0, h, c] = jnp.concatenate([base, kv[i]], axis=1) - z[i]


def wkv_chunks(r, lw, k, v, kk, b, nc):
    nb, t, _ = r.shape
    nchunk = t // CHUNK
    in_spec = pl.BlockSpec((1, nc * CHUNK, D_RWKV), lambda i, j: (i, j, 0))
    out_spec = pl.BlockSpec((1, N_HEAD, nc, 2 * CHUNK, 2 * HEAD), lambda i, j: (i, 0, j, 0, 0))
    out = jax.ShapeDtypeStruct((nb, N_HEAD, nchunk, 2 * CHUNK, 2 * HEAD), f32)
    return pl.pallas_call(
        functools.partial(_wkv_chunk_kernel, nc=nc),
        grid=(nb, nchunk // nc),
        in_specs=[in_spec] * 6,
        out_specs=out_spec,
        out_shape=out,
        compiler_params=_params(("parallel", "parallel"), VMEM_LIMIT),
        name="wkv_chunks",
    )(r, lw, k, v, kk, b)


def _group_norm(y):
    mu = jnp.mean(y, axis=-1, keepdims=True)
    yc = y - mu
    var = jnp.mean(yc * yc, axis=-1, keepdims=True)
    return yc * lax.rsqrt(var + GN_EPS)


def _wkv_serial_kernel(m_ref, g_ref, bonus_ref, lnw_ref, lnb_ref, o_ref, s_ref, st_ref, *, nb, nc):
    @pl.when(pl.program_id(0) == 0)
    def _():
        st_ref[...] = jnp.zeros_like(st_ref)

    nbh = nb * N_HEAD
    st = [st_ref[i] for i in range(nbh)]
    ys = []
    for c in range(nc):
        zs = [_dot1(m_ref[i, c, :, :HEAD], st[i]) + m_ref[i, c, :, HEAD:] for i in range(nbh)]
        st = [z[:CHUNK] for z in zs]
        ys += [z[CHUNK:] for z in zs]
    for i in range(nbh):
        st_ref[i] = st[i]

    mus = [jnp.mean(y, axis=-1, keepdims=True) for y in ys]
    ycs = _each(jnp.subtract, ys, mus)
    vrs = [jnp.mean(yc * yc, axis=-1, keepdims=True) for yc in ycs]
    yns = _each(lambda yc, vr: yc * lax.rsqrt(vr + GN_EPS), ycs, vrs)
    for c in range(nc):
        rows = slice(c * CHUNK, (c + 1) * CHUNK)
        for bi in range(nb):
            yn = jnp.concatenate(yns[c * nbh + bi * N_HEAD:c * nbh + (bi + 1) * N_HEAD], axis=-1)
            o_ref[bi, rows, :] = (yn * lnw_ref[...] + lnb_ref[...] + bonus_ref[bi, rows, :]) * g_ref[bi, rows, :]

    @pl.when(pl.program_id(0) == pl.num_programs(0) - 1)
    def _():
        s_ref[...] = st_ref[...]


def wkv_serial(maps, g, bonus, lnw, lnb, nc):
    nbh, nchunk = maps.shape[0], maps.shape[1]
    nb = g.shape[0]
    blk = pl.BlockSpec((nbh, nc, 2 * CHUNK, 2 * HEAD), lambda j: (0, j, 0, 0))
    tok = pl.BlockSpec((nb, nc * CHUNK, D_RWKV), lambda j: (0, j, 0))
    par = pl.BlockSpec((1, D_RWKV), lambda j: (0, 0))
    return pl.pallas_call(
        functools.partial(_wkv_serial_kernel, nb=nb, nc=nc),
        grid=(nchunk // nc,),
        in_specs=[blk, tok, tok, par, par],
        out_specs=[tok, pl.BlockSpec((nbh, HEAD, HEAD), lambda j: (0, 0, 0))],
        out_shape=[jax.ShapeDtypeStruct((nb, nchunk * CHUNK, D_RWKV), f32),
                   jax.ShapeDtypeStruct((nbh, HEAD, HEAD), f32)],
        scratch_shapes=[pltpu.VMEM((nbh, HEAD, HEAD), f32)],
        compiler_params=_params(("arbitrary",), VMEM_LIMIT),
        name="wkv_serial",
    )(maps, g, bonus, lnw, lnb)


def _wkv_step_kernel(s_ref, r_ref, lw_ref, k_ref, kk_ref, b_ref, vc_ref, so_ref, y_ref):
    s = s_ref[...]
    skk = jnp.sum(s * kk_ref[...], axis=-1, keepdims=True)
    s1 = s * jnp.exp(lw_ref[...]) - skk * b_ref[...] + vc_ref[...] * k_ref[...]
    so_ref[...] = s1
    y_ref[...] = jnp.sum(s1 * r_ref[...], axis=-1, keepdims=True)


def wkv_step(s, r, lw, k, kk, b, vcol):
    n = s.shape[0]
    row = pl.BlockSpec((STEP_ROWS, N_HEAD, 1, HEAD), lambda i: (i, 0, 0, 0))
    col = pl.BlockSpec((STEP_ROWS, N_HEAD, HEAD, 1), lambda i: (i, 0, 0, 0))
    mat = pl.BlockSpec((STEP_ROWS, N_HEAD, HEAD, HEAD), lambda i: (i, 0, 0, 0))
    return pl.pallas_call(
        _wkv_step_kernel,
        grid=(n // STEP_ROWS,),
        in_specs=[mat, row, row, row, row, row, col],
        out_specs=[mat, col],
        out_shape=[jax.ShapeDtypeStruct(s.shape, f32), jax.ShapeDtypeStruct((n, N_HEAD, HEAD, 1), f32)],
        compiler_params=_params(("parallel",)),
        name="wkv_step",
    )(s, r, lw, k, kk, b, vcol)


def _post_kernel(y_ref, g_ref, bonus_ref, lnw_ref, lnb_ref, o_ref):
    yn = jnp.concatenate([_group_norm(y_ref[0, h]) for h in range(N_HEAD)], axis=-1)
    o_ref[0] = (yn * lnw_ref[...] + lnb_ref[...] + bonus_ref[0]) * g_ref[0]


def rwkv_post(y, g, bonus, lnw, lnb, tm):
    nb, _, t, _ = y.shape
    hm = pl.BlockSpec((1, N_HEAD, tm, HEAD), lambda i, j: (i, 0, j, 0))
    tok = pl.BlockSpec((1, tm, D_RWKV), lambda i, j: (i, j, 0))
    par = pl.BlockSpec((1, D_RWKV), lambda i, j: (0, 0))
    return pl.pallas_call(
        _post_kernel,
        grid=(nb, t // tm),
        in_specs=[hm, tok, tok, par, par],
        out_specs=tok,
        out_shape=jax.ShapeDtypeStruct((nb, t, D_RWKV), f32),
        compiler_params=_params(("parallel", "parallel"), VMEM_LIMIT),
        name="rwkv_post",
    )(y, g, bonus, lnw, lnb)


def _logf_kernel(u_ref, b_ref, lf_ref, c_ref, *, nblk):
    ii = lax.broadcasted_iota(jnp.int32, (128, 128), 0)
    jj = lax.broadcasted_iota(jnp.int32, (128, 128), 1)
    upper = (ii <= jj).astype(bf16)
    nb = u_ref.shape[0]
    carry = jnp.zeros((nb * N_HEAD, 1), f32)
    for blk in range(nblk):
        sl = slice(128 * blk, 128 * (blk + 1))
        fl = jnp.concatenate([u_ref[bi, sl, :].T[:N_HEAD] for bi in range(nb)], axis=0)
        lf = jax.nn.log_sigmoid(fl + b_ref[...])
        lf_ref[:, sl] = lf
        cs = _dot_onesr(lf, upper) + carry
        c_ref[:, sl] = cs
        carry = cs[:, 127:128]


def logf_cumsum(u3, bias):
    nb, t, _ = u3.shape
    out = jax.ShapeDtypeStruct((nb * N_HEAD, t), f32)
    return pl.pallas_call(
        functools.partial(_logf_kernel, nblk=t // 128),
        grid=(1,),
        in_specs=[pl.BlockSpec((nb, t, 128), lambda i: (0, 0, COL_F // 128)),
                  pl.BlockSpec((nb * N_HEAD, 1), lambda i: (0, 0))],
        out_specs=[pl.BlockSpec((nb * N_HEAD, t), lambda i: (0, 0))] * 2,
        out_shape=[out, out],
        compiler_params=_params(("arbitrary",), VMEM_LIMIT),
        name="logf_cumsum",
    )(u3, bias)


SUM_ROWS = 16


def _fox_kernel(qi_ref, ki_ref, q_ref, k_ref, v_ref, ck_ref, o_ref, m_ref, acc_ref, *, tq, tk, nsub):
    step_id = pl.program_id(2)
    qi = qi_ref[step_id]
    ki = ki_ref[step_id]

    @pl.when(ki == 0)
    def _():
        m_ref[...] = jnp.full_like(m_ref, NEG)
        acc_ref[...] = jnp.zeros_like(acc_ref)

    def step(masked):
        lane_lo = lax.broadcasted_iota(jnp.int32, (tq, 2 * HEAD), 1) < HEAD
        q = q_ref[0] * (HEAD ** -0.5 * LOG2E)
        qes = [jnp.where(lane_lo == (e == 0), q, 0.0).astype(bf16) for e in range(2)]
        ts = tk // nsub
        ckt = ck_ref[0, 0].T

        def first_col(sub):
            return (sub * ts) // 128 * 128 if masked else 0

        def scores(sub):
            rs = slice(sub * ts, (sub + 1) * ts)
            lo = first_col(sub)
            kb = k_ref[0, rs, :].astype(bf16)
            return [_dot(kb, qes[e][lo:, :], NT) for e in range(2)]

        s_cur = scores(0)
        for sub in range(nsub):
            s_next = scores(sub + 1) if sub + 1 < nsub else None
            rs = slice(sub * ts, (sub + 1) * ts)
            lo = first_col(sub)
            vt = v_ref[0, rs, :].T.astype(bf16)
            ones = jnp.ones((SUM_ROWS, ts), bf16)
            ps, alphas = [], []
            for e in range(2):
                s = s_cur[e] - ckt[rs, e:e + 1]
                if masked:
                    row = lax.broadcasted_iota(jnp.int32, s.shape, 0) + sub * ts
                    col = lax.broadcasted_iota(jnp.int32, s.shape, 1) + lo
                    s = jnp.where(row <= col, s, NEG)
                m_old = m_ref[e, :, lo:]
                m_new = jnp.maximum(m_old, jnp.max(s, axis=0, keepdims=True))
                alphas.append(jnp.exp2(m_old - m_new))
                ps.append(jnp.exp2(s - m_new).astype(bf16))
                m_ref[e, :, lo:] = m_new
            for e in range(2):
                vte = jnp.concatenate([vt[HEAD * e:HEAD * (e + 1), :], ones], axis=0)
                acc_ref[e, :, lo:] = alphas[e] * acc_ref[e, :, lo:] + _dot(vte, ps[e])
            s_cur = s_next

    @pl.when(ki < qi)
    def _():
        step(False)

    @pl.when(ki == qi)
    def _():
        step(True)
        o = [acc_ref[e, 0:HEAD, :] / acc_ref[e, HEAD:HEAD + 1, :] for e in range(2)]
        o_ref[0] = jnp.concatenate(o, axis=0).T


def fox_prompt(u3, ck, tq, nsub):
    nb, t, _ = u3.shape
    nq = t // tq
    cq, ckk, cv = COL_Q // 128, (COL_Q + 512) // 128, (COL_Q + 1024) // 128
    pairs = [(i, j) for i in range(nq) for j in range(i + 1)]
    qi_tab = jnp.asarray([p[0] for p in pairs], jnp.int32)
    ki_tab = jnp.asarray([p[1] for p in pairs], jnp.int32)
    return pl.pallas_call(
        functools.partial(_fox_kernel, tq=tq, tk=tq, nsub=nsub),
        grid_spec=pltpu.PrefetchScalarGridSpec(
            num_scalar_prefetch=2,
            grid=(nb, N_HEAD // 2, len(pairs)),
            in_specs=[pl.BlockSpec((1, tq, 128), lambda b, h, s, qt, kt: (b, qt[s], cq + h)),
                      pl.BlockSpec((1, tq, 128), lambda b, h, s, qt, kt: (b, kt[s], ckk + h)),
                      pl.BlockSpec((1, tq, 128), lambda b, h, s, qt, kt: (b, kt[s], cv + h)),
                      pl.BlockSpec((1, 1, 8, tq), lambda b, h, s, qt, kt: (b, h, 0, kt[s]))],
            out_specs=pl.BlockSpec((1, tq, 128), lambda b, h, s, qt, kt: (b, qt[s], h)),
            scratch_shapes=[pltpu.VMEM((2, 1, tq), f32), pltpu.VMEM((2, HEAD + SUM_ROWS, tq), f32)]),
        out_shape=jax.ShapeDtypeStruct((nb, t, 512), f32),
        compiler_params=_params(("parallel", "parallel", "arbitrary"), VMEM_LIMIT),
        name="fox_prompt",
    )(qi_tab, ki_tab, u3, u3, u3, ck)


def _paged_kernel(pt_ref, q_ref, kn_ref, vn_ref, lfn_ref, *refs, g):
    k_refs, v_refs, lf_refs = refs[0:g], refs[g:2 * g], refs[2 * g:3 * g]
    o_ref = refs[3 * g]
    m_ref, l_ref, acc_ref, car_ref = refs[3 * g + 1:]
    step = pl.program_id(1)
    heads = [slice(HEAD * h, HEAD * (h + 1)) for h in range(N_HEAD)]
    q = q_ref[0]

    @pl.when(step == 0)
    def _():
        lane = lax.broadcasted_iota(jnp.int32, acc_ref.shape, 1)
        acc_ref[...] = jnp.where(lane == 0, vn_ref[0], 0.0)
        qk = q * kn_ref[0]
        for h in range(N_HEAD):
            m_ref[h] = jnp.sum(qk[heads[h]], axis=0, keepdims=True)
            l_ref[h] = jnp.ones((1, 1), f32)
        car_ref[...] = lfn_ref[0]

    ii = lax.broadcasted_iota(jnp.int32, (PAGE, PAGE), 0)
    jj = lax.broadcasted_iota(jnp.int32, (PAGE, PAGE), 1)
    later = (ii > jj).astype(bf16)
    carry = car_ref[...]
    biases = []
    for j in range(g):
        lfp = lf_refs[j][0]
        biases.append(_dot_onesr(lfp, later) + carry)
        carry = carry + jnp.sum(lfp, axis=-1, keepdims=True)
    car_ref[...] = carry

    ss = [jnp.concatenate(
        [jnp.sum(k_refs[j][0, heads[h], :] * q[heads[h]], axis=0, keepdims=True) + biases[j][h:h + 1, :]
         for j in range(g)], axis=0) for h in range(N_HEAD)]
    m_olds = [m_ref[h] for h in range(N_HEAD)]
    m_news = [jnp.maximum(m, jnp.max(jnp.max(s, axis=-1, keepdims=True), axis=0, keepdims=True))
              for m, s in zip(m_olds, ss)]
    alphas = [jnp.exp(mo - mn) for mo, mn in zip(m_olds, m_news)]
    ps = [jnp.exp(s - mn) for s, mn in zip(ss, m_news)]
    for h in range(N_HEAD):
        l_ref[h] = alphas[h] * l_ref[h] + jnp.sum(jnp.sum(ps[h], axis=-1, keepdims=True), axis=0, keepdims=True)
        m_ref[h] = m_news[h]
    for h in range(N_HEAD):
        acc = alphas[h] * acc_ref[heads[h], :]
        for j in range(g):
            acc = acc + v_refs[j][0, heads[h], :] * ps[h][j:j + 1, :]
        acc_ref[heads[h], :] = acc

    @pl.when(step == pl.num_programs(1) - 1)
    def _():
        for h in range(N_HEAD):
            o_ref[0, heads[h], :] = jnp.sum(acc_ref[heads[h], :], axis=-1, keepdims=True) / l_ref[h]


def fox_paged(page_table, q, knew, vnew, lfnew, kpool, vpool, lfpool, g):
    nb, npages = page_table.shape
    last = npages - 1
    cur = lambda b, i, pt: (b, 0, 0)
    pool = lambda j: (lambda b, i, pt: (pt[b, last - (i * g + j)], 0, 0))
    col = pl.BlockSpec((1, 512, 1), cur)
    return pl.pallas_call(
        functools.partial(_paged_kernel, g=g),
        grid_spec=pltpu.PrefetchScalarGridSpec(
            num_scalar_prefetch=1,
            grid=(nb, npages // g),
            in_specs=([col, col, col, pl.BlockSpec((1, N_HEAD, 1), cur)]
                      + [pl.BlockSpec((1, 512, PAGE), pool(j)) for j in range(g)]
                      + [pl.BlockSpec((1, 512, PAGE), pool(j)) for j in range(g)]
                      + [pl.BlockSpec((1, N_HEAD, PAGE), pool(j)) for j in range(g)]),
            out_specs=col,
            scratch_shapes=[pltpu.VMEM((N_HEAD, 1, 1), f32), pltpu.VMEM((N_HEAD, 1, 1), f32),
                            pltpu.VMEM((512, PAGE), f32), pltpu.VMEM((N_HEAD, 1), f32)]),
        out_shape=jax.ShapeDtypeStruct((nb, 512, 1), f32),
        compiler_params=_params(("parallel", "arbitrary"), VMEM_LIMIT),
        name="fox_paged",
    )(page_table, q, knew, vnew, lfnew, *([kpool] * g), *([vpool] * g), *([lfpool] * g))


def _merge_update(or_ref, of_ref, ug_ref, wr_ref, wf_ref, wo_ref):
    br = _dot(or_ref[...].astype(bf16), wr_ref[...])
    bf = _dot(of_ref[...].astype(bf16), wf_ref[...])
    ug = ug_ref[...]
    merged = jax.nn.sigmoid(ug[:, :D_MODEL]) * br + jax.nn.sigmoid(ug[:, D_MODEL:]) * bf
    return _dot(merged.astype(bf16), wo_ref[...])


def _merge_kernel(x_ref, or_ref, of_ref, ug_ref, wr_ref, wf_ref, wo_ref, o_ref):
    o_ref[...] = x_ref[...] + _merge_update(or_ref, of_ref, ug_ref, wr_ref, wf_ref, wo_ref)


def _merge_seq_kernel(x_ref, head_ref, or_ref, of_ref, ug_ref, wr_ref, wf_ref, wo_ref, o_ref, *, tiles_per_seq):
    x = _padded_rows(x_ref, head_ref, pl.program_id(0) % tiles_per_seq == 0)
    o_ref[...] = x + _merge_update(or_ref, of_ref, ug_ref, wr_ref, wf_ref, wo_ref)


def merge_out(x, o_r, o_f, u, wr, wf, wo, tm, head=None):
    m = o_r.shape[0]
    row = lambda i: (i, 0)
    cst = lambda i: (0, 0)
    rest = [pl.BlockSpec((tm, 512), row), pl.BlockSpec((tm, 512), row),
            pl.BlockSpec((tm, 2 * D_MODEL), lambda i: (i, COL_G // (2 * D_MODEL))),
            pl.BlockSpec((512, D_MODEL), cst), pl.BlockSpec((512, D_MODEL), cst),
            pl.BlockSpec((D_MODEL, D_MODEL), cst)]
    if head is None:
        body, first, args = _merge_kernel, [pl.BlockSpec((tm, D_MODEL), row)], (x,)
    else:
        tps = (x.shape[1] + ROW0) // tm
        body = functools.partial(_merge_seq_kernel, tiles_per_seq=tps)
        first = [_seq_tile_spec(tm, D_MODEL, tps), pl.BlockSpec((ROW0, D_MODEL), cst)]
        args = (x, head)
    return pl.pallas_call(
        body,
        grid=(m // tm,),
        in_specs=first + rest,
        out_specs=pl.BlockSpec((tm, D_MODEL), row),
        out_shape=jax.ShapeDtypeStruct((m, D_MODEL), f32),
        compiler_params=_params(("parallel",), VMEM_LIMIT),
        name="merge_out",
    )(*args, o_r, o_f, u, wr, wf, wo)


def _ffn_kernel(x_ref, g2_ref, gf_ref, wu_ref, wd_ref, o_ref, h_ref, acc_ref, *, tiles_per_seq):
    j = pl.program_id(1)

    @pl.when(j == 0)
    def _():
        h_ref[...] = _rms(x_ref[...], g2_ref[...]).astype(bf16)
        acc_ref[...] = jnp.zeros_like(acc_ref)

    hid = jnp.maximum(_dot(h_ref[...], wu_ref[...]), 0.0)
    acc_ref[...] += _dot((hid * hid).astype(bf16), wd_ref[...])

    @pl.when(j == pl.num_programs(1) - 1)
    def _():
        y = _rms(x_ref[...] + acc_ref[...], gf_ref[...])
        if tiles_per_seq is None:
            o_ref[...] = y
        else:
            first = pl.program_id(0) % tiles_per_seq == 0
            o_ref[0] = jnp.where(first, pltpu.roll(y, y.shape[0] - ROW0, 0), y)


def ffn_final(x, g2, gf, wu, wd, tm, tf, seq_out=None):
    m = x.shape[0]
    if seq_out is None:
        tps = None
        out_spec = pl.BlockSpec((tm, D_MODEL), lambda i, j: (i, 0))
        out_shape = jax.ShapeDtypeStruct((m, D_MODEL), f32)
    else:
        tps = (seq_out[1] + ROW0) // tm
        out_spec = _seq_tile_spec(tm, D_MODEL, tps)
        out_shape = jax.ShapeDtypeStruct((seq_out[0], seq_out[1], D_MODEL), f32)
    return pl.pallas_call(
        functools.partial(_ffn_kernel, tiles_per_seq=tps),
        grid=(m // tm, D_FF // tf),
        in_specs=[pl.BlockSpec((tm, D_MODEL), lambda i, j: (i, 0)),
                  pl.BlockSpec((1, D_MODEL), lambda i, j: (0, 0)), pl.BlockSpec((1, D_MODEL), lambda i, j: (0, 0)),
                  pl.BlockSpec((D_MODEL, tf), lambda i, j: (0, j)), pl.BlockSpec((tf, D_MODEL), lambda i, j: (j, 0))],
        out_specs=out_spec,
        out_shape=out_shape,
        scratch_shapes=[pltpu.VMEM((tm, D_MODEL), bf16), pltpu.VMEM((tm, D_MODEL), f32)],
        compiler_params=_params(("arbitrary", "arbitrary"), VMEM_LIMIT),
        name="ffn_final",
    )(x, g2, gf, wu, wd)


def kernel(x_prompt, x_sample, state_shift, state_wkv, cache_k, cache_v, cache_logf, page_table, meta_tokens,
           norm_mix, w_in, mu_shift, w0, w_decay_up, a0, w_a_up, w_g_up, k_k, k_a, r_k, ln_x_w, ln_x_b, b_forget,
           w_br_rwkv, w_br_fox, w_out, norm_ffn, w_ffn_up, w_ffn_down, norm_final):
    assert norm_mix.shape[0] == 1, "single layer"
    nb, seq, _ = x_prompt.shape
    ns = x_sample.shape[0]
    tp = FRONT_PAD + N_META + seq
    tlen = N_META + seq

    wi = w_in[0]
    wi = jnp.concatenate([wi[:, 0:1792], wi[:, 3328:3336], jnp.zeros((D_MODEL, 248), f32),
                          wi[:, 3336:5384], wi[:, 1792:3328]], axis=1).astype(bf16)
    row2 = lambda a: a.reshape(1, -1)
    lane_head = jnp.arange(512, dtype=jnp.int32) // HEAD
    pr = dict(mu=row2(mu_shift[0]), w0=row2(w0[0]), wd=w_decay_up[0], a0=row2(a0[0]), wa=w_a_up[0], wg=w_g_up[0],
              k_k=row2(k_k[0]), k_a=row2(k_a[0]), r_k=row2(r_k[0]),
              bd=(lane_head[:, None] == lane_head[None, :]).astype(bf16))
    lnw, lnb = row2(ln_x_w[0]), row2(ln_x_b[0])
    g_mix, g_ffn, g_fin = row2(norm_mix[0]), row2(norm_ffn[0]), row2(norm_final)
    wr, wf, wo = w_br_rwkv[0].astype(bf16), w_br_fox[0].astype(bf16), w_out[0].astype(bf16)
    wu, wdn = w_ffn_up[0].astype(bf16), w_ffn_down[0].astype(bf16)

    head = jnp.concatenate([jnp.zeros((FRONT_PAD, D_MODEL), f32), meta_tokens], axis=0)
    u2 = norm_matmul_seq(x_prompt, head, g_mix, wi, TM_PROJ, TN_PROJ)
    u3 = u2.reshape(nb, tp, D_INP)
    us2 = norm_matmul(x_sample.reshape(ns, D_MODEL), g_mix, wi, ns, TN_PROJ)

    r, lw, k2, v, kkn, b, g, bonus = rwkv_prep(u3, None, pr, TM_PREP)
    maps = wkv_chunks(r, lw, k2, v, kkn, b, CHUNKS_PER_STEP)
    maps = maps.reshape(nb * N_HEAD, tp // CHUNK, 2 * CHUNK, 2 * HEAD)
    o_r, s_kv = wkv_serial(maps, g, bonus, lnw, lnb, CHUNKS_PER_STEP)
    new_wkv_p = jnp.swapaxes(s_kv, 1, 2).reshape(1, nb, N_HEAD, HEAD, HEAD)

    us3 = us2.reshape(1, ns, D_INP)
    sp = rwkv_prep(us3, state_shift[0].reshape(1, ns, D_SHIFT), pr, ns)
    rs, lws, ks, vs, kks, bs, gs, bonus_s = sp
    rowify = lambda a: a.reshape(ns, N_HEAD, 1, HEAD)
    s_new, ys = wkv_step(state_wkv[0], rowify(rs), rowify(lws), rowify(ks), rowify(kks), rowify(bs),
                         rowify(vs).reshape(ns, N_HEAD, HEAD, 1))
    ys_hm = jnp.transpose(ys.reshape(ns, N_HEAD, HEAD), (1, 0, 2))[None]
    o_r_s = rwkv_post(ys_hm, gs, bonus_s, lnw, lnb, ns)[0]

    bias_rows = jnp.tile(b_forget[0], nb).reshape(nb * N_HEAD, 1)
    lf_t, c_t = logf_cumsum(u3, bias_rows)
    pad_key = jnp.arange(tp, dtype=jnp.int32)[None, :] < FRONT_PAD
    ck = jnp.where(pad_key, -NEG, c_t * LOG2E).reshape(nb, N_HEAD // 2, 2, tp)
    o_f = fox_prompt(u3, jnp.pad(ck, ((0, 0), (0, 0), (0, 6), (0, 0))), TQ_FOX, NSUB_FOX)

    qs = us2[:, COL_Q:COL_Q + 512] * (HEAD ** -0.5)
    k_s = us2[:, COL_Q + 512:COL_Q + 1024]
    v_s = us2[:, COL_Q + 1024:COL_Q + 1536]
    lf_s = jax.nn.log_sigmoid(us2[:, COL_F:COL_F + N_HEAD] + b_forget[0][None, :])
    n_pool = cache_k.shape[1]
    kpool = jnp.transpose(cache_k[0], (0, 2, 3, 1)).reshape(n_pool, 512, PAGE)
    vpool = jnp.transpose(cache_v[0], (0, 2, 3, 1)).reshape(n_pool, 512, PAGE)
    o_f_s = fox_paged(page_table, qs.reshape(ns, 512, 1), k_s.reshape(ns, 512, 1), v_s.reshape(ns, 512, 1),
                      lf_s.reshape(ns, N_HEAD, 1), kpool, vpool, jnp.swapaxes(cache_logf[0], 1, 2),
                      PAGES_PER_STEP).reshape(ns, 512)

    x1 = merge_out(x_prompt, o_r.reshape(nb * tp, 512), o_f.reshape(nb * tp, 512), u2, wr, wf, wo, TM_MERGE, head)
    y_prompt = ffn_final(x1, g_ffn, g_fin, wu, wdn, TM_FFN, TF_FFN, (nb, seq))
    x1s = merge_out(x_sample.reshape(ns, D_MODEL), o_r_s, o_f_s, us2, wr, wf, wo, ns)
    ysamp = ffn_final(x1s, g_ffn, g_fin, wu, wdn, ns, TF_FFN)

    y_sample = ysamp.reshape(ns, 1, D_MODEL)
    new_shift_p = u3[:, tp - 1, :D_SHIFT][None]
    k_p = u3[:, FRONT_PAD:, COL_Q + 512:COL_Q + 1024].reshape(1, nb, tlen, N_HEAD, HEAD)
    v_p = u3[:, FRONT_PAD:, COL_Q + 1024:COL_Q + 1536].reshape(1, nb, tlen, N_HEAD, HEAD)
    lf_p = jnp.transpose(lf_t.reshape(nb, N_HEAD, tp), (0, 2, 1))[:, FRONT_PAD:][None]
    return (y_prompt, y_sample, new_shift_p, new_wkv_p, k_p, v_p, lf_p,
            us2[:, :D_SHIFT][None], s_new[None], k_s.reshape(1, ns, 1, N_HEAD, HEAD),
            v_s.reshape(1, ns, 1, N_HEAD, HEAD), lf_s.reshape(1, ns, 1, N_HEAD))
```

```python
import functools

import jax
import jax.numpy as jnp
from jax import lax
from jax.experimental import pallas as pl
from jax.experimental.pallas import tpu as pltpu

f32 = jnp.float32
bf16 = jnp.bfloat16

D_MODEL = 1024
N_META = 16
D_RWKV = 512
HEAD = 64
N_HEAD = 8
D_SHIFT = 1792
D_FF = 4096
RMS_EPS = 1e-6
GN_EPS = 64e-5
PAGE = 128

FRONT_PAD = 112
ROW0 = FRONT_PAD + N_META
CHUNK = 64
NEG = -1e30
LOG2E = 1.4426950408889634

COL_R = 0
COL_F = 1792
COL_G = 2048
COL_Q = 4096
D_INP = 5632

VMEM_LIMIT = 56 * 1024 * 1024

TM_PROJ, TN_PROJ = 1664, 1408
TM_PREP = 320
CHUNKS_PER_STEP = 5
TQ_FOX, NSUB_FOX = 1664, 8
PAGES_PER_STEP = 32
TM_MERGE = 640
TM_FFN, TF_FFN = 1024, 1024
STEP_ROWS = 4

NN = (((1,), (0,)), ((), ()))
NT = (((1,), (1,)), ((), ()))
TN = (((0,), (0,)), ((), ()))


def _dot(a, b, dims=NN):
    return lax.dot_general(a, b, dims, preferred_element_type=f32)


def _split2(x):
    hi = x.astype(bf16)
    lo = (x - hi.astype(f32)).astype(bf16)
    return hi, lo


def _split3(x):
    hi = x.astype(bf16)
    r1 = x - hi.astype(f32)
    mid = r1.astype(bf16)
    lo = (r1 - mid.astype(f32)).astype(bf16)
    return hi, mid, lo


def _dot3(a, b, dims=NN):
    ah, al = _split2(a)
    bh, bl = _split2(b)
    return _dot(ah, bh, dims) + (_dot(ah, bl, dims) + _dot(al, bh, dims))


def _dot1(a, b, dims=NN):
    return _dot(a.astype(bf16), b.astype(bf16), dims)


def _dot_onesr(x, ones_bf16, dims=NN):
    hi, mid, lo = _split3(x)
    return _dot(hi, ones_bf16, dims) + (_dot(mid, ones_bf16, dims) + _dot(lo, ones_bf16, dims))


def _segsum(x, ones_bf16):
    hi, lo = _split2(x)
    return _dot(hi, ones_bf16) + _dot(lo, ones_bf16)


def _dot_onesl(ones_bf16, x, dims=NN):
    hi, mid, lo = _split3(x)
    return _dot(ones_bf16, hi, dims) + (_dot(ones_bf16, mid, dims) + _dot(ones_bf16, lo, dims))


def _params(sem, vmem=None):
    return pltpu.CompilerParams(dimension_semantics=sem, vmem_limit_bytes=vmem)


def _rms(x, g):
    ms = jnp.mean(x * x, axis=-1, keepdims=True)
    return x * lax.rsqrt(ms + RMS_EPS) * g


def _norm_matmul_kernel(x_ref, g_ref, w_ref, o_ref, h_ref):
    @pl.when(pl.program_id(1) == 0)
    def _():
        h_ref[...] = _rms(x_ref[...], g_ref[...]).astype(bf16)

    o_ref[...] = _dot(h_ref[...], w_ref[...])


def norm_matmul(x, g, w, tm, tn):
    m, d = x.shape
    n = w.shape[1]
    return pl.pallas_call(
        _norm_matmul_kernel,
        grid=(m // tm, n // tn),
        in_specs=[pl.BlockSpec((tm, d), lambda i, j: (i, 0)),
                  pl.BlockSpec((1, d), lambda i, j: (0, 0)),
                  pl.BlockSpec((d, tn), lambda i, j: (0, j))],
        out_specs=pl.BlockSpec((tm, tn), lambda i, j: (i, j)),
        out_shape=jax.ShapeDtypeStruct((m, n), f32),
        scratch_shapes=[pltpu.VMEM((tm, d), bf16)],
        compiler_params=_params(("parallel", "arbitrary"), VMEM_LIMIT),
        name="norm_matmul",
    )(x, g, w)


def _seq_tile_spec(tm, d, tiles_per_seq):
    def index(i, *_):
        start = jnp.maximum((i % tiles_per_seq) * tm - ROW0, 0)
        return (i // tiles_per_seq, pl.multiple_of(start, 128), 0)
    return pl.BlockSpec((pl.Element(1), pl.Element(tm), pl.Element(d)), index)


def _padded_rows(x_ref, head_ref, first):
    x = x_ref[0]
    rowi = lax.broadcasted_iota(jnp.int32, x.shape, 0)
    head = jnp.concatenate([head_ref[...], jnp.zeros((x.shape[0] - ROW0, x.shape[1]), f32)], axis=0)
    return jnp.where(first, jnp.where(rowi < ROW0, head, pltpu.roll(x, ROW0, 0)), x)


def _norm_matmul_seq_kernel(x_ref, head_ref, g_ref, w_ref, o_ref, h_ref, *, tiles_per_seq):
    @pl.when(pl.program_id(1) == 0)
    def _():
        x = _padded_rows(x_ref, head_ref, pl.program_id(0) % tiles_per_seq == 0)
        h_ref[...] = _rms(x, g_ref[...]).astype(bf16)

    o_ref[...] = _dot(h_ref[...], w_ref[...])


def norm_matmul_seq(x3, head, g, w, tm, tn):
    nb, seq, d = x3.shape
    tps = (seq + ROW0) // tm
    n = w.shape[1]
    return pl.pallas_call(
        functools.partial(_norm_matmul_seq_kernel, tiles_per_seq=tps),
        grid=(nb * tps, n // tn),
        in_specs=[_seq_tile_spec(tm, d, tps),
                  pl.BlockSpec((ROW0, d), lambda i, j: (0, 0)),
                  pl.BlockSpec((1, d), lambda i, j: (0, 0)),
                  pl.BlockSpec((d, tn), lambda i, j: (0, j))],
        out_specs=pl.BlockSpec((tm, tn), lambda i, j: (i, j)),
        out_shape=jax.ShapeDtypeStruct((nb * tps * tm, n), f32),
        scratch_shapes=[pltpu.VMEM((tm, d), bf16)],
        compiler_params=_params(("parallel", "arbitrary"), VMEM_LIMIT),
        name="norm_matmul_seq",
    )(x3, head, g, w)


def _prep_kernel(u_ref, up_ref, mu_ref, w0_ref, wd_ref, a0_ref, wa_ref, wg_ref, kk_ref, ka_ref, rk_ref,
                 bd_ref, r_o, lw_o, k_o, v_o, kkn_o, b_o, g_o, bonus_o, *, prev_is_tail):
    u = u_ref[0]
    if prev_is_tail:
        first = jnp.where(pl.program_id(1) == 0, 0.0, up_ref[0, 7:8, :])
        rowi = lax.broadcasted_iota(jnp.int32, u.shape, 0)
        up = jnp.where(rowi == 0, first, pltpu.roll(u, 1, 0))
    else:
        up = up_ref[0]
    us = u + mu_ref[...] * (up - u)
    r = us[:, 0:512]
    k = us[:, 512:1024]
    v = us[:, 1024:1536]
    zw = us[:, 1536:1600]
    za = us[:, 1600:1664]
    zg = us[:, 1664:1792]
    bd = bd_ref[...]

    z = -(w0_ref[...] + _dot3(jnp.tanh(zw), wd_ref[...]))
    w_raw = -(jnp.maximum(z, 0.0) + jnp.log1p(jnp.exp(-jnp.abs(z)))) - 0.5
    lw = -jnp.exp(w_raw)
    a = jax.nn.sigmoid(a0_ref[...] + _dot3(za, wa_ref[...]))
    g = _dot3(jax.nn.sigmoid(zg), wg_ref[...])
    kk = k * kk_ref[...]
    ss = _segsum(kk * kk, bd)
    kkn = kk / jnp.maximum(jnp.sqrt(ss), 1e-12)
    k2 = k * (1.0 + (a - 1.0) * ka_ref[...])
    b = kkn * a
    bonus = _segsum(r * k2 * rk_ref[...], bd) * v

    r_o[0] = r
    lw_o[0] = lw
    k_o[0] = k2
    v_o[0] = v
    kkn_o[0] = kkn
    b_o[0] = b
    g_o[0] = g
    bonus_o[0] = bonus


def rwkv_prep(u3, up3, pr, tm):
    nb, t = u3.shape[0], u3.shape[1]
    row = lambda i, j: (i, j, 0)
    if up3 is None:
        up_arr = u3
        up_spec = pl.BlockSpec((1, 8, D_SHIFT), lambda i, j: (i, jnp.maximum(j * (tm // 8) - 1, 0), 0))
    else:
        up_arr = up3
        up_spec = pl.BlockSpec((1, tm, D_SHIFT), row)
    cst = lambda i, j: (0, 0)
    out = jax.ShapeDtypeStruct((nb, t, D_RWKV), f32)
    out_spec = pl.BlockSpec((1, tm, D_RWKV), row)
    vec = lambda n: pl.BlockSpec((1, n), cst)
    return pl.pallas_call(
        functools.partial(_prep_kernel, prev_is_tail=up3 is None),
        grid=(nb, t // tm),
        in_specs=[pl.BlockSpec((1, tm, D_SHIFT), row), up_spec,
                  vec(D_SHIFT), vec(512), pl.BlockSpec((64, 512), cst), vec(512), pl.BlockSpec((64, 512), cst),
                  pl.BlockSpec((128, 512), cst), vec(512), vec(512), vec(512), pl.BlockSpec((512, 512), cst)],
        out_specs=[out_spec] * 8,
        out_shape=[out] * 8,
        compiler_params=_params(("parallel", "parallel"), VMEM_LIMIT),
        name="rwkv_prep",
    )(u3, up_arr, pr["mu"], pr["w0"], pr["wd"], pr["a0"], pr["wa"], pr["wg"], pr["k_k"], pr["k_a"], pr["r_k"],
      pr["bd"])


def _each(f, *lists):
    return [f(*xs) for xs in zip(*lists)]


def _wkv_chunk_kernel(r_ref, lw_ref, k_ref, v_ref, kk_ref, b_ref, m_ref, *, nc):
    ii = lax.broadcasted_iota(jnp.int32, (CHUNK, CHUNK), 0)
    jj = lax.broadcasted_iota(jnp.int32, (CHUNK, CHUNK), 1)
    incl = jj <= ii
    strict = jj < ii
    ones_incl = incl.astype(bf16)
    same_blk = (ii >> 4) == (jj >> 4)
    eye = (ii == jj).astype(f32)
    units = [(c, h) for c in range(nc) for h in range(N_HEAD)]
    tile = lambda ref: [ref[0, c * CHUNK:(c + 1) * CHUNK, h * HEAD:(h + 1) * HEAD] for c, h in units]
    r, lw, k, v, kk, b = (tile(ref) for ref in (r_ref, lw_ref, k_ref, v_ref, kk_ref, b_ref))

    gcum = _each(lambda x: _dot_onesl(ones_incl, x), lw)
    gend = _each(lambda x: x[CHUNK - 1:CHUNK, :], gcum)
    kkt = _each(lambda x, gc, l: x * jnp.exp(gc - l), kk, gcum, lw)
    rt = _each(lambda x, gc: x * jnp.exp(gc), r, gcum)
    em = _each(lambda gc: jnp.exp(-gc), gcum)
    kh = _each(jnp.multiply, k, em)
    bh = _each(jnp.multiply, b, em)
    ec = _each(lambda ge, gc: jnp.exp(ge - gc), gend, gcum)
    kg = _each(jnp.multiply, k, ec)
    bg = _each(jnp.multiply, b, ec)

    lhs = _each(lambda x, y: jnp.concatenate([x, y], axis=0), kkt, rt)
    ab = _each(lambda x, y: _dot1(x, y, NT), lhs, bh)
    ak = _each(lambda x, y: _dot1(x, y, NT), lhs, kh)
    a_kb = _each(lambda x: jnp.where(strict, x[:CHUNK], 0.0), ab)
    a_rb = _each(lambda x: jnp.where(incl, x[CHUNK:], 0.0), ab)
    a_kk = _each(lambda x: jnp.where(strict, x[:CHUNK], 0.0), ak)
    a_rk = _each(lambda x: jnp.where(incl, x[CHUNK:], 0.0), ak)

    dg = _each(lambda x: jnp.where(same_blk, x, 0.0), a_kb)
    lo = _each(jnp.subtract, a_kb, dg)
    n1 = _each(jnp.negative, dg)
    n2 = _each(lambda x: _dot1(x, x), n1)
    n4 = _each(lambda x: _dot1(x, x), n2)
    n8 = _each(lambda x: _dot1(x, x), n4)
    t12 = _each(lambda x, y: _dot1(eye + x, eye + y), n1, n2)
    t48 = _each(lambda x, y: _dot1(eye + x, eye + y), n4, n8)
    td = _each(_dot1, t12, t48)
    x1 = _each(lambda x, y: -_dot1(x, y), td, lo)
    x2 = _each(lambda x: _dot1(x, x), x1)
    xx = _each(lambda x, y: _dot1(eye + x, eye + y), x1, x2)
    tinv = _each(_dot1, xx, td)

    akv = _each(_dot1, a_kk, v)
    w1u = _each(lambda t, x, y: _dot1(t, jnp.concatenate([x, y], axis=1)), tinv, kkt, akv)
    z = _each(lambda x, y, w: _dot1(jnp.concatenate([x.T, y], axis=0), w), bg, a_rb, w1u)
    kv = _each(lambda x, y, w: _dot1(jnp.concatenate([x.T, y], axis=0), w), kg, a_rk, v)
    for i, (c, h) in enumerate(units):
        base = jnp.concatenate([eye * jnp.exp(gend[i]), rt[i]], axis=0)
        m_ref[0, h, c] = jnp.concatenate([base, kv[i]], axis=1) - z[i]


def wkv_chunks(r, lw, k, v, kk, b, nc):
    nb, t, _ = r.shape
    nchunk = t // CHUNK
    in_spec = pl.BlockSpec((1, nc * CHUNK, D_RWKV), lambda i, j: (i, j, 0))
    out_spec = pl.BlockSpec((1, N_HEAD, nc, 2 * CHUNK, 2 * HEAD), lambda i, j: (i, 0, j, 0, 0))
    out = jax.ShapeDtypeStruct((nb, N_HEAD, nchunk, 2 * CHUNK, 2 * HEAD), f32)
    return pl.pallas_call(
        functools.partial(_wkv_chunk_kernel, nc=nc),
        grid=(nb, nchunk // nc),
        in_specs=[in_spec] * 6,
        out_specs=out_spec,
        out_shape=out,
        compiler_params=_params(("parallel", "parallel"), VMEM_LIMIT),
        name="wkv_chunks",
    )(r, lw, k, v, kk, b)


def _group_norm(y):
    mu = jnp.mean(y, axis=-1, keepdims=True)
    yc = y - mu
    var = jnp.mean(yc * yc, axis=-1, keepdims=True)
    return yc * lax.rsqrt(var + GN_EPS)


def _wkv_serial_kernel(m_ref, g_ref, bonus_ref, lnw_ref, lnb_ref, o_ref, s_ref, st_ref, *, nb, nc):
    @pl.when(pl.program_id(0) == 0)
    def _():
        st_ref[...] = jnp.zeros_like(st_ref)

    nbh = nb * N_HEAD
    st = [st_ref[i] for i in range(nbh)]
    ys = []
    for c in range(nc):
        zs = [_dot1(m_ref[i, c, :, :HEAD], st[i]) + m_ref[i, c, :, HEAD:] for i in range(nbh)]
        st = [z[:CHUNK] for z in zs]
        ys += [z[CHUNK:] for z in zs]
    for i in range(nbh):
        st_ref[i] = st[i]

    mus = [jnp.mean(y, axis=-1, keepdims=True) for y in ys]
    ycs = _each(jnp.subtract, ys, mus)
    vrs = [jnp.mean(yc * yc, axis=-1, keepdims=True) for yc in ycs]
    yns = _each(lambda yc, vr: yc * lax.rsqrt(vr + GN_EPS), ycs, vrs)
    for c in range(nc):
        rows = slice(c * CHUNK, (c + 1) * CHUNK)
        for bi in range(nb):
            yn = jnp.concatenate(yns[c * nbh + bi * N_HEAD:c * nbh + (bi + 1) * N_HEAD], axis=-1)
            o_ref[bi, rows, :] = (yn * lnw_ref[...] + lnb_ref[...] + bonus_ref[bi, rows, :]) * g_ref[bi, rows, :]

    @pl.when(pl.program_id(0) == pl.num_programs(0) - 1)
    def _():
        s_ref[...] = st_ref[...]


def wkv_serial(maps, g, bonus, lnw, lnb, nc):
    nbh, nchunk = maps.shape[0], maps.shape[1]
    nb = g.shape[0]
    blk = pl.BlockSpec((nbh, nc, 2 * CHUNK, 2 * HEAD), lambda j: (0, j, 0, 0))
    tok = pl.BlockSpec((nb, nc * CHUNK, D_RWKV), lambda j: (0, j, 0))
    par = pl.BlockSpec((1, D_RWKV), lambda j: (0, 0))
    return pl.pallas_call(
        functools.partial(_wkv_serial_kernel, nb=nb, nc=nc),
        grid=(nchunk // nc,),
        in_specs=[blk, tok, tok, par, par],
        out_specs=[tok, pl.BlockSpec((nbh, HEAD, HEAD), lambda j: (0, 0, 0))],
        out_shape=[jax.ShapeDtypeStruct((nb, nchunk * CHUNK, D_RWKV), f32),
                   jax.ShapeDtypeStruct((nbh, HEAD, HEAD), f32)],
        scratch_shapes=[pltpu.VMEM((nbh, HEAD, HEAD), f32)],
        compiler_params=_params(("arbitrary",), VMEM_LIMIT),
        name="wkv_serial",
    )(maps, g, bonus, lnw, lnb)


def _wkv_step_kernel(s_ref, r_ref, lw_ref, k_ref, kk_ref, b_ref, vc_ref, so_ref, y_ref):
    s = s_ref[...]
    skk = jnp.sum(s * kk_ref[...], axis=-1, keepdims=True)
    s1 = s * jnp.exp(lw_ref[...]) - skk * b_ref[...] + vc_ref[...] * k_ref[...]
    so_ref[...] = s1
    y_ref[...] = jnp.sum(s1 * r_ref[...], axis=-1, keepdims=True)


def wkv_step(s, r, lw, k, kk, b, vcol):
    n = s.shape[0]
    row = pl.BlockSpec((STEP_ROWS, N_HEAD, 1, HEAD), lambda i: (i, 0, 0, 0))
    col = pl.BlockSpec((STEP_ROWS, N_HEAD, HEAD, 1), lambda i: (i, 0, 0, 0))
    mat = pl.BlockSpec((STEP_ROWS, N_HEAD, HEAD, HEAD), lambda i: (i, 0, 0, 0))
    return pl.pallas_call(
        _wkv_step_kernel,
        grid=(n // STEP_ROWS,),
        in_specs=[mat, row, row, row, row, row, col],
        out_specs=[mat, col],
        out_shape=[jax.ShapeDtypeStruct(s.shape, f32), jax.ShapeDtypeStruct((n, N_HEAD, HEAD, 1), f32)],
        compiler_params=_params(("parallel",)),
        name="wkv_step",
    )(s, r, lw, k, kk, b, vcol)


def _post_kernel(y_ref, g_ref, bonus_ref, lnw_ref, lnb_ref, o_ref):
    yn = jnp.concatenate([_group_norm(y_ref[0, h]) for h in range(N_HEAD)], axis=-1)
    o_ref[0] = (yn * lnw_ref[...] + lnb_ref[...] + bonus_ref[0]) * g_ref[0]


def rwkv_post(y, g, bonus, lnw, lnb, tm):
    nb, _, t, _ = y.shape
    hm = pl.BlockSpec((1, N_HEAD, tm, HEAD), lambda i, j: (i, 0, j, 0))
    tok = pl.BlockSpec((1, tm, D_RWKV), lambda i, j: (i, j, 0))
    par = pl.BlockSpec((1, D_RWKV), lambda i, j: (0, 0))
    return pl.pallas_call(
        _post_kernel,
        grid=(nb, t // tm),
        in_specs=[hm, tok, tok, par, par],
        out_specs=tok,
        out_shape=jax.ShapeDtypeStruct((nb, t, D_RWKV), f32),
        compiler_params=_params(("parallel", "parallel"), VMEM_LIMIT),
        name="rwkv_post",
    )(y, g, bonus, lnw, lnb)


def _logf_kernel(u_ref, b_ref, lf_ref, c_ref, *, nblk):
    ii = lax.broadcasted_iota(jnp.int32, (128, 128), 0)
    jj = lax.broadcasted_iota(jnp.int32, (128, 128), 1)
    upper = (ii <= jj).astype(bf16)
    nb = u_ref.shape[0]
    carry = jnp.zeros((nb * N_HEAD, 1), f32)
    for blk in range(nblk):
        sl = slice(128 * blk, 128 * (blk + 1))
        fl = jnp.concatenate([u_ref[bi, sl, :].T[:N_HEAD] for bi in range(nb)], axis=0)
        lf = jax.nn.log_sigmoid(fl + b_ref[...])
        lf_ref[:, sl] = lf
        cs = _dot_onesr(lf, upper) + carry
        c_ref[:, sl] = cs
        carry = cs[:, 127:128]


def logf_cumsum(u3, bias):
    nb, t, _ = u3.shape
    out = jax.ShapeDtypeStruct((nb * N_HEAD, t), f32)
    return pl.pallas_call(
        functools.partial(_logf_kernel, nblk=t // 128),
        grid=(1,),
        in_specs=[pl.BlockSpec((nb, t, 128), lambda i: (0, 0, COL_F // 128)),
                  pl.BlockSpec((nb * N_HEAD, 1), lambda i: (0, 0))],
        out_specs=[pl.BlockSpec((nb * N_HEAD, t), lambda i: (0, 0))] * 2,
        out_shape=[out, out],
        compiler_params=_params(("arbitrary",), VMEM_LIMIT),
        name="logf_cumsum",
    )(u3, bias)


SUM_ROWS = 16


def _fox_kernel(qi_ref, ki_ref, q_ref, k_ref, v_ref, ck_ref, o_ref, m_ref, acc_ref, *, tq, tk, nsub):
    step_id = pl.program_id(2)
    qi = qi_ref[step_id]
    ki = ki_ref[step_id]

    @pl.when(ki == 0)
    def _():
        m_ref[...] = jnp.full_like(m_ref, NEG)
        acc_ref[...] = jnp.zeros_like(acc_ref)

    def step(masked):
        lane_lo = lax.broadcasted_iota(jnp.int32, (tq, 2 * HEAD), 1) < HEAD
        q = q_ref[0] * (HEAD ** -0.5 * LOG2E)
        qes = [jnp.where(lane_lo == (e == 0), q, 0.0).astype(bf16) for e in range(2)]
        ts = tk // nsub
        ckt = ck_ref[0, 0].T

        def first_col(sub):
            return (sub * ts) // 128 * 128 if masked else 0

        def scores(sub):
            rs = slice(sub * ts, (sub + 1) * ts)
            lo = first_col(sub)
            kb = k_ref[0, rs, :].astype(bf16)
            return [_dot(kb, qes[e][lo:, :], NT) for e in range(2)]

        s_cur = scores(0)
        for sub in range(nsub):
            s_next = scores(sub + 1) if sub + 1 < nsub else None
            rs = slice(sub * ts, (sub + 1) * ts)
            lo = first_col(sub)
            vt = v_ref[0, rs, :].T.astype(bf16)
            ones = jnp.ones((SUM_ROWS, ts), bf16)
            ps, alphas = [], []
            for e in range(2):
                s = s_cur[e] - ckt[rs, e:e + 1]
                if masked:
                    row = lax.broadcasted_iota(jnp.int32, s.shape, 0) + sub * ts
                    col = lax.broadcasted_iota(jnp.int32, s.shape, 1) + lo
                    s = jnp.where(row <= col, s, NEG)
                m_old = m_ref[e, :, lo:]
                m_new = jnp.maximum(m_old, jnp.max(s, axis=0, keepdims=True))
                alphas.append(jnp.exp2(m_old - m_new))
                ps.append(jnp.exp2(s - m_new).astype(bf16))
                m_ref[e, :, lo:] = m_new
            for e in range(2):
                vte = jnp.concatenate([vt[HEAD * e:HEAD * (e + 1), :], ones], axis=0)
                acc_ref[e, :, lo:] = alphas[e] * acc_ref[e, :, lo:] + _dot(vte, ps[e])
            s_cur = s_next

    @pl.when(ki < qi)
    def _():
        step(False)

    @pl.when(ki == qi)
    def _():
        step(True)
        o = [acc_ref[e, 0:HEAD, :] / acc_ref[e, HEAD:HEAD + 1, :] for e in range(2)]
        o_ref[0] = jnp.concatenate(o, axis=0).T


def fox_prompt(u3, ck, tq, nsub):
    nb, t, _ = u3.shape
    nq = t // tq
    cq, ckk, cv = COL_Q // 128, (COL_Q + 512) // 128, (COL_Q + 1024) // 128
    pairs = [(i, j) for i in range(nq) for j in range(i + 1)]
    qi_tab = jnp.asarray([p[0] for p in pairs], jnp.int32)
    ki_tab = jnp.asarray([p[1] for p in pairs], jnp.int32)
    return pl.pallas_call(
        functools.partial(_fox_kernel, tq=tq, tk=tq, nsub=nsub),
        grid_spec=pltpu.PrefetchScalarGridSpec(
            num_scalar_prefetch=2,
            grid=(nb, N_HEAD // 2, len(pairs)),
            in_specs=[pl.BlockSpec((1, tq, 128), lambda b, h, s, qt, kt: (b, qt[s], cq + h)),
                      pl.BlockSpec((1, tq, 128), lambda b, h, s, qt, kt: (b, kt[s], ckk + h)),
                      pl.BlockSpec((1, tq, 128), lambda b, h, s, qt, kt: (b, kt[s], cv + h)),
                      pl.BlockSpec((1, 1, 8, tq), lambda b, h, s, qt, kt: (b, h, 0, kt[s]))],
            out_specs=pl.BlockSpec((1, tq, 128), lambda b, h, s, qt, kt: (b, qt[s], h)),
            scratch_shapes=[pltpu.VMEM((2, 1, tq), f32), pltpu.VMEM((2, HEAD + SUM_ROWS, tq), f32)]),
        out_shape=jax.ShapeDtypeStruct((nb, t, 512), f32),
        compiler_params=_params(("parallel", "parallel", "arbitrary"), VMEM_LIMIT),
        name="fox_prompt",
    )(qi_tab, ki_tab, u3, u3, u3, ck)


def _paged_kernel(pt_ref, q_ref, kn_ref, vn_ref, lfn_ref, *refs, g):
    k_refs, v_refs, lf_refs = refs[0:g], refs[g:2 * g], refs[2 * g:3 * g]
    o_ref = refs[3 * g]
    m_ref, l_ref, acc_ref, car_ref = refs[3 * g + 1:]
    step = pl.program_id(1)
    heads = [slice(HEAD * h, HEAD * (h + 1)) for h in range(N_HEAD)]
    q = q_ref[0]

    @pl.when(step == 0)
    def _():
        lane = lax.broadcasted_iota(jnp.int32, acc_ref.shape, 1)
        acc_ref[...] = jnp.where(lane == 0, vn_ref[0], 0.0)
        qk = q * kn_ref[0]
        for h in range(N_HEAD):
            m_ref[h] = jnp.sum(qk[heads[h]], axis=0, keepdims=True)
            l_ref[h] = jnp.ones((1, 1), f32)
        car_ref[...] = lfn_ref[0]

    ii = lax.broadcasted_iota(jnp.int32, (PAGE, PAGE), 0)
    jj = lax.broadcasted_iota(jnp.int32, (PAGE, PAGE), 1)
    later = (ii > jj).astype(bf16)
    carry = car_ref[...]
    biases = []
    for j in range(g):
        lfp = lf_refs[j][0]
        biases.append(_dot_onesr(lfp, later) + carry)
        carry = carry + jnp.sum(lfp, axis=-1, keepdims=True)
    car_ref[...] = carry

    ss = [jnp.concatenate(
        [jnp.sum(k_refs[j][0, heads[h], :] * q[heads[h]], axis=0, keepdims=True) + biases[j][h:h + 1, :]
         for j in range(g)], axis=0) for h in range(N_HEAD)]
    m_olds = [m_ref[h] for h in range(N_HEAD)]
    m_news = [jnp.maximum(m, jnp.max(jnp.max(s, axis=-1, keepdims=True), axis=0, keepdims=True))
              for m, s in zip(m_olds, ss)]
    alphas = [jnp.exp(mo - mn) for mo, mn in zip(m_olds, m_news)]
    ps = [jnp.exp(s - mn) for s, mn in zip(ss, m_news)]
    for h in range(N_HEAD):
        l_ref[h] = alphas[h] * l_ref[h] + jnp.sum(jnp.sum(ps[h], axis=-1, keepdims=True), axis=0, keepdims=True)
        m_ref[h] = m_news[h]
    for h in range(N_HEAD):
        acc = alphas[h] * acc_ref[heads[h], :]
        for j in range(g):
            acc = acc + v_refs[j][0, heads[h], :] * ps[h][j:j + 1, :]
        acc_ref[heads[h], :] = acc

    @pl.when(step == pl.num_programs(1) - 1)
    def _():
        for h in range(N_HEAD):
            o_ref[0, heads[h], :] = jnp.sum(acc_ref[heads[h], :], axis=-1, keepdims=True) / l_ref[h]


def fox_paged(page_table, q, knew, vnew, lfnew, kpool, vpool, lfpool, g):
    nb, npages = page_table.shape
    last = npages - 1
    cur = lambda b, i, pt: (b, 0, 0)
    pool = lambda j: (lambda b, i, pt: (pt[b, last - (i * g + j)], 0, 0))
    col = pl.BlockSpec((1, 512, 1), cur)
    return pl.pallas_call(
        functools.partial(_paged_kernel, g=g),
        grid_spec=pltpu.PrefetchScalarGridSpec(
            num_scalar_prefetch=1,
            grid=(nb, npages // g),
            in_specs=([col, col, col, pl.BlockSpec((1, N_HEAD, 1), cur)]
                      + [pl.BlockSpec((1, 512, PAGE), pool(j)) for j in range(g)]
                      + [pl.BlockSpec((1, 512, PAGE), pool(j)) for j in range(g)]
                      + [pl.BlockSpec((1, N_HEAD, PAGE), pool(j)) for j in range(g)]),
            out_specs=col,
            scratch_shapes=[pltpu.VMEM((N_HEAD, 1, 1), f32), pltpu.VMEM((N_HEAD, 1, 1), f32),
                            pltpu.VMEM((512, PAGE), f32), pltpu.VMEM((N_HEAD, 1), f32)]),
        out_shape=jax.ShapeDtypeStruct((nb, 512, 1), f32),
        compiler_params=_params(("parallel", "arbitrary"), VMEM_LIMIT),
        name="fox_paged",
    )(page_table, q, knew, vnew, lfnew, *([kpool] * g), *([vpool] * g), *([lfpool] * g))


def _merge_update(or_ref, of_ref, ug_ref, wr_ref, wf_ref, wo_ref):
    br = _dot(or_ref[...].astype(bf16), wr_ref[...])
    bf = _dot(of_ref[...].astype(bf16), wf_ref[...])
    ug = ug_ref[...]
    merged = jax.nn.sigmoid(ug[:, :D_MODEL]) * br + jax.nn.sigmoid(ug[:, D_MODEL:]) * bf
    return _dot(merged.astype(bf16), wo_ref[...])


def _merge_kernel(x_ref, or_ref, of_ref, ug_ref, wr_ref, wf_ref, wo_ref, o_ref):
    o_ref[...] = x_ref[...] + _merge_update(or_ref, of_ref, ug_ref, wr_ref, wf_ref, wo_ref)


def _merge_seq_kernel(x_ref, head_ref, or_ref, of_ref, ug_ref, wr_ref, wf_ref, wo_ref, o_ref, *, tiles_per_seq):
    x = _padded_rows(x_ref, head_ref, pl.program_id(0) % tiles_per_seq == 0)
    o_ref[...] = x + _merge_update(or_ref, of_ref, ug_ref, wr_ref, wf_ref, wo_ref)


def merge_out(x, o_r, o_f, u, wr, wf, wo, tm, head=None):
    m = o_r.shape[0]
    row = lambda i: (i, 0)
    cst = lambda i: (0, 0)
    rest = [pl.BlockSpec((tm, 512), row), pl.BlockSpec((tm, 512), row),
            pl.BlockSpec((tm, 2 * D_MODEL), lambda i: (i, COL_G // (2 * D_MODEL))),
            pl.BlockSpec((512, D_MODEL), cst), pl.BlockSpec((512, D_MODEL), cst),
            pl.BlockSpec((D_MODEL, D_MODEL), cst)]
    if head is None:
        body, first, args = _merge_kernel, [pl.BlockSpec((tm, D_MODEL), row)], (x,)
    else:
        tps = (x.shape[1] + ROW0) // tm
        body = functools.partial(_merge_seq_kernel, tiles_per_seq=tps)
        first = [_seq_tile_spec(tm, D_MODEL, tps), pl.BlockSpec((ROW0, D_MODEL), cst)]
        args = (x, head)
    return pl.pallas_call(
        body,
        grid=(m // tm,),
        in_specs=first + rest,
        out_specs=pl.BlockSpec((tm, D_MODEL), row),
        out_shape=jax.ShapeDtypeStruct((m, D_MODEL), f32),
        compiler_params=_params(("parallel",), VMEM_LIMIT),
        name="merge_out",
    )(*args, o_r, o_f, u, wr, wf, wo)


def _ffn_kernel(x_ref, g2_ref, gf_ref, wu_ref, wd_ref, o_ref, h_ref, acc_ref):
    j = pl.program_id(1)
    x = x_ref[...].reshape(o_ref.shape)

    @pl.when(j == 0)
    def _():
        h_ref[...] = _rms(x, g2_ref[...]).astype(bf16)
        acc_ref[...] = jnp.zeros_like(acc_ref)

    hid = jnp.maximum(_dot(h_ref[...], wu_ref[...]), 0.0)
    acc_ref[...] += _dot((hid * hid).astype(bf16), wd_ref[...])

    @pl.when(j == pl.num_programs(1) - 1)
    def _():
        o_ref[...] = _rms(x + acc_ref[...], gf_ref[...])


def ffn_final(x, g2, gf, wu, wd, tm, tf):
    if x.ndim == 2:
        m = x.shape[0]
        x_spec = pl.BlockSpec((tm, D_MODEL), lambda i, j: (i, 0))
    else:
        tps = (x.shape[1] - ROW0) // tm
        m = x.shape[0] * tps * tm
        x_spec = pl.BlockSpec((pl.Element(1), pl.Element(tm), pl.Element(D_MODEL)),
                              lambda i, j: (i // tps, pl.multiple_of(ROW0 + (i % tps) * tm, 128), 0))
    return pl.pallas_call(
        _ffn_kernel,
        grid=(m // tm, D_FF // tf),
        in_specs=[x_spec,
                  pl.BlockSpec((1, D_MODEL), lambda i, j: (0, 0)), pl.BlockSpec((1, D_MODEL), lambda i, j: (0, 0)),
                  pl.BlockSpec((D_MODEL, tf), lambda i, j: (0, j)), pl.BlockSpec((tf, D_MODEL), lambda i, j: (j, 0))],
        out_specs=pl.BlockSpec((tm, D_MODEL), lambda i, j: (i, 0)),
        out_shape=jax.ShapeDtypeStruct((m, D_MODEL), f32),
        scratch_shapes=[pltpu.VMEM((tm, D_MODEL), bf16), pltpu.VMEM((tm, D_MODEL), f32)],
        compiler_params=_params(("parallel", "arbitrary"), VMEM_LIMIT),
        name="ffn_final",
    )(x, g2, gf, wu, wd)


def kernel(x_prompt, x_sample, state_shift, state_wkv, cache_k, cache_v, cache_logf, page_table, meta_tokens,
           norm_mix, w_in, mu_shift, w0, w_decay_up, a0, w_a_up, w_g_up, k_k, k_a, r_k, ln_x_w, ln_x_b, b_forget,
           w_br_rwkv, w_br_fox, w_out, norm_ffn, w_ffn_up, w_ffn_down, norm_final):
    assert norm_mix.shape[0] == 1, "single layer"
    nb, seq, _ = x_prompt.shape
    ns = x_sample.shape[0]
    tp = FRONT_PAD + N_META + seq
    tlen = N_META + seq

    wi = w_in[0]
    wi = jnp.concatenate([wi[:, 0:1792], wi[:, 3328:3336], jnp.zeros((D_MODEL, 248), f32),
                          wi[:, 3336:5384], wi[:, 1792:3328]], axis=1).astype(bf16)
    row2 = lambda a: a.reshape(1, -1)
    lane_head = jnp.arange(512, dtype=jnp.int32) // HEAD
    pr = dict(mu=row2(mu_shift[0]), w0=row2(w0[0]), wd=w_decay_up[0], a0=row2(a0[0]), wa=w_a_up[0], wg=w_g_up[0],
              k_k=row2(k_k[0]), k_a=row2(k_a[0]), r_k=row2(r_k[0]),
              bd=(lane_head[:, None] == lane_head[None, :]).astype(bf16))
    lnw, lnb = row2(ln_x_w[0]), row2(ln_x_b[0])
    g_mix, g_ffn, g_fin = row2(norm_mix[0]), row2(norm_ffn[0]), row2(norm_final)
    wr, wf, wo = w_br_rwkv[0].astype(bf16), w_br_fox[0].astype(bf16), w_out[0].astype(bf16)
    wu, wdn = w_ffn_up[0].astype(bf16), w_ffn_down[0].astype(bf16)

    head = jnp.concatenate([jnp.zeros((FRONT_PAD, D_MODEL), f32), meta_tokens], axis=0)
    u2 = norm_matmul_seq(x_prompt, head, g_mix, wi, TM_PROJ, TN_PROJ)
    u3 = u2.reshape(nb, tp, D_INP)
    us2 = norm_matmul(x_sample.reshape(ns, D_MODEL), g_mix, wi, ns, TN_PROJ)

    r, lw, k2, v, kkn, b, g, bonus = rwkv_prep(u3, None, pr, TM_PREP)
    maps = wkv_chunks(r, lw, k2, v, kkn, b, CHUNKS_PER_STEP)
    maps = maps.reshape(nb * N_HEAD, tp // CHUNK, 2 * CHUNK, 2 * HEAD)
    o_r, s_kv = wkv_serial(maps, g, bonus, lnw, lnb, CHUNKS_PER_STEP)
    new_wkv_p = jnp.swapaxes(s_kv, 1, 2).reshape(1, nb, N_HEAD, HEAD, HEAD)

    us3 = us2.reshape(1, ns, D_INP)
    sp = rwkv_prep(us3, state_shift[0].reshape(1, ns, D_SHIFT), pr, ns)
    rs, lws, ks, vs, kks, bs, gs, bonus_s = sp
    rowify = lambda a: a.reshape(ns, N_HEAD, 1, HEAD)
    s_new, ys = wkv_step(state_wkv[0], rowify(rs), rowify(lws), rowify(ks), rowify(kks), rowify(bs),
                         rowify(vs).reshape(ns, N_HEAD, HEAD, 1))
    ys_hm = jnp.transpose(ys.reshape(ns, N_HEAD, HEAD), (1, 0, 2))[None]
    o_r_s = rwkv_post(ys_hm, gs, bonus_s, lnw, lnb, ns)[0]

    bias_rows = jnp.tile(b_forget[0], nb).reshape(nb * N_HEAD, 1)
    lf_t, c_t = logf_cumsum(u3, bias_rows)
    pad_key = jnp.arange(tp, dtype=jnp.int32)[None, :] < FRONT_PAD
    ck = jnp.where(pad_key, -NEG, c_t * LOG2E).reshape(nb, N_HEAD // 2, 2, tp)
    o_f = fox_prompt(u3, jnp.pad(ck, ((0, 0), (0, 0), (0, 6), (0, 0))), TQ_FOX, NSUB_FOX)

    qs = us2[:, COL_Q:COL_Q + 512] * (HEAD ** -0.5)
    k_s = us2[:, COL_Q + 512:COL_Q + 1024]
    v_s = us2[:, COL_Q + 1024:COL_Q + 1536]
    lf_s = jax.nn.log_sigmoid(us2[:, COL_F:COL_F + N_HEAD] + b_forget[0][None, :])
    n_pool = cache_k.shape[1]
    kpool = jnp.transpose(cache_k[0], (0, 2, 3, 1)).reshape(n_pool, 512, PAGE)
    vpool = jnp.transpose(cache_v[0], (0, 2, 3, 1)).reshape(n_pool, 512, PAGE)
    o_f_s = fox_paged(page_table, qs.reshape(ns, 512, 1), k_s.reshape(ns, 512, 1), v_s.reshape(ns, 512, 1),
                      lf_s.reshape(ns, N_HEAD, 1), kpool, vpool, jnp.swapaxes(cache_logf[0], 1, 2),
                      PAGES_PER_STEP).reshape(ns, 512)

    x1 = merge_out(x_prompt, o_r.reshape(nb * tp, 512), o_f.reshape(nb * tp, 512), u2, wr, wf, wo, TM_MERGE, head)
    y_prompt = ffn_final(x1.reshape(nb, tp, D_MODEL), g_ffn, g_fin, wu, wdn, TM_FFN, TF_FFN).reshape(nb, seq, D_MODEL)
    x1s = merge_out(x_sample.reshape(ns, D_MODEL), o_r_s, o_f_s, us2, wr, wf, wo, ns)
    ysamp = ffn_final(x1s, g_ffn, g_fin, wu, wdn, ns, TF_FFN)

    y_sample = ysamp.reshape(ns, 1, D_MODEL)
    new_shift_p = u3[:, tp - 1, :D_SHIFT][None]
    k_p = u3[:, FRONT_PAD:, COL_Q + 512:COL_Q + 1024].reshape(1, nb, tlen, N_HEAD, HEAD)
    v_p = u3[:, FRONT_PAD:, COL_Q + 1024:COL_Q + 1536].reshape(1, nb, tlen, N_HEAD, HEAD)
    lf_p = jnp.transpose(lf_t.reshape(nb, N_HEAD, tp), (0, 2, 1))[:, FRONT_PAD:][None]
    return (y_prompt, y_sample, new_shift_p, new_wkv_p, k_p, v_p, lf_p,
            us2[:, :D_SHIFT][None], s_new[None], k_s.reshape(1, ns, 1, N_HEAD, HEAD),
            v_s.reshape(1, ns, 1, N_HEAD, HEAD), lf_s.reshape(1, ns, 1, N_HEAD))
```
